```python
import jax, jax.numpy as jnp
from jax import lax
import numpy as np


D_MODEL = 2048
BATCH = 4
SEQ = 4096
DEPTH = 1

N_META = 16
BLOCK = 128
PAD_LEFT = BLOCK - N_META
HEAD_DIM = 64
MIX_WIDTH = D_MODEL
RWKV_WIDTH = MIX_WIDTH // 2
RWKV_HEADS = RWKV_WIDTH // HEAD_DIM
ATTN_WIDTH = MIX_WIDTH - RWKV_WIDTH
ATTN_Q_HEADS = ATTN_WIDTH // HEAD_DIM
ATTN_KV_HEADS = max(1, ATTN_Q_HEADS // 8)
KV_WIDTH = ATTN_KV_HEADS * HEAD_DIM
WINDOW = 128
ATTN_SCALE = HEAD_DIM ** -0.5
MASK_VALUE = -1e30
DECAY_LORA = 64
AAA_LORA = 64
GATE_LORA = 160
RMS_EPS = 1e-6
GN_EPS = 64e-5
ATTN_COLS = ATTN_WIDTH + 2 * KV_WIDTH
RWKV_COLS = 3 * RWKV_WIDTH + DECAY_LORA + AAA_LORA + GATE_LORA
IN_COLS = ATTN_COLS + RWKV_COLS
PEER_HEADS = 8
N_KEYS = 128
N_EXPERTS = N_KEYS * N_KEYS
PEER_TOPK = 16
D_KEY = 256
PEER_BLOCK = 128

kernel_name = 'hybrid_rwkv7_swa_sink_peer'


def rms_norm(x, g):
    xf = x.astype(jnp.float32)
    y = xf * lax.rsqrt(jnp.mean(xf * xf, axis=-1, keepdims=True) + RMS_EPS)
    return (y * g.astype(jnp.float32)).astype(x.dtype)


def split_cols(p, sizes):
    out, o = [], 0
    for s in sizes:
        out.append(p[..., o:o + s])
        o += s
    return out


def token_shift(p, mu):
    p_prev = jnp.pad(p, ((0, 0), (1, 0), (0, 0)))[:, :-1]
    return p + mu.astype(p.dtype) * (p_prev - p)


def rwkv7_time_mix(pr, pk, pv, pw, pa, pg, w0, w_up, a0, a_up, g_up, k_k, k_a, r_k, gn_w, gn_b):
    B, L, C = pr.shape
    H, N = RWKV_HEADS, HEAD_DIM
    f32 = jnp.float32
    dt = pr.dtype
    w_log = -jax.nn.softplus(-(w0 + jnp.tanh(pw) @ w_up).astype(f32)) - 0.5
    decay = jnp.exp(-jnp.exp(w_log))
    a = jax.nn.sigmoid((a0 + pa @ a_up).astype(f32))
    g = (jax.nn.sigmoid(pg) @ g_up).astype(f32)
    kf = pk.astype(f32)
    kk = (kf * k_k.astype(f32)).reshape(B, L, H, N)
    kk = kk / jnp.maximum(jnp.sqrt(jnp.sum(kk * kk, axis=-1, keepdims=True)), 1e-12)
    k = (kf * (1.0 + (a - 1.0) * k_a.astype(f32))).reshape(B, L, H, N)
    r = pr.astype(f32).reshape(B, L, H, N)
    v = pv.astype(f32).reshape(B, L, H, N)
    a = a.reshape(B, L, H, N)
    decay = decay.reshape(B, L, H, N)

    def step(S, inp):
        r_t, w_t, k_t, v_t, kk_t, a_t = inp
        sa = jnp.einsum('bhij,bhj->bhi', S, -kk_t)
        S = (S * w_t[:, :, None, :] + sa[..., :, None] * (kk_t * a_t)[..., None, :]
             + v_t[..., :, None] * k_t[..., None, :])
        y_t = jnp.einsum('bhij,bhj->bhi', S, r_t)
        return S, y_t

    xs = tuple(jnp.moveaxis(t, 1, 0) for t in (r, decay, k, v, kk, a))
    S0 = jnp.zeros((B, H, N, N), f32)
    _, y = lax.scan(step, S0, xs)
    y = jnp.moveaxis(y, 0, 1)
    mu = jnp.mean(y, axis=-1, keepdims=True)
    var = jnp.mean(jnp.square(y - mu), axis=-1, keepdims=True)
    yn = ((y - mu) * lax.rsqrt(var + GN_EPS)).reshape(B, L, C)
    yn = yn * gn_w.astype(f32) + gn_b.astype(f32)
    bonus = jnp.sum(r * k * r_k.astype(f32), axis=-1, keepdims=True) * v
    return ((yn + bonus.reshape(B, L, C)) * g).astype(dt)


def sliding_window_sink_attention(q, k, v, q_gain, k_gain, sinks):
    B, LP, _ = q.shape
    NB = LP // BLOCK
    KVH = ATTN_KV_HEADS
    G = ATTN_Q_HEADS // KVH
    f32 = jnp.float32
    q = rms_norm(q.reshape(B, LP, ATTN_Q_HEADS, HEAD_DIM), q_gain).reshape(B, NB, BLOCK, KVH, G, HEAD_DIM)
    k = rms_norm(k.reshape(B, LP, KVH, HEAD_DIM), k_gain).reshape(B, NB, BLOCK, KVH, HEAD_DIM)
    v = v.reshape(B, NB, BLOCK, KVH, HEAD_DIM)

    def with_prev(t):
        prev = jnp.pad(t, ((0, 0), (1, 0), (0, 0), (0, 0), (0, 0)))[:, :-1]
        return jnp.concatenate([prev, t], axis=2)

    kw, vw = with_prev(k), with_prev(v)
    s = jnp.einsum('bnqkgd,bnskd->bnkgqs', q, kw, preferred_element_type=f32) * ATTN_SCALE
    blk = jnp.arange(NB)[:, None, None]
    qpos = blk * BLOCK + jnp.arange(BLOCK)[None, :, None]
    kpos = (blk - 1) * BLOCK + jnp.arange(2 * BLOCK)[None, None, :]
    dist = qpos - kpos
    mask = (dist >= 0) & (dist < WINDOW) & (kpos >= PAD_LEFT)
    s = jnp.where(mask[None, :, None, None], s, MASK_VALUE)
    sink = jnp.broadcast_to(sinks.astype(f32).reshape(1, 1, KVH, G, 1, 1), s.shape[:-1] + (1,))
    p = jax.nn.softmax(jnp.concatenate([s, sink], axis=-1), axis=-1)[..., :-1]
    o = jnp.einsum('bnkgqs,bnskd->bnqkgd', p.astype(vw.dtype), vw)
    return o.reshape(B, LP, ATTN_WIDTH)


def peer_ffn(x, peer_query, peer_sub_keys, peer_down, peer_up):
    B, LP, D = x.shape
    T = B * LP
    K = PEER_TOPK
    xt = x.reshape(T, D)
    q = (xt @ peer_query).reshape(T, PEER_HEADS, 2, D_KEY // 2)
    scores = jnp.einsum('thcd,hcnd->thcn', q, peer_sub_keys, preferred_element_type=jnp.float32)
    s_half, i_half = lax.top_k(scores, K)
    cand_s = (s_half[:, :, 0, :, None] + s_half[:, :, 1, None, :]).reshape(T, PEER_HEADS, K * K)
    cand_i = (i_half[:, :, 0, :, None] * N_KEYS + i_half[:, :, 1, None, :]).reshape(T, PEER_HEADS, K * K)
    top_s, top_pos = lax.top_k(cand_s, K)
    idx = jnp.take_along_axis(cand_i, top_pos, axis=-1)
    gate = jax.nn.softmax(top_s, axis=-1).astype(x.dtype)
    nblk = T // PEER_BLOCK
    xb = xt.reshape(nblk, PEER_BLOCK, D)
    ib = idx.reshape(nblk, PEER_BLOCK, PEER_HEADS * K)
    gb = gate.reshape(nblk, PEER_BLOCK, PEER_HEADS * K)

    def one_block(args):
        xc, ic, gc = args
        u = jnp.take(peer_down, ic, axis=0)
        hcur = jax.nn.gelu(jnp.einsum('ped,pd->pe', u, xc), approximate=False)
        vv = jnp.take(peer_up, ic, axis=0)
        return jnp.einsum('pe,ped->pd', hcur * gc, vv)

    out = lax.map(one_block, (xb, ib, gb))
    return out.reshape(B, LP, D)


def hybrid_layer(h, valid, norm1_g, w_in, shift_mu, w0, w_up, a0, a_up, g_up, k_k, k_a, r_k,
                 gn_w, gn_b, q_gain, k_gain, sinks, w_out, norm2_g,
                 peer_query, peer_sub_keys, peer_down, peer_up):
    u = rms_norm(h, norm1_g)
    p = u @ w_in
    p_attn, p_rwkv = p[..., :ATTN_COLS], p[..., ATTN_COLS:]
    q, k, v = split_cols(p_attn, (ATTN_WIDTH, KV_WIDTH, KV_WIDTH))
    p_rwkv = token_shift(p_rwkv, shift_mu)
    pr, pk, pv, pw, pa, pg = split_cols(
        p_rwkv, (RWKV_WIDTH, RWKV_WIDTH, RWKV_WIDTH, DECAY_LORA, AAA_LORA, GATE_LORA))
    y_rwkv = rwkv7_time_mix(pr, pk, pv, pw, pa, pg, w0, w_up, a0, a_up, g_up, k_k, k_a, r_k, gn_w, gn_b)
    y_attn = sliding_window_sink_attention(q, k, v, q_gain, k_gain, sinks)
    h = h + jnp.concatenate([y_rwkv, y_attn], axis=-1) @ w_out
    h = h + peer_ffn(rms_norm(h, norm2_g), peer_query, peer_sub_keys, peer_down, peer_up)
    return jnp.where(valid[None, :, None], h, jnp.zeros_like(h))


def setup_inputs(seed: int = 0) -> dict:
    key = jax.random.key(seed)
    ks = jax.random.split(key, 26)
    f32 = jnp.float32

    def nrm(k, shape, scale):
        return jax.random.normal(k, shape, f32) * scale

    return {
        'x': nrm(ks[0], (BATCH, SEQ, D_MODEL), 1.0),
        'meta_tokens': nrm(ks[1], (N_META, D_MODEL), 1.0),
        'norm1_g': 1.0 + nrm(ks[2], (DEPTH, D_MODEL), 0.02),
        'w_in': nrm(ks[3], (DEPTH, D_MODEL, IN_COLS), D_MODEL ** -0.5),
        'shift_mu': jax.random.uniform(ks[4], (DEPTH, RWKV_COLS), f32),
        'w0': jax.random.uniform(ks[5], (DEPTH, RWKV_WIDTH), f32, -6.5, -1.5),
        'w_up': nrm(ks[6], (DEPTH, DECAY_LORA, RWKV_WIDTH), 0.1 * DECAY_LORA ** -0.5),
        'a0': nrm(ks[7], (DEPTH, RWKV_WIDTH), 0.1),
        'a_up': nrm(ks[8], (DEPTH, AAA_LORA, RWKV_WIDTH), 0.5 * AAA_LORA ** -0.5),
        'g_up': nrm(ks[9], (DEPTH, GATE_LORA, RWKV_WIDTH), GATE_LORA ** -0.5),
        'k_k': 0.85 + nrm(ks[10], (DEPTH, RWKV_WIDTH), 0.02),
        'k_a': 1.0 + nrm(ks[11], (DEPTH, RWKV_WIDTH), 0.02),
        'r_k': nrm(ks[12], (DEPTH, RWKV_HEADS, HEAD_DIM), 0.1),
        'gn_w': 1.0 + nrm(ks[13], (DEPTH, RWKV_WIDTH), 0.02),
        'gn_b': nrm(ks[14], (DEPTH, RWKV_WIDTH), 0.02),
        'q_gain': 1.0 + nrm(ks[15], (DEPTH, HEAD_DIM), 0.02),
        'k_gain': 1.0 + nrm(ks[16], (DEPTH, HEAD_DIM), 0.02),
        'sinks': nrm(ks[17], (DEPTH, ATTN_Q_HEADS), 0.5),
        'w_out': nrm(ks[18], (DEPTH, MIX_WIDTH, D_MODEL), MIX_WIDTH ** -0.5),
        'norm2_g': 1.0 + nrm(ks[19], (DEPTH, D_MODEL), 0.02),
        'peer_query': nrm(ks[20], (DEPTH, D_MODEL, PEER_HEADS * D_KEY), D_MODEL ** -0.5),
        'peer_sub_keys': nrm(ks[21], (DEPTH, PEER_HEADS, 2, N_KEYS, D_KEY // 2), (D_KEY // 2) ** -0.5),
        'peer_down': nrm(ks[22], (DEPTH, N_EXPERTS, D_MODEL), D_MODEL ** -0.5),
        'peer_up': nrm(ks[23], (DEPTH, N_EXPERTS, D_MODEL), PEER_HEADS ** -0.5),
    }


def reference(x, meta_tokens, norm1_g, w_in, shift_mu, w0, w_up, a0, a_up, g_up, k_k, k_a, r_k,
              gn_w, gn_b, q_gain, k_gain, sinks, w_out, norm2_g,
              peer_query, peer_sub_keys, peer_down, peer_up):
    B, S, D = x.shape
    meta = jnp.broadcast_to(meta_tokens.astype(x.dtype)[None], (B, N_META, D))
    h = jnp.concatenate([jnp.zeros((B, PAD_LEFT, D), x.dtype), meta, x], axis=1)
    valid = jnp.arange(h.shape[1]) >= PAD_LEFT
    for i in range(DEPTH):
        h = hybrid_layer(h, valid, norm1_g[i], w_in[i], shift_mu[i], w0[i], w_up[i], a0[i], a_up[i],
                         g_up[i], k_k[i], k_a[i], r_k[i], gn_w[i], gn_b[i], q_gain[i], k_gain[i],
                         sinks[i], w_out[i], norm2_g[i], peer_query[i], peer_sub_keys[i],
                         peer_down[i], peer_up[i])
    return h[:, BLOCK:]
```

```python
import functools

import jax
import jax.numpy as jnp
from jax import lax
from jax.experimental import pallas as pl
from jax.experimental.pallas import tpu as pltpu

N_META = 16
BLOCK = 128
PAD_LEFT = BLOCK - N_META
HEAD_DIM = 64
RWKV_WIDTH = 1024
RWKV_HEADS = RWKV_WIDTH // HEAD_DIM
ATTN_WIDTH = 1024
ATTN_Q_HEADS = ATTN_WIDTH // HEAD_DIM
ATTN_KV_HEADS = 2
KV_WIDTH = ATTN_KV_HEADS * HEAD_DIM
WINDOW = 128
ATTN_SCALE = HEAD_DIM ** -0.5
MASK_VALUE = -1e30
DECAY_LORA = 64
AAA_LORA = 64
GATE_LORA = 160
RMS_EPS = 1e-6
GN_EPS = 64e-5
ATTN_COLS = ATTN_WIDTH + 2 * KV_WIDTH
PEER_HEADS = 8
N_KEYS = 128
PEER_TOPK = 16
D_KEY = 256
PEER_BLOCK = 128

VMEM_LIMIT_BYTES = 48 * 1024 * 1024


def _norm_matmul_kernel(x_ref, g_ref, w_ref, o_ref):
    x = x_ref[...]
    ms = jnp.mean(x * x, axis=-1, keepdims=True)
    u = x * lax.rsqrt(ms + RMS_EPS) * g_ref[...]
    o_ref[...] = jnp.dot(u.astype(jnp.bfloat16), w_ref[...], preferred_element_type=jnp.float32)


def norm_matmul(x, g, w, tm, tn):
    m, k = x.shape
    n = w.shape[1]
    return pl.pallas_call(
        _norm_matmul_kernel,
        grid=(n // tn, m // tm),
        in_specs=[
            pl.BlockSpec((tm, k), lambda j, i: (i, 0)),
            pl.BlockSpec((1, k), lambda j, i: (0, 0)),
            pl.BlockSpec((k, tn), lambda j, i: (0, j)),
        ],
        out_specs=pl.BlockSpec((tm, tn), lambda j, i: (i, j)),
        out_shape=jax.ShapeDtypeStruct((m, n), jnp.float32),
        compiler_params=pltpu.CompilerParams(
            dimension_semantics=("arbitrary", "arbitrary"), vmem_limit_bytes=VMEM_LIMIT_BYTES),
        name="norm_matmul",
    )(x, g.reshape(1, k), w)


def _matmul_residual_kernel(x_ref, w_ref, r_ref, o_ref):
    o_ref[...] = r_ref[...] + jnp.dot(
        x_ref[...].astype(jnp.bfloat16), w_ref[...], preferred_element_type=jnp.float32)


def matmul_residual(x, w, r, tm, tn):
    m, k = x.shape
    n = w.shape[1]
    return pl.pallas_call(
        _matmul_residual_kernel,
        grid=(n // tn, m // tm),
        in_specs=[
            pl.BlockSpec((tm, k), lambda j, i: (i, 0)),
            pl.BlockSpec((k, tn), lambda j, i: (0, j)),
            pl.BlockSpec((tm, tn), lambda j, i: (i, j)),
        ],
        out_specs=pl.BlockSpec((tm, tn), lambda j, i: (i, j)),
        out_shape=jax.ShapeDtypeStruct((m, n), jnp.float32),
        compiler_params=pltpu.CompilerParams(
            dimension_semantics=("arbitrary", "arbitrary"), vmem_limit_bytes=VMEM_LIMIT_BYTES),
        name="matmul_residual",
    )(x, w, r)


def _split_cols(p, sizes):
    out, o = [], 0
    for s in sizes:
        out.append(p[..., o:o + s])
        o += s
    return out


def _rwkv7_time_mix(pr, pk, pv, pw, pa, pg, w0, w_up, a0, a_up, g_up, k_k, k_a, r_k, gn_w, gn_b):
    B, L, C = pr.shape
    H, N = RWKV_HEADS, HEAD_DIM
    w_log = -jax.nn.softplus(-(w0 + jnp.tanh(pw) @ w_up)) - 0.5
    decay = jnp.exp(-jnp.exp(w_log))
    a = jax.nn.sigmoid(a0 + pa @ a_up)
    g = jax.nn.sigmoid(pg) @ g_up
    kk = (pk * k_k).reshape(B, L, H, N)
    kk = kk / jnp.maximum(jnp.sqrt(jnp.sum(kk * kk, axis=-1, keepdims=True)), 1e-12)
    k = (pk * (1.0 + (a - 1.0) * k_a)).reshape(B, L, H, N)
    r = pr.reshape(B, L, H, N)
    v = pv.reshape(B, L, H, N)
    a = a.reshape(B, L, H, N)
    decay = decay.reshape(B, L, H, N)

    def step(S, inp):
        r_t, w_t, k_t, v_t, kk_t, a_t = inp
        sa = jnp.einsum('bhij,bhj->bhi', S, -kk_t)
        S = (S * w_t[:, :, None, :] + sa[..., :, None] * (kk_t * a_t)[..., None, :]
             + v_t[..., :, None] * k_t[..., None, :])
        y_t = jnp.einsum('bhij,bhj->bhi', S, r_t)
        return S, y_t

    xs = tuple(jnp.moveaxis(t, 1, 0) for t in (r, decay, k, v, kk, a))
    S0 = jnp.zeros((B, H, N, N), jnp.float32)
    _, y = lax.scan(step, S0, xs)
    y = jnp.moveaxis(y, 0, 1)
    mu = jnp.mean(y, axis=-1, keepdims=True)
    var = jnp.mean(jnp.square(y - mu), axis=-1, keepdims=True)
    yn = ((y - mu) * lax.rsqrt(var + GN_EPS)).reshape(B, L, C)
    yn = yn * gn_w + gn_b
    bonus = jnp.sum(r * k * r_k, axis=-1, keepdims=True) * v
    return (yn + bonus.reshape(B, L, C)) * g


def _rms(x, g):
    return x * lax.rsqrt(jnp.mean(x * x, axis=-1, keepdims=True) + RMS_EPS) * g


def _attention(q, k, v, q_gain, k_gain, sinks):
    B, LP, _ = q.shape
    NB = LP // BLOCK
    KVH = ATTN_KV_HEADS
    G = ATTN_Q_HEADS // KVH
    q = _rms(q.reshape(B, LP, ATTN_Q_HEADS, HEAD_DIM), q_gain).reshape(B, NB, BLOCK, KVH, G, HEAD_DIM)
    k = _rms(k.reshape(B, LP, KVH, HEAD_DIM), k_gain).reshape(B, NB, BLOCK, KVH, HEAD_DIM)
    v = v.reshape(B, NB, BLOCK, KVH, HEAD_DIM)

    def with_prev(t):
        prev = jnp.pad(t, ((0, 0), (1, 0), (0, 0), (0, 0), (0, 0)))[:, :-1]
        return jnp.concatenate([prev, t], axis=2)

    kw, vw = with_prev(k), with_prev(v)
    s = jnp.einsum('bnqkgd,bnskd->bnkgqs', q, kw, preferred_element_type=jnp.float32) * ATTN_SCALE
    blk = jnp.arange(NB)[:, None, None]
    qpos = blk * BLOCK + jnp.arange(BLOCK)[None, :, None]
    kpos = (blk - 1) * BLOCK + jnp.arange(2 * BLOCK)[None, None, :]
    dist = qpos - kpos
    mask = (dist >= 0) & (dist < WINDOW) & (kpos >= PAD_LEFT)
    s = jnp.where(mask[None, :, None, None], s, MASK_VALUE)
    sink = jnp.broadcast_to(sinks.reshape(1, 1, KVH, G, 1, 1), s.shape[:-1] + (1,))
    p = jax.nn.softmax(jnp.concatenate([s, sink], axis=-1), axis=-1)[..., :-1]
    o = jnp.einsum('bnkgqs,bnskd->bnqkgd', p, vw)
    return o.reshape(B, LP, ATTN_WIDTH)


def _peer_select_and_mix(xt, q, peer_sub_keys, peer_down, peer_up):
    T, D = xt.shape
    K = PEER_TOPK
    q = q.reshape(T, PEER_HEADS, 2, D_KEY // 2)
    scores = jnp.einsum('thcd,hcnd->thcn', q, peer_sub_keys, preferred_element_type=jnp.float32)
    s_half, i_half = lax.top_k(scores, K)
    cand_s = (s_half[:, :, 0, :, None] + s_half[:, :, 1, None, :]).reshape(T, PEER_HEADS, K * K)
    cand_i = (i_half[:, :, 0, :, None] * N_KEYS + i_half[:, :, 1, None, :]).reshape(T, PEER_HEADS, K * K)
    top_s, top_pos = lax.top_k(cand_s, K)
    idx = jnp.take_along_axis(cand_i, top_pos, axis=-1)
    gate = jax.nn.softmax(top_s, axis=-1)
    nblk = T // PEER_BLOCK
    xb = xt.reshape(nblk, PEER_BLOCK, D)
    ib = idx.reshape(nblk, PEER_BLOCK, PEER_HEADS * K)
    gb = gate.reshape(nblk, PEER_BLOCK, PEER_HEADS * K)

    def one_block(args):
        xc, ic, gc = args
        u = jnp.take(peer_down, ic, axis=0)
        hcur = jax.nn.gelu(jnp.einsum('ped,pd->pe', u, xc), approximate=False)
        vv = jnp.take(peer_up, ic, axis=0)
        return jnp.einsum('pe,ped->pd', hcur * gc, vv)

    return lax.map(one_block, (xb, ib, gb)).reshape(T, D)


def kernel(x, meta_tokens, norm1_g, w_in, shift_mu, w0, w_up, a0, a_up, g_up, k_k, k_a, r_k, gn_w, gn_b,
           q_gain, k_gain, sinks, w_out, norm2_g, peer_query, peer_sub_keys, peer_down, peer_up):
    B, S, D = x.shape
    LP = S + BLOCK
    T = B * LP
    meta = jnp.broadcast_to(meta_tokens[None], (B, N_META, D))
    h = jnp.concatenate([jnp.zeros((B, PAD_LEFT, D), x.dtype), meta, x], axis=1)
    valid = jnp.arange(LP) >= PAD_LEFT
    ht = h.reshape(T, D)

    in_cols = w_in.shape[-1]
    in_cols_padded = -(-in_cols // 256) * 256
    w_in_b = jnp.pad(w_in[0], ((0, 0), (0, in_cols_padded - in_cols))).astype(jnp.bfloat16)
    p = norm_matmul(ht, norm1_g[0], w_in_b, 512, in_cols_padded // 2)
    p = p.reshape(B, LP, in_cols_padded)[..., :in_cols]

    p_attn, p_rwkv = p[..., :ATTN_COLS], p[..., ATTN_COLS:]
    q, k, v = _split_cols(p_attn, (ATTN_WIDTH, KV_WIDTH, KV_WIDTH))
    p_prev = jnp.pad(p_rwkv, ((0, 0), (1, 0), (0, 0)))[:, :-1]
    p_rwkv = p_rwkv + shift_mu[0] * (p_prev - p_rwkv)
    pr, pk, pv, pw, pa, pg = _split_cols(
        p_rwkv, (RWKV_WIDTH, RWKV_WIDTH, RWKV_WIDTH, DECAY_LORA, AAA_LORA, GATE_LORA))
    y_rwkv = _rwkv7_time_mix(pr, pk, pv, pw, pa, pg, w0[0], w_up[0], a0[0], a_up[0], g_up[0], k_k[0], k_a[0],
                             r_k[0], gn_w[0], gn_b[0])
    y_attn = _attention(q, k, v, q_gain[0], k_gain[0], sinks[0])
    mix = jnp.concatenate([y_rwkv, y_attn], axis=-1).reshape(T, D)
    h2 = matmul_residual(mix, w_out[0].astype(jnp.bfloat16), ht, 512, 1024)

    pq = norm_matmul(h2, norm2_g[0], peer_query[0].astype(jnp.bfloat16), 512, 1024)
    xn = _rms(h2, norm2_g[0])
    out = h2 + _peer_select_and_mix(xn, pq, peer_sub_keys[0], peer_down[0], peer_up[0])
    out = out.reshape(B, LP, D)
    out = jnp.where(valid[None, :, None], out, jnp.zeros_like(out))
    return out[:, BLOCK:]
```

```python
import functools

import jax
import jax.numpy as jnp
from jax import lax
from jax.experimental import pallas as pl
from jax.experimental.pallas import tpu as pltpu

N_META = 16
BLOCK = 128
PAD_LEFT = BLOCK - N_META
HEAD_DIM = 64
RWKV_WIDTH = 1024
RWKV_HEADS = RWKV_WIDTH // HEAD_DIM
ATTN_WIDTH = 1024
ATTN_Q_HEADS = ATTN_WIDTH // HEAD_DIM
ATTN_KV_HEADS = 2
KV_WIDTH = ATTN_KV_HEADS * HEAD_DIM
WINDOW = 128
ATTN_SCALE = HEAD_DIM ** -0.5
MASK_VALUE = -1e30
DECAY_LORA = 64
AAA_LORA = 64
GATE_LORA = 160
RMS_EPS = 1e-6
GN_EPS = 64e-5
ATTN_COLS = ATTN_WIDTH + 2 * KV_WIDTH
PEER_HEADS = 8
N_KEYS = 128
PEER_TOPK = 16
D_KEY = 256
PEER_BLOCK = 128

VMEM_LIMIT_BYTES = 48 * 1024 * 1024


def _norm_matmul_kernel(x_ref, g_ref, w_ref, o_ref):
    x = x_ref[...]
    ms = jnp.mean(x * x, axis=-1, keepdims=True)
    u = x * lax.rsqrt(ms + RMS_EPS) * g_ref[...]
    o_ref[...] = jnp.dot(u.astype(jnp.bfloat16), w_ref[...], preferred_element_type=jnp.float32)


def norm_matmul(x, g, w, tm, tn):
    m, k = x.shape
    n = w.shape[1]
    return pl.pallas_call(
        _norm_matmul_kernel,
        grid=(n // tn, m // tm),
        in_specs=[
            pl.BlockSpec((tm, k), lambda j, i: (i, 0)),
            pl.BlockSpec((1, k), lambda j, i: (0, 0)),
            pl.BlockSpec((k, tn), lambda j, i: (0, j)),
        ],
        out_specs=pl.BlockSpec((tm, tn), lambda j, i: (i, j)),
        out_shape=jax.ShapeDtypeStruct((m, n), jnp.float32),
        compiler_params=pltpu.CompilerParams(
            dimension_semantics=("arbitrary", "arbitrary"), vmem_limit_bytes=VMEM_LIMIT_BYTES),
        name="norm_matmul",
    )(x, g.reshape(1, k), w)


def _matmul_residual_kernel(x_ref, w_ref, r_ref, o_ref):
    o_ref[...] = r_ref[...] + jnp.dot(
        x_ref[...].astype(jnp.bfloat16), w_ref[...], preferred_element_type=jnp.float32)


def matmul_residual(x, w, r, tm, tn):
    m, k = x.shape
    n = w.shape[1]
    return pl.pallas_call(
        _matmul_residual_kernel,
        grid=(n // tn, m // tm),
        in_specs=[
            pl.BlockSpec((tm, k), lambda j, i: (i, 0)),
            pl.BlockSpec((k, tn), lambda j, i: (0, j)),
            pl.BlockSpec((tm, tn), lambda j, i: (i, j)),
        ],
        out_specs=pl.BlockSpec((tm, tn), lambda j, i: (i, j)),
        out_shape=jax.ShapeDtypeStruct((m, n), jnp.float32),
        compiler_params=pltpu.CompilerParams(
            dimension_semantics=("arbitrary", "arbitrary"), vmem_limit_bytes=VMEM_LIMIT_BYTES),
        name="matmul_residual",
    )(x, w, r)


RWKV_CHUNK = 64
LANES = 128
HEADS_PER_TILE = LANES // HEAD_DIM
RWKV_TILES = RWKV_WIDTH // LANES
LORA_PAD = 512
_HI = lax.Precision.HIGHEST


def _dot(a, b, dims, precision=None):
    return lax.dot_general(a, b, (dims, ((), ())), precision=precision, preferred_element_type=jnp.float32)


def _mm(a, b, precision=None):
    return _dot(a, b, ((1,), (0,)), precision)


def _mm_nt(a, b, precision=None):
    return _dot(a, b, ((1,), (1,)), precision)


def _mm_tn(a, b, precision=None):
    return _dot(a, b, ((0,), (0,)), precision)


def _iota2(shape, axis):
    return lax.broadcasted_iota(jnp.int32, shape, axis)


def _sigmoid(x):
    return 1.0 / (1.0 + jnp.exp(-x))


def _token_shift(x, prev_ref, mu):
    rolled = pltpu.roll(x, 1, 0)
    prev = jnp.where(_iota2(x.shape, 0) == 0, prev_ref[...], rolled)
    prev_ref[...] = x[x.shape[0] - 1:, :]
    return x + mu * (prev - x)


def _rwkv_kernel(xr_ref, xk_ref, xv_ref, xl_ref, mu_ref, mul_ref, vec_ref, wup_ref, aup_ref, gup_ref,
                 o_ref, s_ref, pr_ref, pk_ref, pv_ref, pl_ref):
    C = RWKV_CHUNK
    f32 = jnp.float32

    @pl.when(pl.program_id(1) == 0)
    def _():
        s_ref[...] = jnp.zeros_like(s_ref)
        pr_ref[...] = jnp.zeros_like(pr_ref)
        pk_ref[...] = jnp.zeros_like(pk_ref)
        pv_ref[...] = jnp.zeros_like(pv_ref)
        pl_ref[...] = jnp.zeros_like(pl_ref)

    r = _token_shift(xr_ref[...], pr_ref, mu_ref[0:1, :])
    kraw = _token_shift(xk_ref[...], pk_ref, mu_ref[1:2, :])
    v = _token_shift(xv_ref[...], pv_ref, mu_ref[2:3, :])
    xl = _token_shift(xl_ref[...], pl_ref, mul_ref[...])

    w0, a0, k_k, k_a = vec_ref[0:1, :], vec_ref[1:2, :], vec_ref[2:3, :], vec_ref[3:4, :]
    r_k, gn_w, gn_b = vec_ref[4:5, :], vec_ref[5:6, :], vec_ref[6:7, :]

    x_wa = xl[:, :LANES]
    wl = w0 + _mm(jnp.tanh(x_wa).astype(jnp.bfloat16), wup_ref[...])
    z = -wl
    softplus = jnp.maximum(z, 0.0) + jnp.log(1.0 + jnp.exp(-jnp.abs(z)))
    lw = -jnp.exp(-softplus - 0.5)
    a = _sigmoid(a0 + _mm(x_wa.astype(jnp.bfloat16), aup_ref[...]))
    g = _mm(_sigmoid(xl[:, LANES:LANES + 256]).astype(jnp.bfloat16), gup_ref[...])

    tri = (_iota2((C, C), 0) >= _iota2((C, C), 1)).astype(f32)
    cl = _mm(tri, lw, _HI)
    e_pos = jnp.exp(cl)
    e_excl = jnp.exp(cl - lw)
    e_neg = 1.0 / e_pos
    e_end = e_pos[C - 1:, :]
    e_tail = e_end * e_neg

    lane = _iota2((LANES, LANES), 1)
    row = _iota2((LANES, LANES), 0)
    seg = ((lane >= HEAD_DIM) == (row >= HEAD_DIM)).astype(f32)
    eye = (lane == row).astype(f32)

    gr = _iota2((C, 2 * C), 0)
    gc = _iota2((C, 2 * C), 1)
    gcm = jnp.where(gc >= C, gc - C, gc)
    m_a_strict = ((gc < C) & (gcm < gr)).astype(f32)
    m_b_strict = ((gc >= C) & (gcm < gr)).astype(f32)
    m_incl = (gcm <= gr).astype(f32)
    eye_c = (_iota2((C, C), 0) == _iota2((C, C), 1)).astype(f32)
    lane_c = _iota2((C, LANES), 1)

    outs = []
    for j in range(RWKV_TILES):
        sl = slice(j * LANES, (j + 1) * LANES)
        kr = kraw[:, sl]
        kkr = kr * k_k[:, sl]
        ss = _mm(kkr * kkr, seg, _HI)
        kk = kkr / jnp.maximum(jnp.sqrt(ss), 1e-12)
        a_j = a[:, sl]
        k2 = kr * (1.0 + (a_j - 1.0) * k_a[:, sl])
        r_j = r[:, sl]
        v_j = v[:, sl]
        kka = kk * a_j

        x_cat = jnp.concatenate([-kk * e_excl[:, sl], r_j * e_pos[:, sl]], axis=0)
        z_cat = jnp.concatenate([kka * e_neg[:, sl], k2 * e_neg[:, sl]], axis=0)
        zt_cat = jnp.concatenate([kka * e_tail[:, sl], k2 * e_tail[:, sl]], axis=0)
        s_j = s_ref[j]
        m1 = _mm(x_cat, s_j, _HI)
        ua0, y0 = m1[:C], m1[C:]
        vv = jnp.concatenate([v_j, v_j], axis=0)

        u_heads, y_heads = [], []
        for hl in range(HEADS_PER_TILE):
            hmask = ((lane_c >= HEAD_DIM) == (hl == 1)).astype(f32)
            xh = x_cat * jnp.concatenate([hmask, hmask], axis=0)
            gram = _mm_nt(xh, z_cat, _HI)
            g_top, g_bot = gram[:C], gram[C:]
            a_strict = (g_top * m_a_strict)[:, :C]
            rhs = ua0 + _mm(g_top * m_b_strict, vv, _HI)
            t_inv = eye_c + a_strict
            pw = a_strict
            for _ in range(5):
                pw = _mm(pw, pw, _HI)
                t_inv = t_inv + _mm(t_inv, pw, _HI)
            u_h = _mm(t_inv, rhs, _HI)
            u_heads.append(u_h)
            uv_h = jnp.concatenate([u_h, v_j], axis=0)
            y_heads.append(y0 + _mm(g_bot * m_incl, uv_h, _HI))
        u_j = jnp.where(lane_c < HEAD_DIM, u_heads[0], u_heads[1])
        y_j = jnp.where(lane_c < HEAD_DIM, y_heads[0], y_heads[1])

        lhs = jnp.concatenate([zt_cat, eye * e_end[:, sl]], axis=0)
        rhs_s = jnp.concatenate([u_j, v_j, s_j], axis=0)
        s_ref[j] = _mm_tn(lhs, rhs_s, _HI) * seg

        mu_y = _mm(y_j, seg, _HI) * (1.0 / HEAD_DIM)
        d = y_j - mu_y
        var = _mm(d * d, seg, _HI) * (1.0 / HEAD_DIM)
        yn = d * lax.rsqrt(var + GN_EPS) * gn_w[:, sl] + gn_b[:, sl]
        bonus = _mm(r_j * k2 * r_k[:, sl], seg, _HI) * v_j
        outs.append((yn + bonus) * g[:, sl])
    o_ref[...] = jnp.concatenate(outs, axis=1).astype(o_ref.dtype)


def rwkv_params(shift_mu, w0, w_up, a0, a_up, g_up, k_k, k_a, r_k, gn_w, gn_b):
    W = RWKV_WIDTH
    n_lora = DECAY_LORA + AAA_LORA + GATE_LORA
    mu_rkv = shift_mu[:3 * W].reshape(3, W)
    mu_l = jnp.pad(shift_mu[3 * W:], (0, LORA_PAD - n_lora)).reshape(1, LORA_PAD)
    vecs = jnp.stack([w0, a0, k_k, k_a, r_k.reshape(W), gn_w, gn_b, jnp.zeros_like(w0)])
    w_up_p = jnp.pad(w_up, ((0, LANES - DECAY_LORA), (0, 0))).astype(jnp.bfloat16)
    a_up_p = jnp.pad(a_up, ((DECAY_LORA, LANES - DECAY_LORA - AAA_LORA), (0, 0))).astype(jnp.bfloat16)
    g_up_p = jnp.pad(g_up, ((0, 256 - GATE_LORA), (0, 0))).astype(jnp.bfloat16)
    return mu_rkv, mu_l, vecs, w_up_p, a_up_p, g_up_p


def rwkv7_mix(p, B, LP, col_r, col_k, col_v, col_l, mu_rkv, mu_l, vecs, w_up, a_up, g_up):
    C = RWKV_CHUNK
    nc = LP // C
    W = RWKV_WIDTH

    def col_spec(width, col):
        return pl.BlockSpec((C, width), lambda b, c: (b * nc + c, col // width))

    def full(shape):
        return pl.BlockSpec(shape, lambda b, c: (0,) * len(shape))

    return pl.pallas_call(
        _rwkv_kernel,
        grid=(B, nc),
        in_specs=[
            col_spec(W, col_r), col_spec(W, col_k), col_spec(W, col_v), col_spec(LORA_PAD, col_l),
            full(mu_rkv.shape), full(mu_l.shape), full(vecs.shape),
            full(w_up.shape), full(a_up.shape), full(g_up.shape),
        ],
        out_specs=pl.BlockSpec((C, W), lambda b, c: (b * nc + c, 0)),
        out_shape=jax.ShapeDtypeStruct((B * LP, W), jnp.bfloat16),
        scratch_shapes=[
            pltpu.VMEM((RWKV_TILES, LANES, LANES), jnp.float32),
            pltpu.VMEM((1, W), jnp.float32), pltpu.VMEM((1, W), jnp.float32), pltpu.VMEM((1, W), jnp.float32),
            pltpu.VMEM((1, LORA_PAD), jnp.float32),
        ],
        compiler_params=pltpu.CompilerParams(
            dimension_semantics=("arbitrary", "arbitrary"), vmem_limit_bytes=VMEM_LIMIT_BYTES),
        name="rwkv7_mix",
    )(p, p, p, p, mu_rkv, mu_l, vecs, w_up, a_up, g_up)


def _rms(x, g):
    return x * lax.rsqrt(jnp.mean(x * x, axis=-1, keepdims=True) + RMS_EPS) * g


def _attention(q, k, v, q_gain, k_gain, sinks):
    B, LP, _ = q.shape
    NB = LP // BLOCK
    KVH = ATTN_KV_HEADS
    G = ATTN_Q_HEADS // KVH
    q = _rms(q.reshape(B, LP, ATTN_Q_HEADS, HEAD_DIM), q_gain).reshape(B, NB, BLOCK, KVH, G, HEAD_DIM)
    k = _rms(k.reshape(B, LP, KVH, HEAD_DIM), k_gain).reshape(B, NB, BLOCK, KVH, HEAD_DIM)
    v = v.reshape(B, NB, BLOCK, KVH, HEAD_DIM)

    def with_prev(t):
        prev = jnp.pad(t, ((0, 0), (1, 0), (0, 0), (0, 0), (0, 0)))[:, :-1]
        return jnp.concatenate([prev, t], axis=2)

    kw, vw = with_prev(k), with_prev(v)
    s = jnp.einsum('bnqkgd,bnskd->bnkgqs', q, kw, preferred_element_type=jnp.float32) * ATTN_SCALE
    blk = jnp.arange(NB)[:, None, None]
    qpos = blk * BLOCK + jnp.arange(BLOCK)[None, :, None]
    kpos = (blk - 1) * BLOCK + jnp.arange(2 * BLOCK)[None, None, :]
    dist = qpos - kpos
    mask = (dist >= 0) & (dist < WINDOW) & (kpos >= PAD_LEFT)
    s = jnp.where(mask[None, :, None, None], s, MASK_VALUE)
    sink = jnp.broadcast_to(sinks.reshape(1, 1, KVH, G, 1, 1), s.shape[:-1] + (1,))
    p = jax.nn.softmax(jnp.concatenate([s, sink], axis=-1), axis=-1)[..., :-1]
    o = jnp.einsum('bnkgqs,bnskd->bnqkgd', p, vw)
    return o.reshape(B, LP, ATTN_WIDTH)


def _peer_select_and_mix(xt, q, peer_sub_keys, peer_down, peer_up):
    T, D = xt.shape
    K = PEER_TOPK
    q = q.reshape(T, PEER_HEADS, 2, D_KEY // 2)
    scores = jnp.einsum('thcd,hcnd->thcn', q, peer_sub_keys, preferred_element_type=jnp.float32)
    s_half, i_half = lax.top_k(scores, K)
    cand_s = (s_half[:, :, 0, :, None] + s_half[:, :, 1, None, :]).reshape(T, PEER_HEADS, K * K)
    cand_i = (i_half[:, :, 0, :, None] * N_KEYS + i_half[:, :, 1, None, :]).reshape(T, PEER_HEADS, K * K)
    top_s, top_pos = lax.top_k(cand_s, K)
    idx = jnp.take_along_axis(cand_i, top_pos, axis=-1)
    gate = jax.nn.softmax(top_s, axis=-1)
    nblk = T // PEER_BLOCK
    xb = xt.reshape(nblk, PEER_BLOCK, D)
    ib = idx.reshape(nblk, PEER_BLOCK, PEER_HEADS * K)
    gb = gate.reshape(nblk, PEER_BLOCK, PEER_HEADS * K)

    def one_block(args):
        xc, ic, gc = args
        u = jnp.take(peer_down, ic, axis=0)
        hcur = jax.nn.gelu(jnp.einsum('ped,pd->pe', u, xc), approximate=False)
        vv = jnp.take(peer_up, ic, axis=0)
        return jnp.einsum('pe,ped->pd', hcur * gc, vv)

    return lax.map(one_block, (xb, ib, gb)).reshape(T, D)


def kernel(x, meta_tokens, norm1_g, w_in, shift_mu, w0, w_up, a0, a_up, g_up, k_k, k_a, r_k, gn_w, gn_b,
           q_gain, k_gain, sinks, w_out, norm2_g, peer_query, peer_sub_keys, peer_down, peer_up):
    B, S, D = x.shape
    LP = S + BLOCK
    T = B * LP
    meta = jnp.broadcast_to(meta_tokens[None], (B, N_META, D))
    h = jnp.concatenate([jnp.zeros((B, PAD_LEFT, D), x.dtype), meta, x], axis=1)
    valid = jnp.arange(LP) >= PAD_LEFT
    ht = h.reshape(T, D)

    w = w_in[0]
    n_lora = DECAY_LORA + AAA_LORA + GATE_LORA
    rwkv0 = ATTN_COLS
    lora0 = rwkv0 + 3 * RWKV_WIDTH
    w_in_b = jnp.concatenate([
        w[:, rwkv0:lora0], w[:, :ATTN_WIDTH],
        jnp.pad(w[:, lora0:lora0 + n_lora], ((0, 0), (0, LORA_PAD - n_lora))),
        w[:, ATTN_WIDTH:ATTN_COLS]], axis=1).astype(jnp.bfloat16)
    col_q = 3 * RWKV_WIDTH
    col_l = col_q + ATTN_WIDTH
    col_kv = col_l + LORA_PAD
    n_in = col_kv + 2 * KV_WIDTH
    p = norm_matmul(ht, norm1_g[0], w_in_b, 512, n_in // 2)

    y_rwkv = rwkv7_mix(p, B, LP, 0, RWKV_WIDTH, 2 * RWKV_WIDTH, col_l,
                       *rwkv_params(shift_mu[0], w0[0], w_up[0], a0[0], a_up[0], g_up[0], k_k[0], k_a[0], r_k[0],
                                    gn_w[0], gn_b[0]))
    p3 = p.reshape(B, LP, n_in)
    q = p3[..., col_q:col_q + ATTN_WIDTH]
    k = p3[..., col_kv:col_kv + KV_WIDTH]
    v = p3[..., col_kv + KV_WIDTH:]
    y_attn = _attention(q, k, v, q_gain[0], k_gain[0], sinks[0])
    mix = jnp.concatenate([y_rwkv, y_attn.reshape(T, ATTN_WIDTH).astype(jnp.bfloat16)], axis=-1)
    h2 = matmul_residual(mix, w_out[0].astype(jnp.bfloat16), ht, 512, 1024)

    pq = norm_matmul(h2, norm2_g[0], peer_query[0].astype(jnp.bfloat16), 512, 1024)
    xn = _rms(h2, norm2_g[0])
    out = h2 + _peer_select_and_mix(xn, pq, peer_sub_keys[0], peer_down[0], peer_up[0])
    out = out.reshape(B, LP, D)
    out = jnp.where(valid[None, :, None], out, jnp.zeros_like(out))
    return out[:, BLOCK:]
```

```python
import functools

import jax
import jax.numpy as jnp
from jax import lax
from jax.experimental import pallas as pl
from jax.experimental.pallas import tpu as pltpu

N_META = 16
BLOCK = 128
PAD_LEFT = BLOCK - N_META
HEAD_DIM = 64
RWKV_WIDTH = 1024
RWKV_HEADS = RWKV_WIDTH // HEAD_DIM
ATTN_WIDTH = 1024
ATTN_Q_HEADS = ATTN_WIDTH // HEAD_DIM
ATTN_KV_HEADS = 2
KV_WIDTH = ATTN_KV_HEADS * HEAD_DIM
WINDOW = 128
ATTN_SCALE = HEAD_DIM ** -0.5
MASK_VALUE = -1e30
DECAY_LORA = 64
AAA_LORA = 64
GATE_LORA = 160
RMS_EPS = 1e-6
GN_EPS = 64e-5
ATTN_COLS = ATTN_WIDTH + 2 * KV_WIDTH
PEER_HEADS = 8
N_KEYS = 128
PEER_TOPK = 16
D_KEY = 256
PEER_BLOCK = 128

VMEM_LIMIT_BYTES = 48 * 1024 * 1024


def _norm_matmul_kernel(x_ref, g_ref, w_ref, o_ref):
    x = x_ref[...]
    ms = jnp.mean(x * x, axis=-1, keepdims=True)
    u = x * lax.rsqrt(ms + RMS_EPS) * g_ref[...]
    o_ref[...] = jnp.dot(u.astype(jnp.bfloat16), w_ref[...], preferred_element_type=jnp.float32)


def norm_matmul(x, g, w, tm, tn):
    m, k = x.shape
    n = w.shape[1]
    return pl.pallas_call(
        _norm_matmul_kernel,
        grid=(n // tn, m // tm),
        in_specs=[
            pl.BlockSpec((tm, k), lambda j, i: (i, 0)),
            pl.BlockSpec((1, k), lambda j, i: (0, 0)),
            pl.BlockSpec((k, tn), lambda j, i: (0, j)),
        ],
        out_specs=pl.BlockSpec((tm, tn), lambda j, i: (i, j)),
        out_shape=jax.ShapeDtypeStruct((m, n), jnp.float32),
        compiler_params=pltpu.CompilerParams(
            dimension_semantics=("arbitrary", "arbitrary"), vmem_limit_bytes=VMEM_LIMIT_BYTES),
        name="norm_matmul",
    )(x, g.reshape(1, k), w)


def _matmul_residual_kernel(x_ref, w_ref, r_ref, o_ref):
    o_ref[...] = r_ref[...] + jnp.dot(
        x_ref[...].astype(jnp.bfloat16), w_ref[...], preferred_element_type=jnp.float32)


def matmul_residual(x, w, r, tm, tn):
    m, k = x.shape
    n = w.shape[1]
    return pl.pallas_call(
        _matmul_residual_kernel,
        grid=(n // tn, m // tm),
        in_specs=[
            pl.BlockSpec((tm, k), lambda j, i: (i, 0)),
            pl.BlockSpec((k, tn), lambda j, i: (0, j)),
            pl.BlockSpec((tm, tn), lambda j, i: (i, j)),
        ],
        out_specs=pl.BlockSpec((tm, tn), lambda j, i: (i, j)),
        out_shape=jax.ShapeDtypeStruct((m, n), jnp.float32),
        compiler_params=pltpu.CompilerParams(
            dimension_semantics=("arbitrary", "arbitrary"), vmem_limit_bytes=VMEM_LIMIT_BYTES),
        name="matmul_residual",
    )(x, w, r)


RWKV_CHUNK = 64
LANES = 128
HEADS_PER_TILE = LANES // HEAD_DIM
RWKV_TILES = RWKV_WIDTH // LANES
LORA_PAD = 512
_HI = lax.Precision.HIGHEST


def _dot(a, b, dims, precision=None):
    return lax.dot_general(a, b, (dims, ((), ())), precision=precision, preferred_element_type=jnp.float32)


def _mm(a, b, precision=None):
    return _dot(a, b, ((1,), (0,)), precision)


def _mm_nt(a, b, precision=None):
    return _dot(a, b, ((1,), (1,)), precision)


def _mm_tn(a, b, precision=None):
    return _dot(a, b, ((0,), (0,)), precision)


def _iota2(shape, axis):
    return lax.broadcasted_iota(jnp.int32, shape, axis)


def _sigmoid(x):
    return 1.0 / (1.0 + jnp.exp(-x))


def _token_shift(x, prev_ref, mu):
    rolled = pltpu.roll(x, 1, 0)
    prev = jnp.where(_iota2(x.shape, 0) == 0, prev_ref[...], rolled)
    prev_ref[...] = x[x.shape[0] - 1:, :]
    return x + mu * (prev - x)


def _rwkv_kernel(xr_ref, xk_ref, xv_ref, xl_ref, mu_ref, mul_ref, vec_ref, wup_ref, aup_ref, gup_ref,
                 o_ref, s_ref, pr_ref, pk_ref, pv_ref, pl_ref):
    C = RWKV_CHUNK
    f32 = jnp.float32

    @pl.when(pl.program_id(1) == 0)
    def _():
        s_ref[...] = jnp.zeros_like(s_ref)
        pr_ref[...] = jnp.zeros_like(pr_ref)
        pk_ref[...] = jnp.zeros_like(pk_ref)
        pv_ref[...] = jnp.zeros_like(pv_ref)
        pl_ref[...] = jnp.zeros_like(pl_ref)

    r = _token_shift(xr_ref[...], pr_ref, mu_ref[0:1, :])
    kraw = _token_shift(xk_ref[...], pk_ref, mu_ref[1:2, :])
    v = _token_shift(xv_ref[...], pv_ref, mu_ref[2:3, :])
    xl = _token_shift(xl_ref[...], pl_ref, mul_ref[...])

    w0, a0, k_k, k_a = vec_ref[0:1, :], vec_ref[1:2, :], vec_ref[2:3, :], vec_ref[3:4, :]
    r_k, gn_w, gn_b = vec_ref[4:5, :], vec_ref[5:6, :], vec_ref[6:7, :]

    x_wa = xl[:, :LANES]
    wl = w0 + _mm(jnp.tanh(x_wa).astype(jnp.bfloat16), wup_ref[...])
    z = -wl
    softplus = jnp.maximum(z, 0.0) + jnp.log(1.0 + jnp.exp(-jnp.abs(z)))
    lw = -jnp.exp(-softplus - 0.5)
    a = _sigmoid(a0 + _mm(x_wa.astype(jnp.bfloat16), aup_ref[...]))
    g = _mm(_sigmoid(xl[:, LANES:LANES + 256]).astype(jnp.bfloat16), gup_ref[...])

    tri = (_iota2((C, C), 0) >= _iota2((C, C), 1)).astype(f32)
    cl = _mm(tri, lw, _HI)
    e_pos = jnp.exp(cl)
    e_excl = jnp.exp(cl - lw)
    e_neg = 1.0 / e_pos
    e_end = e_pos[C - 1:, :]
    e_tail = e_end * e_neg

    lane = _iota2((LANES, LANES), 1)
    row = _iota2((LANES, LANES), 0)
    seg = ((lane >= HEAD_DIM) == (row >= HEAD_DIM)).astype(f32)
    eye = (lane == row).astype(f32)

    gr = _iota2((C, 2 * C), 0)
    gc = _iota2((C, 2 * C), 1)
    gcm = jnp.where(gc >= C, gc - C, gc)
    m_a_strict = ((gc < C) & (gcm < gr)).astype(f32)
    m_b_strict = ((gc >= C) & (gcm < gr)).astype(f32)
    m_incl = (gcm <= gr).astype(f32)
    eye_c = (_iota2((C, C), 0) == _iota2((C, C), 1)).astype(f32)
    lane_c = _iota2((C, LANES), 1)

    outs = []
    for j in range(RWKV_TILES):
        sl = slice(j * LANES, (j + 1) * LANES)
        kr = kraw[:, sl]
        kkr = kr * k_k[:, sl]
        ss = _mm(kkr * kkr, seg, _HI)
        kk = kkr / jnp.maximum(jnp.sqrt(ss), 1e-12)
        a_j = a[:, sl]
        k2 = kr * (1.0 + (a_j - 1.0) * k_a[:, sl])
        r_j = r[:, sl]
        v_j = v[:, sl]
        kka = kk * a_j

        x_cat = jnp.concatenate([-kk * e_excl[:, sl], r_j * e_pos[:, sl]], axis=0)
        z_cat = jnp.concatenate([kka * e_neg[:, sl], k2 * e_neg[:, sl]], axis=0)
        zt_cat = jnp.concatenate([kka * e_tail[:, sl], k2 * e_tail[:, sl]], axis=0)
        s_j = s_ref[j]
        m1 = _mm(x_cat, s_j, _HI)
        ua0, y0 = m1[:C], m1[C:]
        vv = jnp.concatenate([v_j, v_j], axis=0)

        u_heads, y_heads = [], []
        for hl in range(HEADS_PER_TILE):
            hmask = ((lane_c >= HEAD_DIM) == (hl == 1)).astype(f32)
            xh = x_cat * jnp.concatenate([hmask, hmask], axis=0)
            gram = _mm_nt(xh, z_cat, _HI)
            g_top, g_bot = gram[:C], gram[C:]
            a_strict = (g_top * m_a_strict)[:, :C]
            rhs = ua0 + _mm(g_top * m_b_strict, vv, _HI)
            t_inv = eye_c + a_strict
            pw = a_strict
            for _ in range(5):
                pw = _mm(pw, pw, _HI)
                t_inv = t_inv + _mm(t_inv, pw, _HI)
            u_h = _mm(t_inv, rhs, _HI)
            u_heads.append(u_h)
            uv_h = jnp.concatenate([u_h, v_j], axis=0)
            y_heads.append(y0 + _mm(g_bot * m_incl, uv_h, _HI))
        u_j = jnp.where(lane_c < HEAD_DIM, u_heads[0], u_heads[1])
        y_j = jnp.where(lane_c < HEAD_DIM, y_heads[0], y_heads[1])

        lhs = jnp.concatenate([zt_cat, eye * e_end[:, sl]], axis=0)
        rhs_s = jnp.concatenate([u_j, v_j, s_j], axis=0)
        s_ref[j] = _mm_tn(lhs, rhs_s, _HI) * seg

        mu_y = _mm(y_j, seg, _HI) * (1.0 / HEAD_DIM)
        d = y_j - mu_y
        var = _mm(d * d, seg, _HI) * (1.0 / HEAD_DIM)
        yn = d * lax.rsqrt(var + GN_EPS) * gn_w[:, sl] + gn_b[:, sl]
        bonus = _mm(r_j * k2 * r_k[:, sl], seg, _HI) * v_j
        outs.append((yn + bonus) * g[:, sl])
    o_ref[...] = jnp.concatenate(outs, axis=1).astype(o_ref.dtype)


def rwkv_params(shift_mu, w0, w_up, a0, a_up, g_up, k_k, k_a, r_k, gn_w, gn_b):
    W = RWKV_WIDTH
    n_lora = DECAY_LORA + AAA_LORA + GATE_LORA
    mu_rkv = shift_mu[:3 * W].reshape(3, W)
    mu_l = jnp.pad(shift_mu[3 * W:], (0, LORA_PAD - n_lora)).reshape(1, LORA_PAD)
    vecs = jnp.stack([w0, a0, k_k, k_a, r_k.reshape(W), gn_w, gn_b, jnp.zeros_like(w0)])
    w_up_p = jnp.pad(w_up, ((0, LANES - DECAY_LORA), (0, 0))).astype(jnp.bfloat16)
    a_up_p = jnp.pad(a_up, ((DECAY_LORA, LANES - DECAY_LORA - AAA_LORA), (0, 0))).astype(jnp.bfloat16)
    g_up_p = jnp.pad(g_up, ((0, 256 - GATE_LORA), (0, 0))).astype(jnp.bfloat16)
    return mu_rkv, mu_l, vecs, w_up_p, a_up_p, g_up_p


def rwkv7_mix(p, B, LP, col_r, col_k, col_v, col_l, mu_rkv, mu_l, vecs, w_up, a_up, g_up):
    C = RWKV_CHUNK
    nc = LP // C
    W = RWKV_WIDTH

    def col_spec(width, col):
        return pl.BlockSpec((C, width), lambda b, c: (b * nc + c, col // width))

    def full(shape):
        return pl.BlockSpec(shape, lambda b, c: (0,) * len(shape))

    return pl.pallas_call(
        _rwkv_kernel,
        grid=(B, nc),
        in_specs=[
            col_spec(W, col_r), col_spec(W, col_k), col_spec(W, col_v), col_spec(LORA_PAD, col_l),
            full(mu_rkv.shape), full(mu_l.shape), full(vecs.shape),
            full(w_up.shape), full(a_up.shape), full(g_up.shape),
        ],
        out_specs=pl.BlockSpec((C, W), lambda b, c: (b * nc + c, 0)),
        out_shape=jax.ShapeDtypeStruct((B * LP, W), jnp.bfloat16),
        scratch_shapes=[
            pltpu.VMEM((RWKV_TILES, LANES, LANES), jnp.float32),
            pltpu.VMEM((1, W), jnp.float32), pltpu.VMEM((1, W), jnp.float32), pltpu.VMEM((1, W), jnp.float32),
            pltpu.VMEM((1, LORA_PAD), jnp.float32),
        ],
        compiler_params=pltpu.CompilerParams(
            dimension_semantics=("arbitrary", "arbitrary"), vmem_limit_bytes=VMEM_LIMIT_BYTES),
        name="rwkv7_mix",
    )(p, p, p, p, mu_rkv, mu_l, vecs, w_up, a_up, g_up)


PEER_PICKS = PEER_HEADS * PEER_TOPK
PEER_TOKENS_PER_STEP = 64
PEER_SLOTS = 3
MXU_MIN_ROWS = 16


def _peer_mix_kernel(idx_ref, x_ref, gate_ref, res_ref, tab_ref, o_ref, buf, sem):
    D = x_ref.shape[1]
    nt = x_ref.shape[0]
    ahead = PEER_SLOTS - 1

    def issue(tt, slot):
        for j in range(PEER_PICKS):
            e = idx_ref[tt, j]
            pltpu.make_async_copy(tab_ref.at[e], buf.at[slot, pl.ds(j, 1), :], sem.at[slot]).start()

    def wait_all(slot):
        pltpu.make_async_copy(buf.at[slot], buf.at[slot], sem.at[slot]).wait()

    def token(tt, slot, prefetch):
        if prefetch:
            issue(tt + ahead, (slot + ahead) % PEER_SLOTS)
        wait_all(slot)
        down = buf[slot, :, :D].astype(jnp.bfloat16)
        up = buf[slot, :, D:].astype(jnp.bfloat16)
        xb = jnp.broadcast_to(x_ref[pl.ds(tt, 1), :], (MXU_MIN_ROWS, D)).astype(jnp.bfloat16)
        h = _mm_nt(xb, down)
        w = 0.5 * h * (1.0 + lax.erf(h * (2.0 ** -0.5))) * gate_ref[pl.ds(tt, 1), :]
        mix = _mm(w.astype(jnp.bfloat16), up)
        o_ref[pl.ds(tt, 1), :] = res_ref[pl.ds(tt, 1), :] + mix[0:1, :]

    for s in range(ahead):
        issue(s, s)

    n_groups = (nt - ahead) // PEER_SLOTS

    def group(gi, carry):
        for s in range(PEER_SLOTS):
            token(gi * PEER_SLOTS + s, s, True)
        return carry

    lax.fori_loop(0, n_groups, group, 0)
    for tt in range(n_groups * PEER_SLOTS, nt):
        token(tt, tt % PEER_SLOTS, tt + ahead < nt)


def peer_mix(idx, xn, gate, res, table):
    T, D = xn.shape
    tb = PEER_TOKENS_PER_STEP

    def tok(width):
        return pl.BlockSpec((tb, width), lambda i: (i, 0))

    return pl.pallas_call(
        _peer_mix_kernel,
        grid=(T // tb,),
        in_specs=[
            pl.BlockSpec((tb, PEER_PICKS), lambda i: (i, 0), memory_space=pltpu.SMEM),
            tok(D), tok(PEER_PICKS), tok(D),
            pl.BlockSpec(memory_space=pl.ANY),
        ],
        out_specs=tok(D),
        out_shape=jax.ShapeDtypeStruct((T, D), jnp.float32),
        scratch_shapes=[
            pltpu.VMEM((PEER_SLOTS, PEER_PICKS, 2 * D), jnp.float32),
            pltpu.SemaphoreType.DMA((PEER_SLOTS,)),
        ],
        compiler_params=pltpu.CompilerParams(
            dimension_semantics=("arbitrary",), vmem_limit_bytes=VMEM_LIMIT_BYTES),
        name="peer_mix",
    )(idx, xn, gate, res, table)


def _rms(x, g):
    return x * lax.rsqrt(jnp.mean(x * x, axis=-1, keepdims=True) + RMS_EPS) * g


def _attention(q, k, v, q_gain, k_gain, sinks):
    B, LP, _ = q.shape
    NB = LP // BLOCK
    KVH = ATTN_KV_HEADS
    G = ATTN_Q_HEADS // KVH
    q = _rms(q.reshape(B, LP, ATTN_Q_HEADS, HEAD_DIM), q_gain).reshape(B, NB, BLOCK, KVH, G, HEAD_DIM)
    k = _rms(k.reshape(B, LP, KVH, HEAD_DIM), k_gain).reshape(B, NB, BLOCK, KVH, HEAD_DIM)
    v = v.reshape(B, NB, BLOCK, KVH, HEAD_DIM)

    def with_prev(t):
        prev = jnp.pad(t, ((0, 0), (1, 0), (0, 0), (0, 0), (0, 0)))[:, :-1]
        return jnp.concatenate([prev, t], axis=2)

    kw, vw = with_prev(k), with_prev(v)
    s = jnp.einsum('bnqkgd,bnskd->bnkgqs', q, kw, preferred_element_type=jnp.float32) * ATTN_SCALE
    blk = jnp.arange(NB)[:, None, None]
    qpos = blk * BLOCK + jnp.arange(BLOCK)[None, :, None]
    kpos = (blk - 1) * BLOCK + jnp.arange(2 * BLOCK)[None, None, :]
    dist = qpos - kpos
    mask = (dist >= 0) & (dist < WINDOW) & (kpos >= PAD_LEFT)
    s = jnp.where(mask[None, :, None, None], s, MASK_VALUE)
    sink = jnp.broadcast_to(sinks.reshape(1, 1, KVH, G, 1, 1), s.shape[:-1] + (1,))
    p = jax.nn.softmax(jnp.concatenate([s, sink], axis=-1), axis=-1)[..., :-1]
    o = jnp.einsum('bnkgqs,bnskd->bnqkgd', p, vw)
    return o.reshape(B, LP, ATTN_WIDTH)


def _peer_select(q, peer_sub_keys):
    T = q.shape[0]
    K = PEER_TOPK
    q = q.reshape(T, PEER_HEADS, 2, D_KEY // 2)
    scores = jnp.einsum('thcd,hcnd->thcn', q, peer_sub_keys, preferred_element_type=jnp.float32)
    s_half, i_half = lax.top_k(scores, K)
    cand_s = (s_half[:, :, 0, :, None] + s_half[:, :, 1, None, :]).reshape(T, PEER_HEADS, K * K)
    cand_i = (i_half[:, :, 0, :, None] * N_KEYS + i_half[:, :, 1, None, :]).reshape(T, PEER_HEADS, K * K)
    top_s, top_pos = lax.top_k(cand_s, K)
    idx = jnp.take_along_axis(cand_i, top_pos, axis=-1)
    gate = jax.nn.softmax(top_s, axis=-1)
    return idx.reshape(T, PEER_PICKS), gate.reshape(T, PEER_PICKS)


def kernel(x, meta_tokens, norm1_g, w_in, shift_mu, w0, w_up, a0, a_up, g_up, k_k, k_a, r_k, gn_w, gn_b,
           q_gain, k_gain, sinks, w_out, norm2_g, peer_query, peer_sub_keys, peer_down, peer_up):
    B, S, D = x.shape
    LP = S + BLOCK
    T = B * LP
    meta = jnp.broadcast_to(meta_tokens[None], (B, N_META, D))
    h = jnp.concatenate([jnp.zeros((B, PAD_LEFT, D), x.dtype), meta, x], axis=1)
    valid = jnp.arange(LP) >= PAD_LEFT
    ht = h.reshape(T, D)

    w = w_in[0]
    n_lora = DECAY_LORA + AAA_LORA + GATE_LORA
    rwkv0 = ATTN_COLS
    lora0 = rwkv0 + 3 * RWKV_WIDTH
    w_in_b = jnp.concatenate([
        w[:, rwkv0:lora0], w[:, :ATTN_WIDTH],
        jnp.pad(w[:, lora0:lora0 + n_lora], ((0, 0), (0, LORA_PAD - n_lora))),
        w[:, ATTN_WIDTH:ATTN_COLS]], axis=1).astype(jnp.bfloat16)
    col_q = 3 * RWKV_WIDTH
    col_l = col_q + ATTN_WIDTH
    col_kv = col_l + LORA_PAD
    n_in = col_kv + 2 * KV_WIDTH
    p = norm_matmul(ht, norm1_g[0], w_in_b, 512, n_in // 2)

    y_rwkv = rwkv7_mix(p, B, LP, 0, RWKV_WIDTH, 2 * RWKV_WIDTH, col_l,
                       *rwkv_params(shift_mu[0], w0[0], w_up[0], a0[0], a_up[0], g_up[0], k_k[0], k_a[0], r_k[0],
                                    gn_w[0], gn_b[0]))
    p3 = p.reshape(B, LP, n_in)
    q = p3[..., col_q:col_q + ATTN_WIDTH]
    k = p3[..., col_kv:col_kv + KV_WIDTH]
    v = p3[..., col_kv + KV_WIDTH:]
    y_attn = _attention(q, k, v, q_gain[0], k_gain[0], sinks[0])
    mix = jnp.concatenate([y_rwkv, y_attn.reshape(T, ATTN_WIDTH).astype(jnp.bfloat16)], axis=-1)
    h2 = matmul_residual(mix, w_out[0].astype(jnp.bfloat16), ht, 512, 1024)

    pq = norm_matmul(h2, norm2_g[0], peer_query[0].astype(jnp.bfloat16), 512, 1024)
    xn = _rms(h2, norm2_g[0])
    idx, gate = _peer_select(pq, peer_sub_keys[0])
    table = jnp.concatenate([peer_down[0], peer_up[0]], axis=1)[:, None, :]
    out = peer_mix(idx, xn, gate, h2, table).reshape(B, LP, D)
    out = jnp.where(valid[None, :, None], out, jnp.zeros_like(out))
    return out[:, BLOCK:]
```

```python
import functools

import jax
import jax.numpy as jnp
from jax import lax
from jax.experimental import pallas as pl
from jax.experimental.pallas import tpu as pltpu

N_META = 16
BLOCK = 128
PAD_LEFT = BLOCK - N_META
HEAD_DIM = 64
RWKV_WIDTH = 1024
RWKV_HEADS = RWKV_WIDTH // HEAD_DIM
ATTN_WIDTH = 1024
ATTN_Q_HEADS = ATTN_WIDTH // HEAD_DIM
ATTN_KV_HEADS = 2
KV_WIDTH = ATTN_KV_HEADS * HEAD_DIM
WINDOW = 128
ATTN_SCALE = HEAD_DIM ** -0.5
MASK_VALUE = -1e30
DECAY_LORA = 64
AAA_LORA = 64
GATE_LORA = 160
RMS_EPS = 1e-6
GN_EPS = 64e-5
ATTN_COLS = ATTN_WIDTH + 2 * KV_WIDTH
PEER_HEADS = 8
N_KEYS = 128
PEER_TOPK = 16
D_KEY = 256
PEER_BLOCK = 128

VMEM_LIMIT_BYTES = 48 * 1024 * 1024


def _norm_matmul_kernel(x_ref, g_ref, w_ref, o_ref):
    x = x_ref[...]
    ms = jnp.mean(x * x, axis=-1, keepdims=True)
    u = x * lax.rsqrt(ms + RMS_EPS) * g_ref[...]
    o_ref[...] = jnp.dot(u.astype(jnp.bfloat16), w_ref[...], preferred_element_type=jnp.float32)


def norm_matmul(x, g, w, tm, tn):
    m, k = x.shape
    n = w.shape[1]
    return pl.pallas_call(
        _norm_matmul_kernel,
        grid=(n // tn, m // tm),
        in_specs=[
            pl.BlockSpec((tm, k), lambda j, i: (i, 0)),
            pl.BlockSpec((1, k), lambda j, i: (0, 0)),
            pl.BlockSpec((k, tn), lambda j, i: (0, j)),
        ],
        out_specs=pl.BlockSpec((tm, tn), lambda j, i: (i, j)),
        out_shape=jax.ShapeDtypeStruct((m, n), jnp.float32),
        compiler_params=pltpu.CompilerParams(
            dimension_semantics=("arbitrary", "arbitrary"), vmem_limit_bytes=VMEM_LIMIT_BYTES),
        name="norm_matmul",
    )(x, g.reshape(1, k), w)


def _matmul_residual_kernel(x_ref, w_ref, r_ref, o_ref):
    o_ref[...] = r_ref[...] + jnp.dot(
        x_ref[...].astype(jnp.bfloat16), w_ref[...], preferred_element_type=jnp.float32)


def matmul_residual(x, w, r, tm, tn):
    m, k = x.shape
    n = w.shape[1]
    return pl.pallas_call(
        _matmul_residual_kernel,
        grid=(n // tn, m // tm),
        in_specs=[
            pl.BlockSpec((tm, k), lambda j, i: (i, 0)),
            pl.BlockSpec((k, tn), lambda j, i: (0, j)),
            pl.BlockSpec((tm, tn), lambda j, i: (i, j)),
        ],
        out_specs=pl.BlockSpec((tm, tn), lambda j, i: (i, j)),
        out_shape=jax.ShapeDtypeStruct((m, n), jnp.float32),
        compiler_params=pltpu.CompilerParams(
            dimension_semantics=("arbitrary", "arbitrary"), vmem_limit_bytes=VMEM_LIMIT_BYTES),
        name="matmul_residual",
    )(x, w, r)


RWKV_CHUNK = 64
LANES = 128
HEADS_PER_TILE = LANES // HEAD_DIM
RWKV_TILES = RWKV_WIDTH // LANES
LORA_PAD = 512
_HI = lax.Precision.HIGHEST


def _dot(a, b, dims, precision=None):
    return lax.dot_general(a, b, (dims, ((), ())), precision=precision, preferred_element_type=jnp.float32)


def _mm(a, b, precision=None):
    return _dot(a, b, ((1,), (0,)), precision)


def _mm_nt(a, b, precision=None):
    return _dot(a, b, ((1,), (1,)), precision)


def _mm_tn(a, b, precision=None):
    return _dot(a, b, ((0,), (0,)), precision)


def _iota2(shape, axis):
    return lax.broadcasted_iota(jnp.int32, shape, axis)


def _sigmoid(x):
    return 1.0 / (1.0 + jnp.exp(-x))


def _token_shift(x, prev_ref, mu):
    rolled = pltpu.roll(x, 1, 0)
    prev = jnp.where(_iota2(x.shape, 0) == 0, prev_ref[...], rolled)
    prev_ref[...] = x[x.shape[0] - 1:, :]
    return x + mu * (prev - x)


def _rwkv_kernel(xr_ref, xk_ref, xv_ref, xl_ref, mu_ref, mul_ref, vec_ref, wup_ref, aup_ref, gup_ref,
                 o_ref, s_ref, pr_ref, pk_ref, pv_ref, pl_ref):
    C = RWKV_CHUNK
    f32 = jnp.float32

    @pl.when(pl.program_id(1) == 0)
    def _():
        s_ref[...] = jnp.zeros_like(s_ref)
        pr_ref[...] = jnp.zeros_like(pr_ref)
        pk_ref[...] = jnp.zeros_like(pk_ref)
        pv_ref[...] = jnp.zeros_like(pv_ref)
        pl_ref[...] = jnp.zeros_like(pl_ref)

    r = _token_shift(xr_ref[...], pr_ref, mu_ref[0:1, :])
    kraw = _token_shift(xk_ref[...], pk_ref, mu_ref[1:2, :])
    v = _token_shift(xv_ref[...], pv_ref, mu_ref[2:3, :])
    xl = _token_shift(xl_ref[...], pl_ref, mul_ref[...])

    w0, a0, k_k, k_a = vec_ref[0:1, :], vec_ref[1:2, :], vec_ref[2:3, :], vec_ref[3:4, :]
    r_k, gn_w, gn_b = vec_ref[4:5, :], vec_ref[5:6, :], vec_ref[6:7, :]

    x_wa = xl[:, :LANES]
    wl = w0 + _mm(jnp.tanh(x_wa).astype(jnp.bfloat16), wup_ref[...])
    z = -wl
    softplus = jnp.maximum(z, 0.0) + jnp.log(1.0 + jnp.exp(-jnp.abs(z)))
    lw = -jnp.exp(-softplus - 0.5)
    a = _sigmoid(a0 + _mm(x_wa.astype(jnp.bfloat16), aup_ref[...]))
    g = _mm(_sigmoid(xl[:, LANES:LANES + 256]).astype(jnp.bfloat16), gup_ref[...])

    tri = (_iota2((C, C), 0) >= _iota2((C, C), 1)).astype(f32)
    cl = _mm(tri, lw, _HI)
    e_pos = jnp.exp(cl)
    e_excl = jnp.exp(cl - lw)
    e_neg = 1.0 / e_pos
    e_end = e_pos[C - 1:, :]
    e_tail = e_end * e_neg

    lane = _iota2((LANES, LANES), 1)
    row = _iota2((LANES, LANES), 0)
    seg = ((lane >= HEAD_DIM) == (row >= HEAD_DIM)).astype(f32)
    eye = (lane == row).astype(f32)

    gr = _iota2((C, 2 * C), 0)
    gc = _iota2((C, 2 * C), 1)
    gcm = jnp.where(gc >= C, gc - C, gc)
    m_a_strict = ((gc < C) & (gcm < gr)).astype(f32)
    m_b_strict = ((gc >= C) & (gcm < gr)).astype(f32)
    m_incl = (gcm <= gr).astype(f32)
    eye_c = (_iota2((C, C), 0) == _iota2((C, C), 1)).astype(f32)
    lane_c = _iota2((C, LANES), 1)

    outs = []
    for j in range(RWKV_TILES):
        sl = slice(j * LANES, (j + 1) * LANES)
        kr = kraw[:, sl]
        kkr = kr * k_k[:, sl]
        ss = _mm(kkr * kkr, seg, _HI)
        kk = kkr / jnp.maximum(jnp.sqrt(ss), 1e-12)
        a_j = a[:, sl]
        k2 = kr * (1.0 + (a_j - 1.0) * k_a[:, sl])
        r_j = r[:, sl]
        v_j = v[:, sl]
        kka = kk * a_j

        x_cat = jnp.concatenate([-kk * e_excl[:, sl], r_j * e_pos[:, sl]], axis=0)
        z_cat = jnp.concatenate([kka * e_neg[:, sl], k2 * e_neg[:, sl]], axis=0)
        zt_cat = jnp.concatenate([kka * e_tail[:, sl], k2 * e_tail[:, sl]], axis=0)
        s_j = s_ref[j]
        m1 = _mm(x_cat, s_j, _HI)
        ua0, y0 = m1[:C], m1[C:]
        vv = jnp.concatenate([v_j, v_j], axis=0)

        u_heads, y_heads = [], []
        for hl in range(HEADS_PER_TILE):
            hmask = ((lane_c >= HEAD_DIM) == (hl == 1)).astype(f32)
            xh = x_cat * jnp.concatenate([hmask, hmask], axis=0)
            gram = _mm_nt(xh, z_cat, _HI)
            g_top, g_bot = gram[:C], gram[C:]
            a_strict = (g_top * m_a_strict)[:, :C]
            rhs = ua0 + _mm(g_top * m_b_strict, vv, _HI)
            t_inv = eye_c + a_strict
            pw = a_strict
            for _ in range(5):
                pw = _mm(pw, pw, _HI)
                t_inv = t_inv + _mm(t_inv, pw, _HI)
            u_h = _mm(t_inv, rhs, _HI)
            u_heads.append(u_h)
            uv_h = jnp.concatenate([u_h, v_j], axis=0)
            y_heads.append(y0 + _mm(g_bot * m_incl, uv_h, _HI))
        u_j = jnp.where(lane_c < HEAD_DIM, u_heads[0], u_heads[1])
        y_j = jnp.where(lane_c < HEAD_DIM, y_heads[0], y_heads[1])

        lhs = jnp.concatenate([zt_cat, eye * e_end[:, sl]], axis=0)
        rhs_s = jnp.concatenate([u_j, v_j, s_j], axis=0)
        s_ref[j] = _mm_tn(lhs, rhs_s, _HI) * seg

        mu_y = _mm(y_j, seg, _HI) * (1.0 / HEAD_DIM)
        d = y_j - mu_y
        var = _mm(d * d, seg, _HI) * (1.0 / HEAD_DIM)
        yn = d * lax.rsqrt(var + GN_EPS) * gn_w[:, sl] + gn_b[:, sl]
        bonus = _mm(r_j * k2 * r_k[:, sl], seg, _HI) * v_j
        outs.append((yn + bonus) * g[:, sl])
    o_ref[...] = jnp.concatenate(outs, axis=1).astype(o_ref.dtype)


def rwkv_params(shift_mu, w0, w_up, a0, a_up, g_up, k_k, k_a, r_k, gn_w, gn_b):
    W = RWKV_WIDTH
    n_lora = DECAY_LORA + AAA_LORA + GATE_LORA
    mu_rkv = shift_mu[:3 * W].reshape(3, W)
    mu_l = jnp.pad(shift_mu[3 * W:], (0, LORA_PAD - n_lora)).reshape(1, LORA_PAD)
    vecs = jnp.stack([w0, a0, k_k, k_a, r_k.reshape(W), gn_w, gn_b, jnp.zeros_like(w0)])
    w_up_p = jnp.pad(w_up, ((0, LANES - DECAY_LORA), (0, 0))).astype(jnp.bfloat16)
    a_up_p = jnp.pad(a_up, ((DECAY_LORA, LANES - DECAY_LORA - AAA_LORA), (0, 0))).astype(jnp.bfloat16)
    g_up_p = jnp.pad(g_up, ((0, 256 - GATE_LORA), (0, 0))).astype(jnp.bfloat16)
    return mu_rkv, mu_l, vecs, w_up_p, a_up_p, g_up_p


def rwkv7_mix(p, B, LP, col_r, col_k, col_v, col_l, mu_rkv, mu_l, vecs, w_up, a_up, g_up):
    C = RWKV_CHUNK
    nc = LP // C
    W = RWKV_WIDTH

    def col_spec(width, col):
        return pl.BlockSpec((C, width), lambda b, c: (b * nc + c, col // width))

    def full(shape):
        return pl.BlockSpec(shape, lambda b, c: (0,) * len(shape))

    return pl.pallas_call(
        _rwkv_kernel,
        grid=(B, nc),
        in_specs=[
            col_spec(W, col_r), col_spec(W, col_k), col_spec(W, col_v), col_spec(LORA_PAD, col_l),
            full(mu_rkv.shape), full(mu_l.shape), full(vecs.shape),
            full(w_up.shape), full(a_up.shape), full(g_up.shape),
        ],
        out_specs=pl.BlockSpec((C, W), lambda b, c: (b * nc + c, 0)),
        out_shape=jax.ShapeDtypeStruct((B * LP, W), jnp.bfloat16),
        scratch_shapes=[
            pltpu.VMEM((RWKV_TILES, LANES, LANES), jnp.float32),
            pltpu.VMEM((1, W), jnp.float32), pltpu.VMEM((1, W), jnp.float32), pltpu.VMEM((1, W), jnp.float32),
            pltpu.VMEM((1, LORA_PAD), jnp.float32),
        ],
        compiler_params=pltpu.CompilerParams(
            dimension_semantics=("arbitrary", "arbitrary"), vmem_limit_bytes=VMEM_LIMIT_BYTES),
        name="rwkv7_mix",
    )(p, p, p, p, mu_rkv, mu_l, vecs, w_up, a_up, g_up)


PEER_PICKS = PEER_HEADS * PEER_TOPK
PEER_TOKENS_PER_STEP = 128
PEER_GROUP = 4
PEER_GROUPS_PER_STEP = PEER_TOKENS_PER_STEP // PEER_GROUP


def _peer_mix_kernel(idx_ref, x_ref, gate_t_ref, res_ref, tab_ref, o_ref, buf, sem):
    D = x_ref.shape[1]
    G = PEER_GROUP
    gate_t = gate_t_ref[0]
    tok_lane = _iota2(gate_t.shape, 1)

    def issue(group, half):
        for s in range(G):
            for j in range(PEER_PICKS):
                e = idx_ref[group * G + s, j]
                pltpu.make_async_copy(
                    tab_ref.at[e], buf.at[half * G + s, pl.ds(j, 1), :], sem.at[half]).start()

    def wait_group(half):
        view = buf.at[pl.ds(half * G, G)]
        pltpu.make_async_copy(view, view, sem.at[half]).wait()

    def mix_group(group, half):
        for s in range(G):
            tt = group * G + s
            down = buf[half * G + s, :, :D]
            up = buf[half * G + s, :, D:]
            h = jnp.sum(down * x_ref[pl.ds(tt, 1), :], axis=1, keepdims=True)
            gate = jnp.sum(jnp.where(tok_lane == tt, gate_t, 0.0), axis=1, keepdims=True)
            w = 0.5 * h * (1.0 + lax.erf(h * (2.0 ** -0.5))) * gate
            o_ref[pl.ds(tt, 1), :] = res_ref[pl.ds(tt, 1), :] + jnp.sum(w * up, axis=0, keepdims=True)

    def step(group, half, prefetch):
        if prefetch:
            issue(group + 1, 1 - half)
        wait_group(half)
        mix_group(group, half)

    issue(0, 0)
    n_pairs = PEER_GROUPS_PER_STEP // 2

    def pair(gp, carry):
        step(2 * gp, 0, True)
        step(2 * gp + 1, 1, True)
        return carry

    lax.fori_loop(0, n_pairs - 1, pair, 0)
    step(PEER_GROUPS_PER_STEP - 2, 0, True)
    step(PEER_GROUPS_PER_STEP - 1, 1, False)


def peer_mix(idx, xn, gate_t, res, table):
    T, D = xn.shape
    tb = PEER_TOKENS_PER_STEP

    def tok(width):
        return pl.BlockSpec((tb, width), lambda i: (i, 0))

    return pl.pallas_call(
        _peer_mix_kernel,
        grid=(T // tb,),
        in_specs=[
            pl.BlockSpec((tb, PEER_PICKS), lambda i: (i, 0), memory_space=pltpu.SMEM),
            tok(D),
            pl.BlockSpec((1, PEER_PICKS, tb), lambda i: (i, 0, 0)),
            tok(D),
            pl.BlockSpec(memory_space=pl.ANY),
        ],
        out_specs=tok(D),
        out_shape=jax.ShapeDtypeStruct((T, D), jnp.float32),
        scratch_shapes=[
            pltpu.VMEM((2 * PEER_GROUP, PEER_PICKS, 2 * D), jnp.float32),
            pltpu.SemaphoreType.DMA((2,)),
        ],
        compiler_params=pltpu.CompilerParams(
            dimension_semantics=("arbitrary",), vmem_limit_bytes=VMEM_LIMIT_BYTES),
        name="peer_mix",
    )(idx, xn, gate_t, res, table)


def _rms(x, g):
    return x * lax.rsqrt(jnp.mean(x * x, axis=-1, keepdims=True) + RMS_EPS) * g


def _attention(q, k, v, q_gain, k_gain, sinks):
    B, LP, _ = q.shape
    NB = LP // BLOCK
    KVH = ATTN_KV_HEADS
    G = ATTN_Q_HEADS // KVH
    q = _rms(q.reshape(B, LP, ATTN_Q_HEADS, HEAD_DIM), q_gain).reshape(B, NB, BLOCK, KVH, G, HEAD_DIM)
    k = _rms(k.reshape(B, LP, KVH, HEAD_DIM), k_gain).reshape(B, NB, BLOCK, KVH, HEAD_DIM)
    v = v.reshape(B, NB, BLOCK, KVH, HEAD_DIM)

    def with_prev(t):
        prev = jnp.pad(t, ((0, 0), (1, 0), (0, 0), (0, 0), (0, 0)))[:, :-1]
        return jnp.concatenate([prev, t], axis=2)

    kw, vw = with_prev(k), with_prev(v)
    s = jnp.einsum('bnqkgd,bnskd->bnkgqs', q, kw, preferred_element_type=jnp.float32) * ATTN_SCALE
    blk = jnp.arange(NB)[:, None, None]
    qpos = blk * BLOCK + jnp.arange(BLOCK)[None, :, None]
    kpos = (blk - 1) * BLOCK + jnp.arange(2 * BLOCK)[None, None, :]
    dist = qpos - kpos
    mask = (dist >= 0) & (dist < WINDOW) & (kpos >= PAD_LEFT)
    s = jnp.where(mask[None, :, None, None], s, MASK_VALUE)
    sink = jnp.broadcast_to(sinks.reshape(1, 1, KVH, G, 1, 1), s.shape[:-1] + (1,))
    p = jax.nn.softmax(jnp.concatenate([s, sink], axis=-1), axis=-1)[..., :-1]
    o = jnp.einsum('bnkgqs,bnskd->bnqkgd', p, vw)
    return o.reshape(B, LP, ATTN_WIDTH)


def _peer_select(q, peer_sub_keys):
    T = q.shape[0]
    K = PEER_TOPK
    q = q.reshape(T, PEER_HEADS, 2, D_KEY // 2)
    scores = jnp.einsum('thcd,hcnd->thcn', q, peer_sub_keys, preferred_element_type=jnp.float32)
    s_half, i_half = lax.top_k(scores, K)
    cand_s = (s_half[:, :, 0, :, None] + s_half[:, :, 1, None, :]).reshape(T, PEER_HEADS, K * K)
    cand_i = (i_half[:, :, 0, :, None] * N_KEYS + i_half[:, :, 1, None, :]).reshape(T, PEER_HEADS, K * K)
    top_s, top_pos = lax.top_k(cand_s, K)
    idx = jnp.take_along_axis(cand_i, top_pos, axis=-1)
    gate = jax.nn.softmax(top_s, axis=-1)
    return idx.reshape(T, PEER_PICKS), gate.reshape(T, PEER_PICKS)


def kernel(x, meta_tokens, norm1_g, w_in, shift_mu, w0, w_up, a0, a_up, g_up, k_k, k_a, r_k, gn_w, gn_b,
           q_gain, k_gain, sinks, w_out, norm2_g, peer_query, peer_sub_keys, peer_down, peer_up):
    B, S, D = x.shape
    LP = S + BLOCK
    T = B * LP
    meta = jnp.broadcast_to(meta_tokens[None], (B, N_META, D))
    h = jnp.concatenate([jnp.zeros((B, PAD_LEFT, D), x.dtype), meta, x], axis=1)
    valid = jnp.arange(LP) >= PAD_LEFT
    ht = h.reshape(T, D)

    w = w_in[0]
    n_lora = DECAY_LORA + AAA_LORA + GATE_LORA
    rwkv0 = ATTN_COLS
    lora0 = rwkv0 + 3 * RWKV_WIDTH
    w_in_b = jnp.concatenate([
        w[:, rwkv0:lora0], w[:, :ATTN_WIDTH],
        jnp.pad(w[:, lora0:lora0 + n_lora], ((0, 0), (0, LORA_PAD - n_lora))),
        w[:, ATTN_WIDTH:ATTN_COLS]], axis=1).astype(jnp.bfloat16)
    col_q = 3 * RWKV_WIDTH
    col_l = col_q + ATTN_WIDTH
    col_kv = col_l + LORA_PAD
    n_in = col_kv + 2 * KV_WIDTH
    p = norm_matmul(ht, norm1_g[0], w_in_b, 512, n_in // 2)

    y_rwkv = rwkv7_mix(p, B, LP, 0, RWKV_WIDTH, 2 * RWKV_WIDTH, col_l,
                       *rwkv_params(shift_mu[0], w0[0], w_up[0], a0[0], a_up[0], g_up[0], k_k[0], k_a[0], r_k[0],
                                    gn_w[0], gn_b[0]))
    p3 = p.reshape(B, LP, n_in)
    q = p3[..., col_q:col_q + ATTN_WIDTH]
    k = p3[..., col_kv:col_kv + KV_WIDTH]
    v = p3[..., col_kv + KV_WIDTH:]
    y_attn = _attention(q, k, v, q_gain[0], k_gain[0], sinks[0])
    mix = jnp.concatenate([y_rwkv, y_attn.reshape(T, ATTN_WIDTH).astype(jnp.bfloat16)], axis=-1)
    h2 = matmul_residual(mix, w_out[0].astype(jnp.bfloat16), ht, 512, 1024)

    pq = norm_matmul(h2, norm2_g[0], peer_query[0].astype(jnp.bfloat16), 512, 1024)
    xn = _rms(h2, norm2_g[0])
    idx, gate = _peer_select(pq, peer_sub_keys[0])
    table = jnp.concatenate([peer_down[0], peer_up[0]], axis=1)[:, None, :]
    gate_t = gate.reshape(T // PEER_TOKENS_PER_STEP, PEER_TOKENS_PER_STEP, PEER_PICKS).transpose(0, 2, 1)
    out = peer_mix(idx, xn, gate_t, h2, table).reshape(B, LP, D)
    out = jnp.where(valid[None, :, None], out, jnp.zeros_like(out))
    return out[:, BLOCK:]
```

```python
import functools

import jax
import jax.numpy as jnp
from jax import lax
from jax.experimental import pallas as pl
from jax.experimental.pallas import tpu as pltpu

N_META = 16
BLOCK = 128
PAD_LEFT = BLOCK - N_META
HEAD_DIM = 64
RWKV_WIDTH = 1024
RWKV_HEADS = RWKV_WIDTH // HEAD_DIM
ATTN_WIDTH = 1024
ATTN_Q_HEADS = ATTN_WIDTH // HEAD_DIM
ATTN_KV_HEADS = 2
KV_WIDTH = ATTN_KV_HEADS * HEAD_DIM
WINDOW = 128
ATTN_SCALE = HEAD_DIM ** -0.5
MASK_VALUE = -1e30
DECAY_LORA = 64
AAA_LORA = 64
GATE_LORA = 160
RMS_EPS = 1e-6
GN_EPS = 64e-5
ATTN_COLS = ATTN_WIDTH + 2 * KV_WIDTH
PEER_HEADS = 8
N_KEYS = 128
PEER_TOPK = 16
D_KEY = 256
PEER_BLOCK = 128

VMEM_LIMIT_BYTES = 48 * 1024 * 1024


def _norm_matmul_kernel(x_ref, g_ref, w_ref, o_ref):
    x = x_ref[...]
    ms = jnp.mean(x * x, axis=-1, keepdims=True)
    u = x * lax.rsqrt(ms + RMS_EPS) * g_ref[...]
    o_ref[...] = jnp.dot(u.astype(jnp.bfloat16), w_ref[...], preferred_element_type=jnp.float32)


def norm_matmul(x, g, w, tm, tn):
    m, k = x.shape
    n = w.shape[1]
    return pl.pallas_call(
        _norm_matmul_kernel,
        grid=(n // tn, m // tm),
        in_specs=[
            pl.BlockSpec((tm, k), lambda j, i: (i, 0)),
            pl.BlockSpec((1, k), lambda j, i: (0, 0)),
            pl.BlockSpec((k, tn), lambda j, i: (0, j)),
        ],
        out_specs=pl.BlockSpec((tm, tn), lambda j, i: (i, j)),
        out_shape=jax.ShapeDtypeStruct((m, n), jnp.float32),
        compiler_params=pltpu.CompilerParams(
            dimension_semantics=("arbitrary", "arbitrary"), vmem_limit_bytes=VMEM_LIMIT_BYTES),
        name="norm_matmul",
    )(x, g.reshape(1, k), w)


def _matmul_residual_kernel(x_ref, w_ref, r_ref, o_ref):
    o_ref[...] = r_ref[...] + jnp.dot(
        x_ref[...].astype(jnp.bfloat16), w_ref[...], preferred_element_type=jnp.float32)


def matmul_residual(x, w, r, tm, tn):
    m, k = x.shape
    n = w.shape[1]
    return pl.pallas_call(
        _matmul_residual_kernel,
        grid=(n // tn, m // tm),
        in_specs=[
            pl.BlockSpec((tm, k), lambda j, i: (i, 0)),
            pl.BlockSpec((k, tn), lambda j, i: (0, j)),
            pl.BlockSpec((tm, tn), lambda j, i: (i, j)),
        ],
        out_specs=pl.BlockSpec((tm, tn), lambda j, i: (i, j)),
        out_shape=jax.ShapeDtypeStruct((m, n), jnp.float32),
        compiler_params=pltpu.CompilerParams(
            dimension_semantics=("arbitrary", "arbitrary"), vmem_limit_bytes=VMEM_LIMIT_BYTES),
        name="matmul_residual",
    )(x, w, r)


RWKV_CHUNK = 64
LANES = 128
RWKV_TILES = RWKV_WIDTH // LANES
LORA_PAD = 512


def _dot(a, b, dims):
    return lax.dot_general(a, b, (dims, ((), ())), preferred_element_type=jnp.float32)


def _mm(a, b):
    return _dot(a, b, ((1,), (0,)))


def _mm_nt(a, b):
    return _dot(a, b, ((1,), (1,)))


def _mm_tn(a, b):
    return _dot(a, b, ((0,), (0,)))


def _iota2(shape, axis):
    return lax.broadcasted_iota(jnp.int32, shape, axis)


def _sigmoid(x):
    return 1.0 / (1.0 + jnp.exp(-x))


def _bf16_pieces(x, n):
    pieces = []
    for _ in range(n):
        p = x.astype(jnp.bfloat16)
        pieces.append(p)
        x = x - p.astype(jnp.float32)
    return pieces


def _token_shift(x, prev_ref, mu):
    rolled = pltpu.roll(x, 1, 0)
    prev = jnp.where(_iota2(x.shape, 0) == 0, prev_ref[...], rolled)
    prev_ref[...] = x[x.shape[0] - 1:, :]
    return x + mu * (prev - x)


def _rwkv_kernel(xr_ref, xk_ref, xv_ref, xl_ref, mu_ref, mul_ref, vec_ref, wup_ref, aup_ref, gup_ref,
                 o_ref, s_ref, pr_ref, pk_ref, pv_ref, pl_ref):
    C = RWKV_CHUNK
    f32, bf16 = jnp.float32, jnp.bfloat16

    @pl.when(pl.program_id(1) == 0)
    def _():
        s_ref[...] = jnp.zeros_like(s_ref)
        pr_ref[...] = jnp.zeros_like(pr_ref)
        pk_ref[...] = jnp.zeros_like(pk_ref)
        pv_ref[...] = jnp.zeros_like(pv_ref)
        pl_ref[...] = jnp.zeros_like(pl_ref)

    r = _token_shift(xr_ref[...], pr_ref, mu_ref[0:1, :])
    kraw = _token_shift(xk_ref[...], pk_ref, mu_ref[1:2, :])
    v = _token_shift(xv_ref[...], pv_ref, mu_ref[2:3, :])
    xl = _token_shift(xl_ref[...], pl_ref, mul_ref[...])

    w0, a0, k_k, k_a = vec_ref[0:1, :], vec_ref[1:2, :], vec_ref[2:3, :], vec_ref[3:4, :]
    r_k, gn_w, gn_b = vec_ref[4:5, :], vec_ref[5:6, :], vec_ref[6:7, :]

    x_wa = xl[:, :LANES]
    wl = w0 + _mm(jnp.tanh(x_wa).astype(jnp.bfloat16), wup_ref[...])
    z = -wl
    softplus = jnp.maximum(z, 0.0) + jnp.log(1.0 + jnp.exp(-jnp.abs(z)))
    lw = -jnp.exp(-softplus - 0.5)
    a = _sigmoid(a0 + _mm(x_wa.astype(jnp.bfloat16), aup_ref[...]))
    g = _mm(_sigmoid(xl[:, LANES:LANES + 256]).astype(jnp.bfloat16), gup_ref[...])

    tri = (_iota2((C, C), 0) >= _iota2((C, C), 1)).astype(bf16)
    cl = sum(_mm(tri, piece) for piece in _bf16_pieces(lw, 3))
    e_pos = jnp.exp(cl)
    e_excl = jnp.exp(cl - lw)
    e_neg = 1.0 / e_pos
    e_end = e_pos[C - 1:, :]
    e_tail = e_end * e_neg

    lane = _iota2((LANES, LANES), 1)
    row = _iota2((LANES, LANES), 0)
    same_head = (lane >= HEAD_DIM) == (row >= HEAD_DIM)
    seg = same_head.astype(bf16)
    eye = (lane == row).astype(f32)
    t_row, t_col = row & (C - 1), lane & (C - 1)
    strict = t_col < t_row
    incl = t_col <= t_row
    incl2 = jnp.concatenate([incl, incl], axis=1)
    lane_c = _iota2((C, LANES), 1)
    head0 = lane_c < HEAD_DIM

    def seg_sum(x):
        return sum(_mm(piece, seg) for piece in _bf16_pieces(x, 2))

    def by_head(x):
        return jnp.concatenate([jnp.where(head0, x, 0.0), jnp.where(head0, 0.0, x)], axis=0).astype(bf16)

    def twice(x):
        return jnp.concatenate([x, x], axis=0)

    tiles = range(RWKV_TILES)
    sls = [slice(j * LANES, (j + 1) * LANES) for j in tiles]
    kr = [kraw[:, sl] for sl in sls]
    kkr = [kr[j] * k_k[:, sls[j]] for j in tiles]
    kk_ss = [seg_sum(kkr[j] * kkr[j]) for j in tiles]
    kk = [kkr[j] / jnp.maximum(jnp.sqrt(kk_ss[j]), 1e-12) for j in tiles]
    a_t = [a[:, sl] for sl in sls]
    k2 = [kr[j] * (1.0 + (a_t[j] - 1.0) * k_a[:, sls[j]]) for j in tiles]
    r_t = [r[:, sl] for sl in sls]
    v_t = [v[:, sl] for sl in sls]
    kka = [kk[j] * a_t[j] for j in tiles]
    alpha = [-kk[j] * e_excl[:, sls[j]] for j in tiles]
    r_dec = [r_t[j] * e_pos[:, sls[j]] for j in tiles]
    beta = [kka[j] * e_neg[:, sls[j]] for j in tiles]
    k_neg = [k2[j] * e_neg[:, sls[j]] for j in tiles]
    s_old = [s_ref[j] for j in tiles]

    gram = [_mm_nt(jnp.concatenate([by_head(alpha[j]), by_head(r_dec[j])], axis=0),
                   jnp.concatenate([by_head(beta[j]), by_head(k_neg[j])], axis=0)) for j in tiles]
    m1 = [_mm_nt(jnp.concatenate([alpha[j], r_dec[j]], axis=0).astype(bf16), s_old[j].astype(bf16))
          for j in tiles]
    a_s = [jnp.where(strict, gram[j][:LANES, :LANES], 0.0) for j in tiles]
    b_s = [jnp.where(strict, gram[j][:LANES, LANES:], 0.0).astype(bf16) for j in tiles]
    r_i = [jnp.where(incl2, gram[j][LANES:, :], 0.0).astype(bf16) for j in tiles]
    vv = [twice(v_t[j]).astype(bf16) for j in tiles]
    rhs = [twice(m1[j][:C]) + _mm(b_s[j], vv[j]) for j in tiles]
    t_inv = [eye + a_s[j] for j in tiles]
    pw = [a_s[j].astype(bf16) for j in tiles]
    for _ in range(5):
        pw = [_mm(pw[j], pw[j]).astype(bf16) for j in tiles]
        t_inv = [t_inv[j] + _mm(t_inv[j].astype(bf16), pw[j]) for j in tiles]
    u_st = [_mm(t_inv[j].astype(bf16), rhs[j].astype(bf16)) for j in tiles]
    y_st = [twice(m1[j][C:]) + _mm(r_i[j], jnp.concatenate([u_st[j].astype(bf16), vv[j]], axis=0))
            for j in tiles]
    u_t = [jnp.where(head0, u_st[j][:C], u_st[j][C:]) for j in tiles]
    y_t = [jnp.where(head0, y_st[j][:C], y_st[j][C:]) for j in tiles]

    upd = [_mm_tn(jnp.concatenate([u_t[j], v_t[j]], axis=0).astype(bf16),
                  jnp.concatenate([kka[j] * e_tail[:, sls[j]], k2[j] * e_tail[:, sls[j]]], axis=0).astype(bf16))
           for j in tiles]
    for j in tiles:
        s_ref[j] = s_old[j] * e_end[:, sls[j]] + jnp.where(same_head, upd[j], 0.0)

    mu_y = [seg_sum(y_t[j]) * (1.0 / HEAD_DIM) for j in tiles]
    dev = [y_t[j] - mu_y[j] for j in tiles]
    var = [seg_sum(dev[j] * dev[j]) * (1.0 / HEAD_DIM) for j in tiles]
    bonus = [seg_sum(r_t[j] * k2[j] * r_k[:, sls[j]]) * v_t[j] for j in tiles]
    outs = [(dev[j] * lax.rsqrt(var[j] + GN_EPS) * gn_w[:, sls[j]] + gn_b[:, sls[j]] + bonus[j]) * g[:, sls[j]]
            for j in tiles]
    o_ref[...] = jnp.concatenate(outs, axis=1).astype(o_ref.dtype)


def rwkv_params(shift_mu, w0, w_up, a0, a_up, g_up, k_k, k_a, r_k, gn_w, gn_b):
    W = RWKV_WIDTH
    n_lora = DECAY_LORA + AAA_LORA + GATE_LORA
    mu_rkv = shift_mu[:3 * W].reshape(3, W)
    mu_l = jnp.pad(shift_mu[3 * W:], (0, LORA_PAD - n_lora)).reshape(1, LORA_PAD)
    vecs = jnp.stack([w0, a0, k_k, k_a, r_k.reshape(W), gn_w, gn_b, jnp.zeros_like(w0)])
    w_up_p = jnp.pad(w_up, ((0, LANES - DECAY_LORA), (0, 0))).astype(jnp.bfloat16)
    a_up_p = jnp.pad(a_up, ((DECAY_LORA, LANES - DECAY_LORA - AAA_LORA), (0, 0))).astype(jnp.bfloat16)
    g_up_p = jnp.pad(g_up, ((0, 256 - GATE_LORA), (0, 0))).astype(jnp.bfloat16)
    return mu_rkv, mu_l, vecs, w_up_p, a_up_p, g_up_p


def rwkv7_mix(p, B, LP, col_r, col_k, col_v, col_l, mu_rkv, mu_l, vecs, w_up, a_up, g_up):
    C = RWKV_CHUNK
    nc = LP // C
    W = RWKV_WIDTH

    def col_spec(width, col):
        return pl.BlockSpec((C, width), lambda b, c: (b * nc + c, col // width))

    def full(shape):
        return pl.BlockSpec(shape, lambda b, c: (0,) * len(shape))

    return pl.pallas_call(
        _rwkv_kernel,
        grid=(B, nc),
        in_specs=[
            col_spec(W, col_r), col_spec(W, col_k), col_spec(W, col_v), col_spec(LORA_PAD, col_l),
            full(mu_rkv.shape), full(mu_l.shape), full(vecs.shape),
            full(w_up.shape), full(a_up.shape), full(g_up.shape),
        ],
        out_specs=pl.BlockSpec((C, W), lambda b, c: (b * nc + c, 0)),
        out_shape=jax.ShapeDtypeStruct((B * LP, W), jnp.bfloat16),
        scratch_shapes=[
            pltpu.VMEM((RWKV_TILES, LANES, LANES), jnp.float32),
            pltpu.VMEM((1, W), jnp.float32), pltpu.VMEM((1, W), jnp.float32), pltpu.VMEM((1, W), jnp.float32),
            pltpu.VMEM((1, LORA_PAD), jnp.float32),
        ],
        compiler_params=pltpu.CompilerParams(
            dimension_semantics=("arbitrary", "arbitrary"), vmem_limit_bytes=VMEM_LIMIT_BYTES),
        name="rwkv7_mix",
    )(p, p, p, p, mu_rkv, mu_l, vecs, w_up, a_up, g_up)


PEER_PICKS = PEER_HEADS * PEER_TOPK
PEER_TOKENS_PER_STEP = 128
PEER_GROUP = 4
PEER_GROUPS_PER_STEP = PEER_TOKENS_PER_STEP // PEER_GROUP


def _peer_mix_kernel(idx_ref, x_ref, gate_t_ref, res_ref, tab_ref, o_ref, buf, sem):
    D = x_ref.shape[1]
    G = PEER_GROUP
    gate_t = gate_t_ref[0]
    tok_lane = _iota2(gate_t.shape, 1)

    def issue(group, half):
        for s in range(G):
            for j in range(PEER_PICKS):
                e = idx_ref[group * G + s, j]
                pltpu.make_async_copy(
                    tab_ref.at[e], buf.at[half * G + s, pl.ds(j, 1), :], sem.at[half]).start()

    def wait_group(half):
        view = buf.at[pl.ds(half * G, G)]
        pltpu.make_async_copy(view, view, sem.at[half]).wait()

    def mix_group(group, half):
        for s in range(G):
            tt = group * G + s
            down = buf[half * G + s, :, :D]
            up = buf[half * G + s, :, D:]
            h = jnp.sum(down * x_ref[pl.ds(tt, 1), :], axis=1, keepdims=True)
            gate = jnp.sum(jnp.where(tok_lane == tt, gate_t, 0.0), axis=1, keepdims=True)
            w = 0.5 * h * (1.0 + lax.erf(h * (2.0 ** -0.5))) * gate
            o_ref[pl.ds(tt, 1), :] = res_ref[pl.ds(tt, 1), :] + jnp.sum(w * up, axis=0, keepdims=True)

    def step(group, half, prefetch):
        if prefetch:
            issue(group + 1, 1 - half)
        wait_group(half)
        mix_group(group, half)

    issue(0, 0)
    n_pairs = PEER_GROUPS_PER_STEP // 2

    def pair(gp, carry):
        step(2 * gp, 0, True)
        step(2 * gp + 1, 1, True)
        return carry

    lax.fori_loop(0, n_pairs - 1, pair, 0)
    step(PEER_GROUPS_PER_STEP - 2, 0, True)
    step(PEER_GROUPS_PER_STEP - 1, 1, False)


def peer_mix(idx, xn, gate_t, res, table):
    T, D = xn.shape
    tb = PEER_TOKENS_PER_STEP

    def tok(width):
        return pl.BlockSpec((tb, width), lambda i: (i, 0))

    return pl.pallas_call(
        _peer_mix_kernel,
        grid=(T // tb,),
        in_specs=[
            pl.BlockSpec((tb, PEER_PICKS), lambda i: (i, 0), memory_space=pltpu.SMEM),
            tok(D),
            pl.BlockSpec((1, PEER_PICKS, tb), lambda i: (i, 0, 0)),
            tok(D),
            pl.BlockSpec(memory_space=pl.ANY),
        ],
        out_specs=tok(D),
        out_shape=jax.ShapeDtypeStruct((T, D), jnp.float32),
        scratch_shapes=[
            pltpu.VMEM((2 * PEER_GROUP, PEER_PICKS, 2 * D), jnp.float32),
            pltpu.SemaphoreType.DMA((2,)),
        ],
        compiler_params=pltpu.CompilerParams(
            dimension_semantics=("arbitrary",), vmem_limit_bytes=VMEM_LIMIT_BYTES),
        name="peer_mix",
    )(idx, xn, gate_t, res, table)


def _rms(x, g):
    return x * lax.rsqrt(jnp.mean(x * x, axis=-1, keepdims=True) + RMS_EPS) * g


def _attention(q, k, v, q_gain, k_gain, sinks):
    B, LP, _ = q.shape
    NB = LP // BLOCK
    KVH = ATTN_KV_HEADS
    G = ATTN_Q_HEADS // KVH
    q = _rms(q.reshape(B, LP, ATTN_Q_HEADS, HEAD_DIM), q_gain).reshape(B, NB, BLOCK, KVH, G, HEAD_DIM)
    k = _rms(k.reshape(B, LP, KVH, HEAD_DIM), k_gain).reshape(B, NB, BLOCK, KVH, HEAD_DIM)
    v = v.reshape(B, NB, BLOCK, KVH, HEAD_DIM)

    def with_prev(t):
        prev = jnp.pad(t, ((0, 0), (1, 0), (0, 0), (0, 0), (0, 0)))[:, :-1]
        return jnp.concatenate([prev, t], axis=2)

    kw, vw = with_prev(k), with_prev(v)
    s = jnp.einsum('bnqkgd,bnskd->bnkgqs', q, kw, preferred_element_type=jnp.float32) * ATTN_SCALE
    blk = jnp.arange(NB)[:, None, None]
    qpos = blk * BLOCK + jnp.arange(BLOCK)[None, :, None]
    kpos = (blk - 1) * BLOCK + jnp.arange(2 * BLOCK)[None, None, :]
    dist = qpos - kpos
    mask = (dist >= 0) & (dist < WINDOW) & (kpos >= PAD_LEFT)
    s = jnp.where(mask[None, :, None, None], s, MASK_VALUE)
    sink = jnp.broadcast_to(sinks.reshape(1, 1, KVH, G, 1, 1), s.shape[:-1] + (1,))
    p = jax.nn.softmax(jnp.concatenate([s, sink], axis=-1), axis=-1)[..., :-1]
    o = jnp.einsum('bnkgqs,bnskd->bnqkgd', p, vw)
    return o.reshape(B, LP, ATTN_WIDTH)


def _peer_select(q, peer_sub_keys):
    T = q.shape[0]
    K = PEER_TOPK
    q = q.reshape(T, PEER_HEADS, 2, D_KEY // 2)
    scores = jnp.einsum('thcd,hcnd->thcn', q, peer_sub_keys, preferred_element_type=jnp.float32)
    s_half, i_half = lax.top_k(scores, K)
    cand_s = (s_half[:, :, 0, :, None] + s_half[:, :, 1, None, :]).reshape(T, PEER_HEADS, K * K)
    cand_i = (i_half[:, :, 0, :, None] * N_KEYS + i_half[:, :, 1, None, :]).reshape(T, PEER_HEADS, K * K)
    top_s, top_pos = lax.top_k(cand_s, K)
    idx = jnp.take_along_axis(cand_i, top_pos, axis=-1)
    gate = jax.nn.softmax(top_s, axis=-1)
    return idx.reshape(T, PEER_PICKS), gate.reshape(T, PEER_PICKS)


def kernel(x, meta_tokens, norm1_g, w_in, shift_mu, w0, w_up, a0, a_up, g_up, k_k, k_a, r_k, gn_w, gn_b,
           q_gain, k_gain, sinks, w_out, norm2_g, peer_query, peer_sub_keys, peer_down, peer_up):
    B, S, D = x.shape
    LP = S + BLOCK
    T = B * LP
    meta = jnp.broadcast_to(meta_tokens[None], (B, N_META, D))
    h = jnp.concatenate([jnp.zeros((B, PAD_LEFT, D), x.dtype), meta, x], axis=1)
    valid = jnp.arange(LP) >= PAD_LEFT
    ht = h.reshape(T, D)

    w = w_in[0]
    n_lora = DECAY_LORA + AAA_LORA + GATE_LORA
    rwkv0 = ATTN_COLS
    lora0 = rwkv0 + 3 * RWKV_WIDTH
    w_in_b = jnp.concatenate([
        w[:, rwkv0:lora0], w[:, :ATTN_WIDTH],
        jnp.pad(w[:, lora0:lora0 + n_lora], ((0, 0), (0, LORA_PAD - n_lora))),
        w[:, ATTN_WIDTH:ATTN_COLS]], axis=1).astype(jnp.bfloat16)
    col_q = 3 * RWKV_WIDTH
    col_l = col_q + ATTN_WIDTH
    col_kv = col_l + LORA_PAD
    n_in = col_kv + 2 * KV_WIDTH
    p = norm_matmul(ht, norm1_g[0], w_in_b, 512, n_in // 2)

    y_rwkv = rwkv7_mix(p, B, LP, 0, RWKV_WIDTH, 2 * RWKV_WIDTH, col_l,
                       *rwkv_params(shift_mu[0], w0[0], w_up[0], a0[0], a_up[0], g_up[0], k_k[0], k_a[0], r_k[0],
                                    gn_w[0], gn_b[0]))
    p3 = p.reshape(B, LP, n_in)
    q = p3[..., col_q:col_q + ATTN_WIDTH]
    k = p3[..., col_kv:col_kv + KV_WIDTH]
    v = p3[..., col_kv + KV_WIDTH:]
    y_attn = _attention(q, k, v, q_gain[0], k_gain[0], sinks[0])
    mix = jnp.concatenate([y_rwkv, y_attn.reshape(T, ATTN_WIDTH).astype(jnp.bfloat16)], axis=-1)
    h2 = matmul_residual(mix, w_out[0].astype(jnp.bfloat16), ht, 512, 1024)

    pq = norm_matmul(h2, norm2_g[0], peer_query[0].astype(jnp.bfloat16), 512, 1024)
    xn = _rms(h2, norm2_g[0])
    idx, gate = _peer_select(pq, peer_sub_keys[0])
    table = jnp.concatenate([peer_down[0], peer_up[0]], axis=1)[:, None, :]
    gate_t = gate.reshape(T // PEER_TOKENS_PER_STEP, PEER_TOKENS_PER_STEP, PEER_PICKS).transpose(0, 2, 1)
    out = peer_mix(idx, xn, gate_t, h2, table).reshape(B, LP, D)
    out = jnp.where(valid[None, :, None], out, jnp.zeros_like(out))
    return out[:, BLOCK:]
```

```python
import functools

import jax
import jax.numpy as jnp
from jax import lax
from jax.experimental import pallas as pl
from jax.experimental.pallas import tpu as pltpu

N_META = 16
BLOCK = 128
PAD_LEFT = BLOCK - N_META
HEAD_DIM = 64
RWKV_WIDTH = 1024
RWKV_HEADS = RWKV_WIDTH // HEAD_DIM
ATTN_WIDTH = 1024
ATTN_Q_HEADS = ATTN_WIDTH // HEAD_DIM
ATTN_KV_HEADS = 2
KV_WIDTH = ATTN_KV_HEADS * HEAD_DIM
WINDOW = 128
ATTN_SCALE = HEAD_DIM ** -0.5
MASK_VALUE = -1e30
DECAY_LORA = 64
AAA_LORA = 64
GATE_LORA = 160
RMS_EPS = 1e-6
GN_EPS = 64e-5
ATTN_COLS = ATTN_WIDTH + 2 * KV_WIDTH
PEER_HEADS = 8
N_KEYS = 128
PEER_TOPK = 16
D_KEY = 256
PEER_BLOCK = 128

VMEM_LIMIT_BYTES = 48 * 1024 * 1024


def _norm_matmul_kernel(x_ref, g_ref, w_ref, o_ref):
    x = x_ref[...]
    ms = jnp.mean(x * x, axis=-1, keepdims=True)
    u = x * lax.rsqrt(ms + RMS_EPS) * g_ref[...]
    o_ref[...] = jnp.dot(
        u.astype(jnp.bfloat16), w_ref[...], preferred_element_type=jnp.float32).astype(o_ref.dtype)


def norm_matmul(x, g, w, tm, tn, out_dtype=jnp.float32):
    m, k = x.shape
    n = w.shape[1]
    return pl.pallas_call(
        _norm_matmul_kernel,
        grid=(n // tn, m // tm),
        in_specs=[
            pl.BlockSpec((tm, k), lambda j, i: (i, 0)),
            pl.BlockSpec((1, k), lambda j, i: (0, 0)),
            pl.BlockSpec((k, tn), lambda j, i: (0, j)),
        ],
        out_specs=pl.BlockSpec((tm, tn), lambda j, i: (i, j)),
        out_shape=jax.ShapeDtypeStruct((m, n), out_dtype),
        compiler_params=pltpu.CompilerParams(
            dimension_semantics=("arbitrary", "arbitrary"), vmem_limit_bytes=VMEM_LIMIT_BYTES),
        name="norm_matmul",
    )(x, g.reshape(1, k), w)


def _matmul_residual_kernel(x_ref, w_ref, r_ref, o_ref):
    o_ref[...] = r_ref[...] + jnp.dot(
        x_ref[...].astype(jnp.bfloat16), w_ref[...], preferred_element_type=jnp.float32)


def matmul_residual(x, w, r, tm, tn):
    m, k = x.shape
    n = w.shape[1]
    return pl.pallas_call(
        _matmul_residual_kernel,
        grid=(n // tn, m // tm),
        in_specs=[
            pl.BlockSpec((tm, k), lambda j, i: (i, 0)),
            pl.BlockSpec((k, tn), lambda j, i: (0, j)),
            pl.BlockSpec((tm, tn), lambda j, i: (i, j)),
        ],
        out_specs=pl.BlockSpec((tm, tn), lambda j, i: (i, j)),
        out_shape=jax.ShapeDtypeStruct((m, n), jnp.float32),
        compiler_params=pltpu.CompilerParams(
            dimension_semantics=("arbitrary", "arbitrary"), vmem_limit_bytes=VMEM_LIMIT_BYTES),
        name="matmul_residual",
    )(x, w, r)


RWKV_CHUNK = 64
LANES = 128
RWKV_TILES = RWKV_WIDTH // LANES
LORA_PAD = 512


def _dot(a, b, dims):
    return lax.dot_general(a, b, (dims, ((), ())), preferred_element_type=jnp.float32)


def _mm(a, b):
    return _dot(a, b, ((1,), (0,)))


def _mm_nt(a, b):
    return _dot(a, b, ((1,), (1,)))


def _mm_tn(a, b):
    return _dot(a, b, ((0,), (0,)))


def _iota2(shape, axis):
    return lax.broadcasted_iota(jnp.int32, shape, axis)


def _sigmoid(x):
    return 1.0 / (1.0 + jnp.exp(-x))


def _bf16_pieces(x, n):
    pieces = []
    for _ in range(n):
        p = x.astype(jnp.bfloat16)
        pieces.append(p)
        x = x - p.astype(jnp.float32)
    return pieces


def _token_shift(x, prev_ref, mu):
    rolled = pltpu.roll(x, 1, 0)
    prev = jnp.where(_iota2(x.shape, 0) == 0, prev_ref[...], rolled)
    prev_ref[...] = x[x.shape[0] - 1:, :]
    return x + mu * (prev - x)


def _rwkv_kernel(xr_ref, xk_ref, xv_ref, xl_ref, mu_ref, mul_ref, vec_ref, wup_ref, aup_ref, gup_ref,
                 o_ref, s_ref, pr_ref, pk_ref, pv_ref, pl_ref):
    C = RWKV_CHUNK
    f32, bf16 = jnp.float32, jnp.bfloat16

    @pl.when(pl.program_id(1) == 0)
    def _():
        s_ref[...] = jnp.zeros_like(s_ref)
        pr_ref[...] = jnp.zeros_like(pr_ref)
        pk_ref[...] = jnp.zeros_like(pk_ref)
        pv_ref[...] = jnp.zeros_like(pv_ref)
        pl_ref[...] = jnp.zeros_like(pl_ref)

    r = _token_shift(xr_ref[...], pr_ref, mu_ref[0:1, :])
    kraw = _token_shift(xk_ref[...], pk_ref, mu_ref[1:2, :])
    v = _token_shift(xv_ref[...], pv_ref, mu_ref[2:3, :])
    xl = _token_shift(xl_ref[...], pl_ref, mul_ref[...])

    w0, a0, k_k, k_a = vec_ref[0:1, :], vec_ref[1:2, :], vec_ref[2:3, :], vec_ref[3:4, :]
    r_k, gn_w, gn_b = vec_ref[4:5, :], vec_ref[5:6, :], vec_ref[6:7, :]

    x_wa = xl[:, :LANES]
    wl = w0 + _mm(jnp.tanh(x_wa).astype(jnp.bfloat16), wup_ref[...])
    z = -wl
    softplus = jnp.maximum(z, 0.0) + jnp.log(1.0 + jnp.exp(-jnp.abs(z)))
    lw = -jnp.exp(-softplus - 0.5)
    a = _sigmoid(a0 + _mm(x_wa.astype(jnp.bfloat16), aup_ref[...]))
    g = _mm(_sigmoid(xl[:, LANES:LANES + 256]).astype(jnp.bfloat16), gup_ref[...])

    tri = (_iota2((C, C), 0) >= _iota2((C, C), 1)).astype(bf16)
    cl = sum(_mm(tri, piece) for piece in _bf16_pieces(lw, 3))
    e_pos = jnp.exp(cl)
    e_excl = jnp.exp(cl - lw)
    e_neg = 1.0 / e_pos
    e_end = e_pos[C - 1:, :]
    e_tail = e_end * e_neg

    lane = _iota2((LANES, LANES), 1)
    row = _iota2((LANES, LANES), 0)
    same_head = (lane >= HEAD_DIM) == (row >= HEAD_DIM)
    seg = same_head.astype(bf16)
    eye = (lane == row).astype(f32)
    t_row, t_col = row & (C - 1), lane & (C - 1)
    strict = t_col < t_row
    incl = t_col <= t_row
    incl2 = jnp.concatenate([incl, incl], axis=1)
    lane_c = _iota2((C, LANES), 1)
    head0 = lane_c < HEAD_DIM

    def seg_sum(x):
        return sum(_mm(piece, seg) for piece in _bf16_pieces(x, 2))

    def by_head(x):
        return jnp.concatenate([jnp.where(head0, x, 0.0), jnp.where(head0, 0.0, x)], axis=0).astype(bf16)

    def twice(x):
        return jnp.concatenate([x, x], axis=0)

    tiles = range(RWKV_TILES)
    sls = [slice(j * LANES, (j + 1) * LANES) for j in tiles]
    kr = [kraw[:, sl] for sl in sls]
    kkr = [kr[j] * k_k[:, sls[j]] for j in tiles]
    kk_ss = [seg_sum(kkr[j] * kkr[j]) for j in tiles]
    kk = [kkr[j] / jnp.maximum(jnp.sqrt(kk_ss[j]), 1e-12) for j in tiles]
    a_t = [a[:, sl] for sl in sls]
    k2 = [kr[j] * (1.0 + (a_t[j] - 1.0) * k_a[:, sls[j]]) for j in tiles]
    r_t = [r[:, sl] for sl in sls]
    v_t = [v[:, sl] for sl in sls]
    kka = [kk[j] * a_t[j] for j in tiles]
    alpha = [-kk[j] * e_excl[:, sls[j]] for j in tiles]
    r_dec = [r_t[j] * e_pos[:, sls[j]] for j in tiles]
    beta = [kka[j] * e_neg[:, sls[j]] for j in tiles]
    k_neg = [k2[j] * e_neg[:, sls[j]] for j in tiles]
    s_old = [s_ref[j] for j in tiles]

    gram = [_mm_nt(jnp.concatenate([by_head(alpha[j]), by_head(r_dec[j])], axis=0),
                   jnp.concatenate([by_head(beta[j]), by_head(k_neg[j])], axis=0)) for j in tiles]
    m1 = [_mm_nt(jnp.concatenate([alpha[j], r_dec[j]], axis=0).astype(bf16), s_old[j].astype(bf16))
          for j in tiles]
    a_s = [jnp.where(strict, gram[j][:LANES, :LANES], 0.0) for j in tiles]
    b_s = [jnp.where(strict, gram[j][:LANES, LANES:], 0.0).astype(bf16) for j in tiles]
    r_i = [jnp.where(incl2, gram[j][LANES:, :], 0.0).astype(bf16) for j in tiles]
    vv = [twice(v_t[j]).astype(bf16) for j in tiles]
    rhs = [twice(m1[j][:C]) + _mm(b_s[j], vv[j]) for j in tiles]
    t_inv = [eye + a_s[j] for j in tiles]
    pw = [a_s[j].astype(bf16) for j in tiles]
    for _ in range(5):
        pw = [_mm(pw[j], pw[j]).astype(bf16) for j in tiles]
        t_inv = [t_inv[j] + _mm(t_inv[j].astype(bf16), pw[j]) for j in tiles]
    u_st = [_mm(t_inv[j].astype(bf16), rhs[j].astype(bf16)) for j in tiles]
    y_st = [twice(m1[j][C:]) + _mm(r_i[j], jnp.concatenate([u_st[j].astype(bf16), vv[j]], axis=0))
            for j in tiles]
    u_t = [jnp.where(head0, u_st[j][:C], u_st[j][C:]) for j in tiles]
    y_t = [jnp.where(head0, y_st[j][:C], y_st[j][C:]) for j in tiles]

    upd = [_mm_tn(jnp.concatenate([u_t[j], v_t[j]], axis=0).astype(bf16),
                  jnp.concatenate([kka[j] * e_tail[:, sls[j]], k2[j] * e_tail[:, sls[j]]], axis=0).astype(bf16))
           for j in tiles]
    for j in tiles:
        s_ref[j] = s_old[j] * e_end[:, sls[j]] + jnp.where(same_head, upd[j], 0.0)

    mu_y = [seg_sum(y_t[j]) * (1.0 / HEAD_DIM) for j in tiles]
    dev = [y_t[j] - mu_y[j] for j in tiles]
    var = [seg_sum(dev[j] * dev[j]) * (1.0 / HEAD_DIM) for j in tiles]
    bonus = [seg_sum(r_t[j] * k2[j] * r_k[:, sls[j]]) * v_t[j] for j in tiles]
    outs = [(dev[j] * lax.rsqrt(var[j] + GN_EPS) * gn_w[:, sls[j]] + gn_b[:, sls[j]] + bonus[j]) * g[:, sls[j]]
            for j in tiles]
    o_ref[...] = jnp.concatenate(outs, axis=1).astype(o_ref.dtype)


def rwkv_params(shift_mu, w0, w_up, a0, a_up, g_up, k_k, k_a, r_k, gn_w, gn_b):
    W = RWKV_WIDTH
    n_lora = DECAY_LORA + AAA_LORA + GATE_LORA
    mu_rkv = shift_mu[:3 * W].reshape(3, W)
    mu_l = jnp.pad(shift_mu[3 * W:], (0, LORA_PAD - n_lora)).reshape(1, LORA_PAD)
    vecs = jnp.stack([w0, a0, k_k, k_a, r_k.reshape(W), gn_w, gn_b, jnp.zeros_like(w0)])
    w_up_p = jnp.pad(w_up, ((0, LANES - DECAY_LORA), (0, 0))).astype(jnp.bfloat16)
    a_up_p = jnp.pad(a_up, ((DECAY_LORA, LANES - DECAY_LORA - AAA_LORA), (0, 0))).astype(jnp.bfloat16)
    g_up_p = jnp.pad(g_up, ((0, 256 - GATE_LORA), (0, 0))).astype(jnp.bfloat16)
    return mu_rkv, mu_l, vecs, w_up_p, a_up_p, g_up_p


def rwkv7_mix(p, B, LP, col_r, col_k, col_v, col_l, mu_rkv, mu_l, vecs, w_up, a_up, g_up):
    C = RWKV_CHUNK
    nc = LP // C
    W = RWKV_WIDTH

    def col_spec(width, col):
        return pl.BlockSpec((C, width), lambda b, c: (b * nc + c, col // width))

    def full(shape):
        return pl.BlockSpec(shape, lambda b, c: (0,) * len(shape))

    return pl.pallas_call(
        _rwkv_kernel,
        grid=(B, nc),
        in_specs=[
            col_spec(W, col_r), col_spec(W, col_k), col_spec(W, col_v), col_spec(LORA_PAD, col_l),
            full(mu_rkv.shape), full(mu_l.shape), full(vecs.shape),
            full(w_up.shape), full(a_up.shape), full(g_up.shape),
        ],
        out_specs=pl.BlockSpec((C, W), lambda b, c: (b * nc + c, 0)),
        out_shape=jax.ShapeDtypeStruct((B * LP, W), jnp.bfloat16),
        scratch_shapes=[
            pltpu.VMEM((RWKV_TILES, LANES, LANES), jnp.float32),
            pltpu.VMEM((1, W), jnp.float32), pltpu.VMEM((1, W), jnp.float32), pltpu.VMEM((1, W), jnp.float32),
            pltpu.VMEM((1, LORA_PAD), jnp.float32),
        ],
        compiler_params=pltpu.CompilerParams(
            dimension_semantics=("arbitrary", "arbitrary"), vmem_limit_bytes=VMEM_LIMIT_BYTES),
        name="rwkv7_mix",
    )(p, p, p, p, mu_rkv, mu_l, vecs, w_up, a_up, g_up)


PEER_PICKS = PEER_HEADS * PEER_TOPK
PEER_TOKENS_PER_STEP = 128
PEER_GROUP = 4
PEER_GROUPS_PER_STEP = PEER_TOKENS_PER_STEP // PEER_GROUP


def _peer_mix_kernel(idx_ref, x_ref, gate_t_ref, res_ref, tab_ref, o_ref, buf, sem):
    D = x_ref.shape[1]
    G = PEER_GROUP
    gate_t = gate_t_ref[0]
    tok_lane = _iota2(gate_t.shape, 1)

    def issue(group, half):
        for s in range(G):
            for j in range(PEER_PICKS):
                e = idx_ref[0, j, group * G + s]
                pltpu.make_async_copy(
                    tab_ref.at[e], buf.at[half * G + s, pl.ds(j, 1), :], sem.at[half]).start()

    def wait_group(half):
        view = buf.at[pl.ds(half * G, G)]
        pltpu.make_async_copy(view, view, sem.at[half]).wait()

    def mix_group(group, half):
        for s in range(G):
            tt = group * G + s
            down = buf[half * G + s, :, :D]
            up = buf[half * G + s, :, D:]
            h = jnp.sum(down * x_ref[pl.ds(tt, 1), :], axis=1, keepdims=True)
            gate = jnp.sum(jnp.where(tok_lane == tt, gate_t, 0.0), axis=1, keepdims=True)
            w = 0.5 * h * (1.0 + lax.erf(h * (2.0 ** -0.5))) * gate
            o_ref[pl.ds(tt, 1), :] = res_ref[pl.ds(tt, 1), :] + jnp.sum(w * up, axis=0, keepdims=True)

    def step(group, half, prefetch):
        if prefetch:
            issue(group + 1, 1 - half)
        wait_group(half)
        mix_group(group, half)

    issue(0, 0)
    n_pairs = PEER_GROUPS_PER_STEP // 2

    def pair(gp, carry):
        step(2 * gp, 0, True)
        step(2 * gp + 1, 1, True)
        return carry

    lax.fori_loop(0, n_pairs - 1, pair, 0)
    step(PEER_GROUPS_PER_STEP - 2, 0, True)
    step(PEER_GROUPS_PER_STEP - 1, 1, False)


def peer_mix(idx, xn, gate_t, res, table):
    T, D = xn.shape
    tb = PEER_TOKENS_PER_STEP

    def tok(width):
        return pl.BlockSpec((tb, width), lambda i: (i, 0))

    return pl.pallas_call(
        _peer_mix_kernel,
        grid=(T // tb,),
        in_specs=[
            pl.BlockSpec((1, PEER_PICKS, tb), lambda i: (i, 0, 0), memory_space=pltpu.SMEM),
            tok(D),
            pl.BlockSpec((1, PEER_PICKS, tb), lambda i: (i, 0, 0)),
            tok(D),
            pl.BlockSpec(memory_space=pl.ANY),
        ],
        out_specs=tok(D),
        out_shape=jax.ShapeDtypeStruct((T, D), jnp.float32),
        scratch_shapes=[
            pltpu.VMEM((2 * PEER_GROUP, PEER_PICKS, 2 * D), jnp.float32),
            pltpu.SemaphoreType.DMA((2,)),
        ],
        compiler_params=pltpu.CompilerParams(
            dimension_semantics=("arbitrary",), vmem_limit_bytes=VMEM_LIMIT_BYTES),
        name="peer_mix",
    )(idx, xn, gate_t, res, table)


def _top_rows(s, k):
    n = s.shape[0]
    rows = _iota2(s.shape, 0).astype(jnp.float32)
    vals, poss = [], []
    for _ in range(k):
        m = jnp.max(s, axis=0, keepdims=True)
        pos = jnp.min(jnp.where(s == m, rows, float(n)), axis=0, keepdims=True)
        vals.append(m)
        poss.append(pos)
        s = jnp.where(rows == pos, -jnp.inf, s)
    return jnp.concatenate(vals, axis=0), jnp.concatenate(poss, axis=0)


def _pick_rows(table, sel):
    out = jnp.zeros_like(sel)
    for r in range(table.shape[0]):
        out = out + jnp.where(sel == float(r), table[r:r + 1, :], 0.0)
    return out


def _peer_select_kernel(q_ref, keys_ref, idx_ref, gate_ref):
    K = PEER_TOPK
    half_w = D_KEY // 2
    idx_rows, gate_rows = [], []
    for h in range(PEER_HEADS):
        tops = []
        for c in range(2):
            hc = 2 * h + c
            scores = _mm_nt(keys_ref[hc], q_ref[:, hc * half_w:(hc + 1) * half_w])
            tops.append(_top_rows(scores, K))
        (s1, i1), (s2, i2) = tops
        cand = jnp.concatenate([s1[i:i + 1, :] + s2 for i in range(K)], axis=0)
        top_s, pos = _top_rows(cand, K)
        first = jnp.floor(pos * (1.0 / K))
        second = pos - first * K
        expert = _pick_rows(i1, first) * N_KEYS + _pick_rows(i2, second)
        e = jnp.exp(top_s - top_s[0:1, :])
        idx_rows.append(expert.astype(jnp.int32))
        gate_rows.append(e / jnp.sum(e, axis=0, keepdims=True))
    idx_ref[0] = jnp.concatenate(idx_rows, axis=0)
    gate_ref[0] = jnp.concatenate(gate_rows, axis=0)


def peer_select(q, keys):
    T, W = q.shape
    tb = PEER_TOKENS_PER_STEP
    out_block = pl.BlockSpec((1, PEER_PICKS, tb), lambda i: (i, 0, 0))
    return pl.pallas_call(
        _peer_select_kernel,
        grid=(T // tb,),
        in_specs=[pl.BlockSpec((tb, W), lambda i: (i, 0)), pl.BlockSpec(keys.shape, lambda i: (0, 0, 0))],
        out_specs=[out_block, out_block],
        out_shape=[jax.ShapeDtypeStruct((T // tb, PEER_PICKS, tb), jnp.int32),
                   jax.ShapeDtypeStruct((T // tb, PEER_PICKS, tb), jnp.float32)],
        compiler_params=pltpu.CompilerParams(
            dimension_semantics=("arbitrary",), vmem_limit_bytes=VMEM_LIMIT_BYTES),
        name="peer_select",
    )(q, keys)


def _rms(x, g):
    return x * lax.rsqrt(jnp.mean(x * x, axis=-1, keepdims=True) + RMS_EPS) * g


def _attention(q, k, v, q_gain, k_gain, sinks):
    B, LP, _ = q.shape
    NB = LP // BLOCK
    KVH = ATTN_KV_HEADS
    G = ATTN_Q_HEADS // KVH
    q = _rms(q.reshape(B, LP, ATTN_Q_HEADS, HEAD_DIM), q_gain).reshape(B, NB, BLOCK, KVH, G, HEAD_DIM)
    k = _rms(k.reshape(B, LP, KVH, HEAD_DIM), k_gain).reshape(B, NB, BLOCK, KVH, HEAD_DIM)
    v = v.reshape(B, NB, BLOCK, KVH, HEAD_DIM)

    def with_prev(t):
        prev = jnp.pad(t, ((0, 0), (1, 0), (0, 0), (0, 0), (0, 0)))[:, :-1]
        return jnp.concatenate([prev, t], axis=2)

    kw, vw = with_prev(k), with_prev(v)
    s = jnp.einsum('bnqkgd,bnskd->bnkgqs', q, kw, preferred_element_type=jnp.float32) * ATTN_SCALE
    blk = jnp.arange(NB)[:, None, None]
    qpos = blk * BLOCK + jnp.arange(BLOCK)[None, :, None]
    kpos = (blk - 1) * BLOCK + jnp.arange(2 * BLOCK)[None, None, :]
    dist = qpos - kpos
    mask = (dist >= 0) & (dist < WINDOW) & (kpos >= PAD_LEFT)
    s = jnp.where(mask[None, :, None, None], s, MASK_VALUE)
    sink = jnp.broadcast_to(sinks.reshape(1, 1, KVH, G, 1, 1), s.shape[:-1] + (1,))
    p = jax.nn.softmax(jnp.concatenate([s, sink], axis=-1), axis=-1)[..., :-1]
    o = jnp.einsum('bnkgqs,bnskd->bnqkgd', p, vw)
    return o.reshape(B, LP, ATTN_WIDTH)


def kernel(x, meta_tokens, norm1_g, w_in, shift_mu, w0, w_up, a0, a_up, g_up, k_k, k_a, r_k, gn_w, gn_b,
           q_gain, k_gain, sinks, w_out, norm2_g, peer_query, peer_sub_keys, peer_down, peer_up):
    B, S, D = x.shape
    LP = S + BLOCK
    T = B * LP
    meta = jnp.broadcast_to(meta_tokens[None], (B, N_META, D))
    h = jnp.concatenate([jnp.zeros((B, PAD_LEFT, D), x.dtype), meta, x], axis=1)
    valid = jnp.arange(LP) >= PAD_LEFT
    ht = h.reshape(T, D)

    w = w_in[0]
    n_lora = DECAY_LORA + AAA_LORA + GATE_LORA
    rwkv0 = ATTN_COLS
    lora0 = rwkv0 + 3 * RWKV_WIDTH
    w_in_b = jnp.concatenate([
        w[:, rwkv0:lora0], w[:, :ATTN_WIDTH],
        jnp.pad(w[:, lora0:lora0 + n_lora], ((0, 0), (0, LORA_PAD - n_lora))),
        w[:, ATTN_WIDTH:ATTN_COLS]], axis=1).astype(jnp.bfloat16)
    col_q = 3 * RWKV_WIDTH
    col_l = col_q + ATTN_WIDTH
    col_kv = col_l + LORA_PAD
    n_in = col_kv + 2 * KV_WIDTH
    p = norm_matmul(ht, norm1_g[0], w_in_b, 512, n_in // 2)

    y_rwkv = rwkv7_mix(p, B, LP, 0, RWKV_WIDTH, 2 * RWKV_WIDTH, col_l,
                       *rwkv_params(shift_mu[0], w0[0], w_up[0], a0[0], a_up[0], g_up[0], k_k[0], k_a[0], r_k[0],
                                    gn_w[0], gn_b[0]))
    p3 = p.reshape(B, LP, n_in)
    q = p3[..., col_q:col_q + ATTN_WIDTH]
    k = p3[..., col_kv:col_kv + KV_WIDTH]
    v = p3[..., col_kv + KV_WIDTH:]
    y_attn = _attention(q, k, v, q_gain[0], k_gain[0], sinks[0])
    mix = jnp.concatenate([y_rwkv, y_attn.reshape(T, ATTN_WIDTH).astype(jnp.bfloat16)], axis=-1)
    h2 = matmul_residual(mix, w_out[0].astype(jnp.bfloat16), ht, 512, 1024)

    pq = norm_matmul(h2, norm2_g[0], peer_query[0].astype(jnp.bfloat16), 512, 1024, jnp.bfloat16)
    xn = _rms(h2, norm2_g[0])
    keys = peer_sub_keys[0].reshape(2 * PEER_HEADS, N_KEYS, D_KEY // 2).astype(jnp.bfloat16)
    idx_t, gate_t = peer_select(pq, keys)
    table = jnp.concatenate([peer_down[0], peer_up[0]], axis=1)[:, None, :]
    out = peer_mix(idx_t, xn, gate_t, h2, table).reshape(B, LP, D)
    out = jnp.where(valid[None, :, None], out, jnp.zeros_like(out))
    return out[:, BLOCK:]
```

```python
import functools

import jax
import jax.numpy as jnp
from jax import lax
from jax.experimental import pallas as pl
from jax.experimental.pallas import tpu as pltpu

N_META = 16
BLOCK = 128
PAD_LEFT = BLOCK - N_META
HEAD_DIM = 64
RWKV_WIDTH = 1024
RWKV_HEADS = RWKV_WIDTH // HEAD_DIM
ATTN_WIDTH = 1024
ATTN_Q_HEADS = ATTN_WIDTH // HEAD_DIM
ATTN_KV_HEADS = 2
KV_WIDTH = ATTN_KV_HEADS * HEAD_DIM
WINDOW = 128
ATTN_SCALE = HEAD_DIM ** -0.5
MASK_VALUE = -1e30
DECAY_LORA = 64
AAA_LORA = 64
GATE_LORA = 160
RMS_EPS = 1e-6
GN_EPS = 64e-5
ATTN_COLS = ATTN_WIDTH + 2 * KV_WIDTH
PEER_HEADS = 8
N_KEYS = 128
PEER_TOPK = 16
D_KEY = 256
PEER_BLOCK = 128

VMEM_LIMIT_BYTES = 48 * 1024 * 1024


def _norm_matmul_kernel(x_ref, g_ref, w_ref, o_ref):
    x = x_ref[...]
    ms = jnp.mean(x * x, axis=-1, keepdims=True)
    u = x * lax.rsqrt(ms + RMS_EPS) * g_ref[...]
    o_ref[...] = jnp.dot(
        u.astype(jnp.bfloat16), w_ref[...], preferred_element_type=jnp.float32).astype(o_ref.dtype)


def norm_matmul(x, g, w, tm, tn, out_dtype=jnp.float32):
    m, k = x.shape
    n = w.shape[1]
    return pl.pallas_call(
        _norm_matmul_kernel,
        grid=(n // tn, m // tm),
        in_specs=[
            pl.BlockSpec((tm, k), lambda j, i: (i, 0)),
            pl.BlockSpec((1, k), lambda j, i: (0, 0)),
            pl.BlockSpec((k, tn), lambda j, i: (0, j)),
        ],
        out_specs=pl.BlockSpec((tm, tn), lambda j, i: (i, j)),
        out_shape=jax.ShapeDtypeStruct((m, n), out_dtype),
        compiler_params=pltpu.CompilerParams(
            dimension_semantics=("arbitrary", "arbitrary"), vmem_limit_bytes=VMEM_LIMIT_BYTES),
        name="norm_matmul",
    )(x, g.reshape(1, k), w)


def _matmul_residual_kernel(x_ref, w_ref, r_ref, o_ref):
    o_ref[...] = r_ref[...] + jnp.dot(
        x_ref[...].astype(jnp.bfloat16), w_ref[...], preferred_element_type=jnp.float32)


def matmul_residual(x, w, r, tm, tn):
    m, k = x.shape
    n = w.shape[1]
    return pl.pallas_call(
        _matmul_residual_kernel,
        grid=(n // tn, m // tm),
        in_specs=[
            pl.BlockSpec((tm, k), lambda j, i: (i, 0)),
            pl.BlockSpec((k, tn), lambda j, i: (0, j)),
            pl.BlockSpec((tm, tn), lambda j, i: (i, j)),
        ],
        out_specs=pl.BlockSpec((tm, tn), lambda j, i: (i, j)),
        out_shape=jax.ShapeDtypeStruct((m, n), jnp.float32),
        compiler_params=pltpu.CompilerParams(
            dimension_semantics=("arbitrary", "arbitrary"), vmem_limit_bytes=VMEM_LIMIT_BYTES),
        name="matmul_residual",
    )(x, w, r)


RWKV_CHUNK = 64
LANES = 128
RWKV_TILES = RWKV_WIDTH // LANES
LORA_PAD = 512


def _dot(a, b, dims):
    return lax.dot_general(a, b, (dims, ((), ())), preferred_element_type=jnp.float32)


def _mm(a, b):
    return _dot(a, b, ((1,), (0,)))


def _mm_nt(a, b):
    return _dot(a, b, ((1,), (1,)))


def _mm_tn(a, b):
    return _dot(a, b, ((0,), (0,)))


def _iota2(shape, axis):
    return lax.broadcasted_iota(jnp.int32, shape, axis)


def _sigmoid(x):
    return 1.0 / (1.0 + jnp.exp(-x))


def _bf16_pieces(x, n):
    pieces = []
    for _ in range(n):
        p = x.astype(jnp.bfloat16)
        pieces.append(p)
        x = x - p.astype(jnp.float32)
    return pieces


def _token_shift(x, prev_ref, mu):
    rolled = pltpu.roll(x, 1, 0)
    prev = jnp.where(_iota2(x.shape, 0) == 0, prev_ref[...], rolled)
    prev_ref[...] = x[x.shape[0] - 1:, :]
    return x + mu * (prev - x)


def _rwkv_kernel(xr_ref, xk_ref, xv_ref, xl_ref, mu_ref, mul_ref, vec_ref, wup_ref, aup_ref, gup_ref,
                 o_ref, s_ref, pr_ref, pk_ref, pv_ref, pl_ref):
    C = RWKV_CHUNK
    f32, bf16 = jnp.float32, jnp.bfloat16

    @pl.when(pl.program_id(1) == 0)
    def _():
        s_ref[...] = jnp.zeros_like(s_ref)
        pr_ref[...] = jnp.zeros_like(pr_ref)
        pk_ref[...] = jnp.zeros_like(pk_ref)
        pv_ref[...] = jnp.zeros_like(pv_ref)
        pl_ref[...] = jnp.zeros_like(pl_ref)

    r = _token_shift(xr_ref[...], pr_ref, mu_ref[0:1, :])
    kraw = _token_shift(xk_ref[...], pk_ref, mu_ref[1:2, :])
    v = _token_shift(xv_ref[...], pv_ref, mu_ref[2:3, :])
    xl = _token_shift(xl_ref[...], pl_ref, mul_ref[...])

    w0, a0, k_k, k_a = vec_ref[0:1, :], vec_ref[1:2, :], vec_ref[2:3, :], vec_ref[3:4, :]
    r_k, gn_w, gn_b = vec_ref[4:5, :], vec_ref[5:6, :], vec_ref[6:7, :]

    x_wa = xl[:, :LANES]
    wl = w0 + _mm(jnp.tanh(x_wa).astype(jnp.bfloat16), wup_ref[...])
    z = -wl
    softplus = jnp.maximum(z, 0.0) + jnp.log(1.0 + jnp.exp(-jnp.abs(z)))
    lw = -jnp.exp(-softplus - 0.5)
    a = _sigmoid(a0 + _mm(x_wa.astype(jnp.bfloat16), aup_ref[...]))
    g = _mm(_sigmoid(xl[:, LANES:LANES + 256]).astype(jnp.bfloat16), gup_ref[...])

    tri = (_iota2((C, C), 0) >= _iota2((C, C), 1)).astype(bf16)
    cl = sum(_mm(tri, piece) for piece in _bf16_pieces(lw, 3))
    e_pos = jnp.exp(cl)
    e_excl = jnp.exp(cl - lw)
    e_neg = 1.0 / e_pos
    e_end = e_pos[C - 1:, :]
    e_tail = e_end * e_neg

    lane = _iota2((LANES, LANES), 1)
    row = _iota2((LANES, LANES), 0)
    same_head = (lane >= HEAD_DIM) == (row >= HEAD_DIM)
    seg = same_head.astype(bf16)
    eye = (lane == row).astype(f32)
    t_row, t_col = row & (C - 1), lane & (C - 1)
    strict = t_col < t_row
    incl = t_col <= t_row
    incl2 = jnp.concatenate([incl, incl], axis=1)
    lane_c = _iota2((C, LANES), 1)
    head0 = lane_c < HEAD_DIM

    def seg_sum(x):
        return sum(_mm(piece, seg) for piece in _bf16_pieces(x, 2))

    def by_head(x):
        return jnp.concatenate([jnp.where(head0, x, 0.0), jnp.where(head0, 0.0, x)], axis=0).astype(bf16)

    def twice(x):
        return jnp.concatenate([x, x], axis=0)

    tiles = range(RWKV_TILES)
    sls = [slice(j * LANES, (j + 1) * LANES) for j in tiles]
    kr = [kraw[:, sl] for sl in sls]
    kkr = [kr[j] * k_k[:, sls[j]] for j in tiles]
    kk_ss = [seg_sum(kkr[j] * kkr[j]) for j in tiles]
    kk = [kkr[j] / jnp.maximum(jnp.sqrt(kk_ss[j]), 1e-12) for j in tiles]
    a_t = [a[:, sl] for sl in sls]
    k2 = [kr[j] * (1.0 + (a_t[j] - 1.0) * k_a[:, sls[j]]) for j in tiles]
    r_t = [r[:, sl] for sl in sls]
    v_t = [v[:, sl] for sl in sls]
    kka = [kk[j] * a_t[j] for j in tiles]
    alpha = [-kk[j] * e_excl[:, sls[j]] for j in tiles]
    r_dec = [r_t[j] * e_pos[:, sls[j]] for j in tiles]
    beta = [kka[j] * e_neg[:, sls[j]] for j in tiles]
    k_neg = [k2[j] * e_neg[:, sls[j]] for j in tiles]
    s_old = [s_ref[j] for j in tiles]

    gram = [_mm_nt(jnp.concatenate([by_head(alpha[j]), by_head(r_dec[j])], axis=0),
                   jnp.concatenate([by_head(beta[j]), by_head(k_neg[j])], axis=0)) for j in tiles]
    m1 = [_mm_nt(jnp.concatenate([alpha[j], r_dec[j]], axis=0).astype(bf16), s_old[j].astype(bf16))
          for j in tiles]
    a_s = [jnp.where(strict, gram[j][:LANES, :LANES], 0.0) for j in tiles]
    b_s = [jnp.where(strict, gram[j][:LANES, LANES:], 0.0).astype(bf16) for j in tiles]
    r_i = [jnp.where(incl2, gram[j][LANES:, :], 0.0).astype(bf16) for j in tiles]
    vv = [twice(v_t[j]).astype(bf16) for j in tiles]
    rhs = [twice(m1[j][:C]) + _mm(b_s[j], vv[j]) for j in tiles]
    t_inv = [eye + a_s[j] for j in tiles]
    pw = [a_s[j].astype(bf16) for j in tiles]
    for _ in range(5):
        pw = [_mm(pw[j], pw[j]).astype(bf16) for j in tiles]
        t_inv = [t_inv[j] + _mm(t_inv[j].astype(bf16), pw[j]) for j in tiles]
    u_st = [_mm(t_inv[j].astype(bf16), rhs[j].astype(bf16)) for j in tiles]
    y_st = [twice(m1[j][C:]) + _mm(r_i[j], jnp.concatenate([u_st[j].astype(bf16), vv[j]], axis=0))
            for j in tiles]
    u_t = [jnp.where(head0, u_st[j][:C], u_st[j][C:]) for j in tiles]
    y_t = [jnp.where(head0, y_st[j][:C], y_st[j][C:]) for j in tiles]

    upd = [_mm_tn(jnp.concatenate([u_t[j], v_t[j]], axis=0).astype(bf16),
                  jnp.concatenate([kka[j] * e_tail[:, sls[j]], k2[j] * e_tail[:, sls[j]]], axis=0).astype(bf16))
           for j in tiles]
    for j in tiles:
        s_ref[j] = s_old[j] * e_end[:, sls[j]] + jnp.where(same_head, upd[j], 0.0)

    mu_y = [seg_sum(y_t[j]) * (1.0 / HEAD_DIM) for j in tiles]
    dev = [y_t[j] - mu_y[j] for j in tiles]
    var = [seg_sum(dev[j] * dev[j]) * (1.0 / HEAD_DIM) for j in tiles]
    bonus = [seg_sum(r_t[j] * k2[j] * r_k[:, sls[j]]) * v_t[j] for j in tiles]
    outs = [(dev[j] * lax.rsqrt(var[j] + GN_EPS) * gn_w[:, sls[j]] + gn_b[:, sls[j]] + bonus[j]) * g[:, sls[j]]
            for j in tiles]
    o_ref[...] = jnp.concatenate(outs, axis=1).astype(o_ref.dtype)


def rwkv_params(shift_mu, w0, w_up, a0, a_up, g_up, k_k, k_a, r_k, gn_w, gn_b):
    W = RWKV_WIDTH
    n_lora = DECAY_LORA + AAA_LORA + GATE_LORA
    mu_rkv = shift_mu[:3 * W].reshape(3, W)
    mu_l = jnp.pad(shift_mu[3 * W:], (0, LORA_PAD - n_lora)).reshape(1, LORA_PAD)
    vecs = jnp.stack([w0, a0, k_k, k_a, r_k.reshape(W), gn_w, gn_b, jnp.zeros_like(w0)])
    w_up_p = jnp.pad(w_up, ((0, LANES - DECAY_LORA), (0, 0))).astype(jnp.bfloat16)
    a_up_p = jnp.pad(a_up, ((DECAY_LORA, LANES - DECAY_LORA - AAA_LORA), (0, 0))).astype(jnp.bfloat16)
    g_up_p = jnp.pad(g_up, ((0, 256 - GATE_LORA), (0, 0))).astype(jnp.bfloat16)
    return mu_rkv, mu_l, vecs, w_up_p, a_up_p, g_up_p


def rwkv7_mix(p, B, LP, col_r, col_k, col_v, col_l, mu_rkv, mu_l, vecs, w_up, a_up, g_up):
    C = RWKV_CHUNK
    nc = LP // C
    W = RWKV_WIDTH

    def col_spec(width, col):
        return pl.BlockSpec((C, width), lambda b, c: (b * nc + c, col // width))

    def full(shape):
        return pl.BlockSpec(shape, lambda b, c: (0,) * len(shape))

    return pl.pallas_call(
        _rwkv_kernel,
        grid=(B, nc),
        in_specs=[
            col_spec(W, col_r), col_spec(W, col_k), col_spec(W, col_v), col_spec(LORA_PAD, col_l),
            full(mu_rkv.shape), full(mu_l.shape), full(vecs.shape),
            full(w_up.shape), full(a_up.shape), full(g_up.shape),
        ],
        out_specs=pl.BlockSpec((C, W), lambda b, c: (b * nc + c, 0)),
        out_shape=jax.ShapeDtypeStruct((B * LP, W), jnp.bfloat16),
        scratch_shapes=[
            pltpu.VMEM((RWKV_TILES, LANES, LANES), jnp.float32),
            pltpu.VMEM((1, W), jnp.float32), pltpu.VMEM((1, W), jnp.float32), pltpu.VMEM((1, W), jnp.float32),
            pltpu.VMEM((1, LORA_PAD), jnp.float32),
        ],
        compiler_params=pltpu.CompilerParams(
            dimension_semantics=("arbitrary", "arbitrary"), vmem_limit_bytes=VMEM_LIMIT_BYTES),
        name="rwkv7_mix",
    )(p, p, p, p, mu_rkv, mu_l, vecs, w_up, a_up, g_up)


PEER_PICKS = PEER_HEADS * PEER_TOPK
PEER_TOKENS_PER_STEP = 128
PEER_GROUP = 4
PEER_GROUPS_PER_STEP = PEER_TOKENS_PER_STEP // PEER_GROUP
PEER_BUFFERS = 3


def pack_expert_table(down, up):
    def bits(t):
        return lax.bitcast_convert_type(t.astype(jnp.bfloat16), jnp.uint16).astype(jnp.uint32)
    return ((bits(down) << 16) | bits(up))[:, None, :]


def _peer_mix_kernel(idx_ref, x_ref, gate_t_ref, res_ref, tab_ref, o_ref, *scratch):
    bufs, sem = scratch[:PEER_BUFFERS], scratch[PEER_BUFFERS]
    G = PEER_GROUP
    gate_t = gate_t_ref[0]
    tok_lane = _iota2(gate_t.shape, 1)

    def issue_token(group, s, b):
        for j in range(PEER_PICKS):
            e = idx_ref[0, j, group * G + s]
            pltpu.make_async_copy(tab_ref.at[e], bufs[b].at[s, pl.ds(j, 1), :], sem.at[b]).start()

    def wait_group(b):
        pltpu.make_async_copy(bufs[b], bufs[b], sem.at[b]).wait()

    def mix_token(group, s, b):
        tt = group * G + s
        words = bufs[b][s]
        down = lax.bitcast_convert_type(words & jnp.uint32(0xFFFF0000), jnp.float32)
        up = lax.bitcast_convert_type(words << 16, jnp.float32)
        h = jnp.sum(down * x_ref[pl.ds(tt, 1), :], axis=1, keepdims=True)
        gate = jnp.sum(jnp.where(tok_lane == tt, gate_t, 0.0), axis=1, keepdims=True)
        w = 0.5 * h * (1.0 + lax.erf(h * (2.0 ** -0.5))) * gate
        o_ref[pl.ds(tt, 1), :] = res_ref[pl.ds(tt, 1), :] + jnp.sum(w * up, axis=0, keepdims=True)

    def step(group, b, prefetch):
        wait_group(b)
        for s in range(G):
            if prefetch:
                issue_token(group + PEER_BUFFERS - 1, s, (b + PEER_BUFFERS - 1) % PEER_BUFFERS)
            mix_token(group, s, b)

    for g in range(PEER_BUFFERS - 1):
        for s in range(G):
            issue_token(g, s, g)
    n_main = (PEER_GROUPS_PER_STEP - (PEER_BUFFERS - 1)) // PEER_BUFFERS

    def rotation(it, carry):
        for b in range(PEER_BUFFERS):
            step(it * PEER_BUFFERS + b, b, True)
        return carry

    lax.fori_loop(0, n_main, rotation, 0)
    for g in range(n_main * PEER_BUFFERS, PEER_GROUPS_PER_STEP):
        step(g, g % PEER_BUFFERS, g + PEER_BUFFERS - 1 < PEER_GROUPS_PER_STEP)


def peer_mix(idx, xn, gate_t, res, table):
    T, D = xn.shape
    tb = PEER_TOKENS_PER_STEP

    def tok(width):
        return pl.BlockSpec((tb, width), lambda i: (i, 0))

    return pl.pallas_call(
        _peer_mix_kernel,
        grid=(T // tb,),
        in_specs=[
            pl.BlockSpec((1, PEER_PICKS, tb), lambda i: (i, 0, 0), memory_space=pltpu.SMEM),
            tok(D),
            pl.BlockSpec((1, PEER_PICKS, tb), lambda i: (i, 0, 0)),
            tok(D),
            pl.BlockSpec(memory_space=pl.ANY),
        ],
        out_specs=tok(D),
        out_shape=jax.ShapeDtypeStruct((T, D), jnp.float32),
        scratch_shapes=[pltpu.VMEM((PEER_GROUP, PEER_PICKS, D), jnp.uint32) for _ in range(PEER_BUFFERS)]
        + [pltpu.SemaphoreType.DMA((PEER_BUFFERS,))],
        compiler_params=pltpu.CompilerParams(
            dimension_semantics=("arbitrary",), vmem_limit_bytes=VMEM_LIMIT_BYTES),
        name="peer_mix",
    )(idx, xn, gate_t, res, table)


def _top_rows(s, k):
    n = s.shape[0]
    rows = _iota2(s.shape, 0).astype(jnp.float32)
    vals, poss = [], []
    for _ in range(k):
        m = jnp.max(s, axis=0, keepdims=True)
        pos = jnp.min(jnp.where(s == m, rows, float(n)), axis=0, keepdims=True)
        vals.append(m)
        poss.append(pos)
        s = jnp.where(rows == pos, -jnp.inf, s)
    return jnp.concatenate(vals, axis=0), jnp.concatenate(poss, axis=0)


def _pick_rows(table, sel):
    out = jnp.zeros_like(sel)
    for r in range(table.shape[0]):
        out = out + jnp.where(sel == float(r), table[r:r + 1, :], 0.0)
    return out


def _peer_select_kernel(q_ref, keys_ref, idx_ref, gate_ref):
    K = PEER_TOPK
    half_w = D_KEY // 2
    idx_rows, gate_rows = [], []
    for h in range(PEER_HEADS):
        tops = []
        for c in range(2):
            hc = 2 * h + c
            scores = _mm_nt(keys_ref[hc], q_ref[:, hc * half_w:(hc + 1) * half_w])
            tops.append(_top_rows(scores, K))
        (s1, i1), (s2, i2) = tops
        cand = jnp.concatenate([s1[i:i + 1, :] + s2 for i in range(K)], axis=0)
        top_s, pos = _top_rows(cand, K)
        first = jnp.floor(pos * (1.0 / K))
        second = pos - first * K
        expert = _pick_rows(i1, first) * N_KEYS + _pick_rows(i2, second)
        e = jnp.exp(top_s - top_s[0:1, :])
        idx_rows.append(expert.astype(jnp.int32))
        gate_rows.append(e / jnp.sum(e, axis=0, keepdims=True))
    idx_ref[0] = jnp.concatenate(idx_rows, axis=0)
    gate_ref[0] = jnp.concatenate(gate_rows, axis=0)


def peer_select(q, keys):
    T, W = q.shape
    tb = PEER_TOKENS_PER_STEP
    out_block = pl.BlockSpec((1, PEER_PICKS, tb), lambda i: (i, 0, 0))
    return pl.pallas_call(
        _peer_select_kernel,
        grid=(T // tb,),
        in_specs=[pl.BlockSpec((tb, W), lambda i: (i, 0)), pl.BlockSpec(keys.shape, lambda i: (0, 0, 0))],
        out_specs=[out_block, out_block],
        out_shape=[jax.ShapeDtypeStruct((T // tb, PEER_PICKS, tb), jnp.int32),
                   jax.ShapeDtypeStruct((T // tb, PEER_PICKS, tb), jnp.float32)],
        compiler_params=pltpu.CompilerParams(
            dimension_semantics=("arbitrary",), vmem_limit_bytes=VMEM_LIMIT_BYTES),
        name="peer_select",
    )(q, keys)


def _rms(x, g):
    return x * lax.rsqrt(jnp.mean(x * x, axis=-1, keepdims=True) + RMS_EPS) * g


def _attention(q, k, v, q_gain, k_gain, sinks):
    B, LP, _ = q.shape
    NB = LP // BLOCK
    KVH = ATTN_KV_HEADS
    G = ATTN_Q_HEADS // KVH
    q = _rms(q.reshape(B, LP, ATTN_Q_HEADS, HEAD_DIM), q_gain).reshape(B, NB, BLOCK, KVH, G, HEAD_DIM)
    k = _rms(k.reshape(B, LP, KVH, HEAD_DIM), k_gain).reshape(B, NB, BLOCK, KVH, HEAD_DIM)
    v = v.reshape(B, NB, BLOCK, KVH, HEAD_DIM)

    def with_prev(t):
        prev = jnp.pad(t, ((0, 0), (1, 0), (0, 0), (0, 0), (0, 0)))[:, :-1]
        return jnp.concatenate([prev, t], axis=2)

    kw, vw = with_prev(k), with_prev(v)
    s = jnp.einsum('bnqkgd,bnskd->bnkgqs', q, kw, preferred_element_type=jnp.float32) * ATTN_SCALE
    blk = jnp.arange(NB)[:, None, None]
    qpos = blk * BLOCK + jnp.arange(BLOCK)[None, :, None]
    kpos = (blk - 1) * BLOCK + jnp.arange(2 * BLOCK)[None, None, :]
    dist = qpos - kpos
    mask = (dist >= 0) & (dist < WINDOW) & (kpos >= PAD_LEFT)
    s = jnp.where(mask[None, :, None, None], s, MASK_VALUE)
    sink = jnp.broadcast_to(sinks.reshape(1, 1, KVH, G, 1, 1), s.shape[:-1] + (1,))
    p = jax.nn.softmax(jnp.concatenate([s, sink], axis=-1), axis=-1)[..., :-1]
    o = jnp.einsum('bnkgqs,bnskd->bnqkgd', p, vw)
    return o.reshape(B, LP, ATTN_WIDTH)


def kernel(x, meta_tokens, norm1_g, w_in, shift_mu, w0, w_up, a0, a_up, g_up, k_k, k_a, r_k, gn_w, gn_b,
           q_gain, k_gain, sinks, w_out, norm2_g, peer_query, peer_sub_keys, peer_down, peer_up):
    B, S, D = x.shape
    LP = S + BLOCK
    T = B * LP
    meta = jnp.broadcast_to(meta_tokens[None], (B, N_META, D))
    h = jnp.concatenate([jnp.zeros((B, PAD_LEFT, D), x.dtype), meta, x], axis=1)
    valid = jnp.arange(LP) >= PAD_LEFT
    ht = h.reshape(T, D)

    w = w_in[0]
    n_lora = DECAY_LORA + AAA_LORA + GATE_LORA
    rwkv0 = ATTN_COLS
    lora0 = rwkv0 + 3 * RWKV_WIDTH
    w_in_b = jnp.concatenate([
        w[:, rwkv0:lora0], w[:, :ATTN_WIDTH],
        jnp.pad(w[:, lora0:lora0 + n_lora], ((0, 0), (0, LORA_PAD - n_lora))),
        w[:, ATTN_WIDTH:ATTN_COLS]], axis=1).astype(jnp.bfloat16)
    col_q = 3 * RWKV_WIDTH
    col_l = col_q + ATTN_WIDTH
    col_kv = col_l + LORA_PAD
    n_in = col_kv + 2 * KV_WIDTH
    p = norm_matmul(ht, norm1_g[0], w_in_b, 512, n_in // 2)

    y_rwkv = rwkv7_mix(p, B, LP, 0, RWKV_WIDTH, 2 * RWKV_WIDTH, col_l,
                       *rwkv_params(shift_mu[0], w0[0], w_up[0], a0[0], a_up[0], g_up[0], k_k[0], k_a[0], r_k[0],
                                    gn_w[0], gn_b[0]))
    p3 = p.reshape(B, LP, n_in)
    q = p3[..., col_q:col_q + ATTN_WIDTH]
    k = p3[..., col_kv:col_kv + KV_WIDTH]
    v = p3[..., col_kv + KV_WIDTH:]
    y_attn = _attention(q, k, v, q_gain[0], k_gain[0], sinks[0])
    mix = jnp.concatenate([y_rwkv, y_attn.reshape(T, ATTN_WIDTH).astype(jnp.bfloat16)], axis=-1)
    h2 = matmul_residual(mix, w_out[0].astype(jnp.bfloat16), ht, 512, 1024)

    pq = norm_matmul(h2, norm2_g[0], peer_query[0].astype(jnp.bfloat16), 512, 1024, jnp.bfloat16)
    xn = _rms(h2, norm2_g[0])
    keys = peer_sub_keys[0].reshape(2 * PEER_HEADS, N_KEYS, D_KEY // 2).astype(jnp.bfloat16)
    idx_t, gate_t = peer_select(pq, keys)
    table = pack_expert_table(peer_down[0], peer_up[0])
    out = peer_mix(idx_t, xn, gate_t, h2, table).reshape(B, LP, D)
    out = jnp.where(valid[None, :, None], out, jnp.zeros_like(out))
    return out[:, BLOCK:]
```

```python
import functools

import jax
import jax.numpy as jnp
from jax import lax
from jax.experimental import pallas as pl
from jax.experimental.pallas import tpu as pltpu

N_META = 16
BLOCK = 128
PAD_LEFT = BLOCK - N_META
HEAD_DIM = 64
RWKV_WIDTH = 1024
RWKV_HEADS = RWKV_WIDTH // HEAD_DIM
ATTN_WIDTH = 1024
ATTN_Q_HEADS = ATTN_WIDTH // HEAD_DIM
ATTN_KV_HEADS = 2
KV_WIDTH = ATTN_KV_HEADS * HEAD_DIM
WINDOW = 128
ATTN_SCALE = HEAD_DIM ** -0.5
MASK_VALUE = -1e30
DECAY_LORA = 64
AAA_LORA = 64
GATE_LORA = 160
RMS_EPS = 1e-6
GN_EPS = 64e-5
ATTN_COLS = ATTN_WIDTH + 2 * KV_WIDTH
PEER_HEADS = 8
N_KEYS = 128
PEER_TOPK = 16
D_KEY = 256
PEER_BLOCK = 128

VMEM_LIMIT_BYTES = 48 * 1024 * 1024


def _norm_matmul_kernel(x_ref, g_ref, w_ref, o_ref):
    x = x_ref[...]
    ms = jnp.mean(x * x, axis=-1, keepdims=True)
    u = x * lax.rsqrt(ms + RMS_EPS) * g_ref[...]
    o_ref[...] = jnp.dot(
        u.astype(jnp.bfloat16), w_ref[...], preferred_element_type=jnp.float32).astype(o_ref.dtype)


def norm_matmul(x, g, w, tm, tn, out_dtype=jnp.float32):
    m, k = x.shape
    n = w.shape[1]
    return pl.pallas_call(
        _norm_matmul_kernel,
        grid=(n // tn, m // tm),
        in_specs=[
            pl.BlockSpec((tm, k), lambda j, i: (i, 0)),
            pl.BlockSpec((1, k), lambda j, i: (0, 0)),
            pl.BlockSpec((k, tn), lambda j, i: (0, j)),
        ],
        out_specs=pl.BlockSpec((tm, tn), lambda j, i: (i, j)),
        out_shape=jax.ShapeDtypeStruct((m, n), out_dtype),
        compiler_params=pltpu.CompilerParams(
            dimension_semantics=("arbitrary", "arbitrary"), vmem_limit_bytes=VMEM_LIMIT_BYTES),
        name="norm_matmul",
    )(x, g.reshape(1, k), w)


def _matmul_residual_kernel(x_ref, w_ref, r_ref, o_ref):
    o_ref[...] = r_ref[...] + jnp.dot(
        x_ref[...].astype(jnp.bfloat16), w_ref[...], preferred_element_type=jnp.float32)


def matmul_residual(x, w, r, tm, tn):
    m, k = x.shape
    n = w.shape[1]
    return pl.pallas_call(
        _matmul_residual_kernel,
        grid=(n // tn, m // tm),
        in_specs=[
            pl.BlockSpec((tm, k), lambda j, i: (i, 0)),
            pl.BlockSpec((k, tn), lambda j, i: (0, j)),
            pl.BlockSpec((tm, tn), lambda j, i: (i, j)),
        ],
        out_specs=pl.BlockSpec((tm, tn), lambda j, i: (i, j)),
        out_shape=jax.ShapeDtypeStruct((m, n), jnp.float32),
        compiler_params=pltpu.CompilerParams(
            dimension_semantics=("arbitrary", "arbitrary"), vmem_limit_bytes=VMEM_LIMIT_BYTES),
        name="matmul_residual",
    )(x, w, r)


RWKV_CHUNK = 64
LANES = 128
RWKV_TILES = RWKV_WIDTH // LANES
LORA_PAD = 512


def _dot(a, b, dims):
    return lax.dot_general(a, b, (dims, ((), ())), preferred_element_type=jnp.float32)


def _mm(a, b):
    return _dot(a, b, ((1,), (0,)))


def _mm_nt(a, b):
    return _dot(a, b, ((1,), (1,)))


def _mm_tn(a, b):
    return _dot(a, b, ((0,), (0,)))


def _iota2(shape, axis):
    return lax.broadcasted_iota(jnp.int32, shape, axis)


def _sigmoid(x):
    return 1.0 / (1.0 + jnp.exp(-x))


def _bf16_pieces(x, n):
    pieces = []
    for _ in range(n):
        p = x.astype(jnp.bfloat16)
        pieces.append(p)
        x = x - p.astype(jnp.float32)
    return pieces


def _token_shift(x, prev_ref, mu):
    rolled = pltpu.roll(x, 1, 0)
    prev = jnp.where(_iota2(x.shape, 0) == 0, prev_ref[...], rolled)
    prev_ref[...] = x[x.shape[0] - 1:, :]
    return x + mu * (prev - x)


def _rwkv_kernel(xr_ref, xk_ref, xv_ref, xl_ref, mu_ref, mul_ref, vec_ref, wup_ref, aup_ref, gup_ref,
                 o_ref, s_ref, pr_ref, pk_ref, pv_ref, pl_ref):
    C = RWKV_CHUNK
    f32, bf16 = jnp.float32, jnp.bfloat16

    @pl.when(pl.program_id(1) == 0)
    def _():
        s_ref[...] = jnp.zeros_like(s_ref)
        pr_ref[...] = jnp.zeros_like(pr_ref)
        pk_ref[...] = jnp.zeros_like(pk_ref)
        pv_ref[...] = jnp.zeros_like(pv_ref)
        pl_ref[...] = jnp.zeros_like(pl_ref)

    r = _token_shift(xr_ref[...], pr_ref, mu_ref[0:1, :])
    kraw = _token_shift(xk_ref[...], pk_ref, mu_ref[1:2, :])
    v = _token_shift(xv_ref[...], pv_ref, mu_ref[2:3, :])
    xl = _token_shift(xl_ref[...], pl_ref, mul_ref[...])

    w0, a0, k_k, k_a = vec_ref[0:1, :], vec_ref[1:2, :], vec_ref[2:3, :], vec_ref[3:4, :]
    r_k, gn_w, gn_b = vec_ref[4:5, :], vec_ref[5:6, :], vec_ref[6:7, :]

    x_wa = xl[:, :LANES]
    wl = w0 + _mm(jnp.tanh(x_wa).astype(jnp.bfloat16), wup_ref[...])
    z = -wl
    softplus = jnp.maximum(z, 0.0) + jnp.log(1.0 + jnp.exp(-jnp.abs(z)))
    lw = -jnp.exp(-softplus - 0.5)
    a = _sigmoid(a0 + _mm(x_wa.astype(jnp.bfloat16), aup_ref[...]))
    g = _mm(_sigmoid(xl[:, LANES:LANES + 256]).astype(jnp.bfloat16), gup_ref[...])

    tri = (_iota2((C, C), 0) >= _iota2((C, C), 1)).astype(bf16)
    cl = sum(_mm(tri, piece) for piece in _bf16_pieces(lw, 3))
    e_pos = jnp.exp(cl)
    e_excl = jnp.exp(cl - lw)
    e_neg = 1.0 / e_pos
    e_end = e_pos[C - 1:, :]
    e_tail = e_end * e_neg

    lane = _iota2((LANES, LANES), 1)
    row = _iota2((LANES, LANES), 0)
    same_head = (lane >= HEAD_DIM) == (row >= HEAD_DIM)
    seg = same_head.astype(bf16)
    eye = (lane == row).astype(f32)
    t_row, t_col = row & (C - 1), lane & (C - 1)
    strict = t_col < t_row
    incl = t_col <= t_row
    incl2 = jnp.concatenate([incl, incl], axis=1)
    lane_c = _iota2((C, LANES), 1)
    head0 = lane_c < HEAD_DIM

    def seg_sum(x):
        return sum(_mm(piece, seg) for piece in _bf16_pieces(x, 2))

    def by_head(x):
        return jnp.concatenate([jnp.where(head0, x, 0.0), jnp.where(head0, 0.0, x)], axis=0).astype(bf16)

    def twice(x):
        return jnp.concatenate([x, x], axis=0)

    tiles = range(RWKV_TILES)
    sls = [slice(j * LANES, (j + 1) * LANES) for j in tiles]
    kr = [kraw[:, sl] for sl in sls]
    kkr = [kr[j] * k_k[:, sls[j]] for j in tiles]
    kk_ss = [seg_sum(kkr[j] * kkr[j]) for j in tiles]
    kk = [kkr[j] / jnp.maximum(jnp.sqrt(kk_ss[j]), 1e-12) for j in tiles]
    a_t = [a[:, sl] for sl in sls]
    k2 = [kr[j] * (1.0 + (a_t[j] - 1.0) * k_a[:, sls[j]]) for j in tiles]
    r_t = [r[:, sl] for sl in sls]
    v_t = [v[:, sl] for sl in sls]
    kka = [kk[j] * a_t[j] for j in tiles]
    alpha = [-kk[j] * e_excl[:, sls[j]] for j in tiles]
    r_dec = [r_t[j] * e_pos[:, sls[j]] for j in tiles]
    beta = [kka[j] * e_neg[:, sls[j]] for j in tiles]
    k_neg = [k2[j] * e_neg[:, sls[j]] for j in tiles]
    s_old = [s_ref[j] for j in tiles]

    gram = [_mm_nt(jnp.concatenate([by_head(alpha[j]), by_head(r_dec[j])], axis=0),
                   jnp.concatenate([by_head(beta[j]), by_head(k_neg[j])], axis=0)) for j in tiles]
    m1 = [_mm_nt(jnp.concatenate([alpha[j], r_dec[j]], axis=0).astype(bf16), s_old[j].astype(bf16))
          for j in tiles]
    a_s = [jnp.where(strict, gram[j][:LANES, :LANES], 0.0) for j in tiles]
    b_s = [jnp.where(strict, gram[j][:LANES, LANES:], 0.0).astype(bf16) for j in tiles]
    r_i = [jnp.where(incl2, gram[j][LANES:, :], 0.0).astype(bf16) for j in tiles]
    vv = [twice(v_t[j]).astype(bf16) for j in tiles]
    rhs = [twice(m1[j][:C]) + _mm(b_s[j], vv[j]) for j in tiles]
    t_inv = [eye + a_s[j] for j in tiles]
    pw = [a_s[j].astype(bf16) for j in tiles]
    for _ in range(5):
        pw = [_mm(pw[j], pw[j]).astype(bf16) for j in tiles]
        t_inv = [t_inv[j] + _mm(t_inv[j].astype(bf16), pw[j]) for j in tiles]
    u_st = [_mm(t_inv[j].astype(bf16), rhs[j].astype(bf16)) for j in tiles]
    y_st = [twice(m1[j][C:]) + _mm(r_i[j], jnp.concatenate([u_st[j].astype(bf16), vv[j]], axis=0))
            for j in tiles]
    u_t = [jnp.where(head0, u_st[j][:C], u_st[j][C:]) for j in tiles]
    y_t = [jnp.where(head0, y_st[j][:C], y_st[j][C:]) for j in tiles]

    upd = [_mm_tn(jnp.concatenate([u_t[j], v_t[j]], axis=0).astype(bf16),
                  jnp.concatenate([kka[j] * e_tail[:, sls[j]], k2[j] * e_tail[:, sls[j]]], axis=0).astype(bf16))
           for j in tiles]
    for j in tiles:
        s_ref[j] = s_old[j] * e_end[:, sls[j]] + jnp.where(same_head, upd[j], 0.0)

    mu_y = [seg_sum(y_t[j]) * (1.0 / HEAD_DIM) for j in tiles]
    dev = [y_t[j] - mu_y[j] for j in tiles]
    var = [seg_sum(dev[j] * dev[j]) * (1.0 / HEAD_DIM) for j in tiles]
    bonus = [seg_sum(r_t[j] * k2[j] * r_k[:, sls[j]]) * v_t[j] for j in tiles]
    outs = [(dev[j] * lax.rsqrt(var[j] + GN_EPS) * gn_w[:, sls[j]] + gn_b[:, sls[j]] + bonus[j]) * g[:, sls[j]]
            for j in tiles]
    o_ref[...] = jnp.concatenate(outs, axis=1).astype(o_ref.dtype)


def rwkv_params(shift_mu, w0, w_up, a0, a_up, g_up, k_k, k_a, r_k, gn_w, gn_b):
    W = RWKV_WIDTH
    n_lora = DECAY_LORA + AAA_LORA + GATE_LORA
    mu_rkv = shift_mu[:3 * W].reshape(3, W)
    mu_l = jnp.pad(shift_mu[3 * W:], (0, LORA_PAD - n_lora)).reshape(1, LORA_PAD)
    vecs = jnp.stack([w0, a0, k_k, k_a, r_k.reshape(W), gn_w, gn_b, jnp.zeros_like(w0)])
    w_up_p = jnp.pad(w_up, ((0, LANES - DECAY_LORA), (0, 0))).astype(jnp.bfloat16)
    a_up_p = jnp.pad(a_up, ((DECAY_LORA, LANES - DECAY_LORA - AAA_LORA), (0, 0))).astype(jnp.bfloat16)
    g_up_p = jnp.pad(g_up, ((0, 256 - GATE_LORA), (0, 0))).astype(jnp.bfloat16)
    return mu_rkv, mu_l, vecs, w_up_p, a_up_p, g_up_p


def rwkv7_mix(p, B, LP, col_r, col_k, col_v, col_l, mu_rkv, mu_l, vecs, w_up, a_up, g_up):
    C = RWKV_CHUNK
    nc = LP // C
    W = RWKV_WIDTH

    def col_spec(width, col):
        return pl.BlockSpec((C, width), lambda b, c: (b * nc + c, col // width))

    def full(shape):
        return pl.BlockSpec(shape, lambda b, c: (0,) * len(shape))

    return pl.pallas_call(
        _rwkv_kernel,
        grid=(B, nc),
        in_specs=[
            col_spec(W, col_r), col_spec(W, col_k), col_spec(W, col_v), col_spec(LORA_PAD, col_l),
            full(mu_rkv.shape), full(mu_l.shape), full(vecs.shape),
            full(w_up.shape), full(a_up.shape), full(g_up.shape),
        ],
        out_specs=pl.BlockSpec((C, W), lambda b, c: (b * nc + c, 0)),
        out_shape=jax.ShapeDtypeStruct((B * LP, W), jnp.bfloat16),
        scratch_shapes=[
            pltpu.VMEM((RWKV_TILES, LANES, LANES), jnp.float32),
            pltpu.VMEM((1, W), jnp.float32), pltpu.VMEM((1, W), jnp.float32), pltpu.VMEM((1, W), jnp.float32),
            pltpu.VMEM((1, LORA_PAD), jnp.float32),
        ],
        compiler_params=pltpu.CompilerParams(
            dimension_semantics=("arbitrary", "arbitrary"), vmem_limit_bytes=VMEM_LIMIT_BYTES),
        name="rwkv7_mix",
    )(p, p, p, p, mu_rkv, mu_l, vecs, w_up, a_up, g_up)


PEER_PICKS = PEER_HEADS * PEER_TOPK
PEER_TOKENS_PER_STEP = 128
PEER_GROUP = 4
PEER_GROUPS_PER_STEP = PEER_TOKENS_PER_STEP // PEER_GROUP
PEER_BUFFERS = 3


def pack_expert_table(down, up):
    def bits(t):
        return lax.bitcast_convert_type(t.astype(jnp.bfloat16), jnp.uint16).astype(jnp.uint32)
    return ((bits(down) << 16) | bits(up))[:, None, :]


def _peer_mix_kernel(idx_ref, x_ref, gate_t_ref, res_ref, tab_ref, o_ref, *scratch):
    bufs, sem = scratch[:PEER_BUFFERS], scratch[PEER_BUFFERS]
    G = PEER_GROUP
    gate_t = gate_t_ref[0]
    tok_lane = _iota2(gate_t.shape, 1)

    def issue_token(group, s, b):
        for j in range(PEER_PICKS):
            e = idx_ref[0, j, group * G + s]
            pltpu.make_async_copy(tab_ref.at[e], bufs[b].at[s, pl.ds(j, 1), :], sem.at[b]).start(priority=j % 2)

    def wait_group(b):
        pltpu.make_async_copy(bufs[b], bufs[b], sem.at[b]).wait()

    def mix_token(group, s, b):
        tt = group * G + s
        words = bufs[b][s]
        down = lax.bitcast_convert_type(words & jnp.uint32(0xFFFF0000), jnp.float32)
        up = lax.bitcast_convert_type(words << 16, jnp.float32)
        h = jnp.sum(down * x_ref[pl.ds(tt, 1), :], axis=1, keepdims=True)
        gate = jnp.sum(jnp.where(tok_lane == tt, gate_t, 0.0), axis=1, keepdims=True)
        w = 0.5 * h * (1.0 + lax.erf(h * (2.0 ** -0.5))) * gate
        o_ref[pl.ds(tt, 1), :] = res_ref[pl.ds(tt, 1), :] + jnp.sum(w * up, axis=0, keepdims=True)

    def step(group, b, prefetch):
        wait_group(b)
        for s in range(G):
            if prefetch:
                issue_token(group + PEER_BUFFERS - 1, s, (b + PEER_BUFFERS - 1) % PEER_BUFFERS)
            mix_token(group, s, b)

    for g in range(PEER_BUFFERS - 1):
        for s in range(G):
            issue_token(g, s, g)
    n_main = (PEER_GROUPS_PER_STEP - (PEER_BUFFERS - 1)) // PEER_BUFFERS

    def rotation(it, carry):
        for b in range(PEER_BUFFERS):
            step(it * PEER_BUFFERS + b, b, True)
        return carry

    lax.fori_loop(0, n_main, rotation, 0)
    for g in range(n_main * PEER_BUFFERS, PEER_GROUPS_PER_STEP):
        step(g, g % PEER_BUFFERS, g + PEER_BUFFERS - 1 < PEER_GROUPS_PER_STEP)


def peer_mix(idx, xn, gate_t, res, table):
    T, D = xn.shape
    tb = PEER_TOKENS_PER_STEP

    def tok(width):
        return pl.BlockSpec((tb, width), lambda i: (i, 0))

    return pl.pallas_call(
        _peer_mix_kernel,
        grid=(T // tb,),
        in_specs=[
            pl.BlockSpec((1, PEER_PICKS, tb), lambda i: (i, 0, 0), memory_space=pltpu.SMEM),
            tok(D),
            pl.BlockSpec((1, PEER_PICKS, tb), lambda i: (i, 0, 0)),
            tok(D),
            pl.BlockSpec(memory_space=pl.ANY),
        ],
        out_specs=tok(D),
        out_shape=jax.ShapeDtypeStruct((T, D), jnp.float32),
        scratch_shapes=[pltpu.VMEM((PEER_GROUP, PEER_PICKS, D), jnp.uint32) for _ in range(PEER_BUFFERS)]
        + [pltpu.SemaphoreType.DMA((PEER_BUFFERS,))],
        compiler_params=pltpu.CompilerParams(
            dimension_semantics=("arbitrary",), vmem_limit_bytes=VMEM_LIMIT_BYTES),
        name="peer_mix",
    )(idx, xn, gate_t, res, table)


def _top_rows(s, k):
    n = s.shape[0]
    rows = _iota2(s.shape, 0).astype(jnp.float32)
    vals, poss = [], []
    for _ in range(k):
        m = jnp.max(s, axis=0, keepdims=True)
        pos = jnp.min(jnp.where(s == m, rows, float(n)), axis=0, keepdims=True)
        vals.append(m)
        poss.append(pos)
        s = jnp.where(rows == pos, -jnp.inf, s)
    return jnp.concatenate(vals, axis=0), jnp.concatenate(poss, axis=0)


def _pick_rows(table, sel):
    out = jnp.zeros_like(sel)
    for r in range(table.shape[0]):
        out = out + jnp.where(sel == float(r), table[r:r + 1, :], 0.0)
    return out


def _peer_select_kernel(q_ref, keys_ref, idx_ref, gate_ref):
    K = PEER_TOPK
    half_w = D_KEY // 2
    idx_rows, gate_rows = [], []
    for h in range(PEER_HEADS):
        tops = []
        for c in range(2):
            hc = 2 * h + c
            scores = _mm_nt(keys_ref[hc], q_ref[:, hc * half_w:(hc + 1) * half_w])
            tops.append(_top_rows(scores, K))
        (s1, i1), (s2, i2) = tops
        cand = jnp.concatenate([s1[i:i + 1, :] + s2 for i in range(K)], axis=0)
        top_s, pos = _top_rows(cand, K)
        first = jnp.floor(pos * (1.0 / K))
        second = pos - first * K
        expert = _pick_rows(i1, first) * N_KEYS + _pick_rows(i2, second)
        e = jnp.exp(top_s - top_s[0:1, :])
        idx_rows.append(expert.astype(jnp.int32))
        gate_rows.append(e / jnp.sum(e, axis=0, keepdims=True))
    idx_ref[0] = jnp.concatenate(idx_rows, axis=0)
    gate_ref[0] = jnp.concatenate(gate_rows, axis=0)


def peer_select(q, keys):
    T, W = q.shape
    tb = PEER_TOKENS_PER_STEP
    out_block = pl.BlockSpec((1, PEER_PICKS, tb), lambda i: (i, 0, 0))
    return pl.pallas_call(
        _peer_select_kernel,
        grid=(T // tb,),
        in_specs=[pl.BlockSpec((tb, W), lambda i: (i, 0)), pl.BlockSpec(keys.shape, lambda i: (0, 0, 0))],
        out_specs=[out_block, out_block],
        out_shape=[jax.ShapeDtypeStruct((T // tb, PEER_PICKS, tb), jnp.int32),
                   jax.ShapeDtypeStruct((T // tb, PEER_PICKS, tb), jnp.float32)],
        compiler_params=pltpu.CompilerParams(
            dimension_semantics=("arbitrary",), vmem_limit_bytes=VMEM_LIMIT_BYTES),
        name="peer_select",
    )(q, keys)


def _rms(x, g):
    return x * lax.rsqrt(jnp.mean(x * x, axis=-1, keepdims=True) + RMS_EPS) * g


def _attention(q, k, v, q_gain, k_gain, sinks):
    B, LP, _ = q.shape
    NB = LP // BLOCK
    KVH = ATTN_KV_HEADS
    G = ATTN_Q_HEADS // KVH
    q = _rms(q.reshape(B, LP, ATTN_Q_HEADS, HEAD_DIM), q_gain).reshape(B, NB, BLOCK, KVH, G, HEAD_DIM)
    k = _rms(k.reshape(B, LP, KVH, HEAD_DIM), k_gain).reshape(B, NB, BLOCK, KVH, HEAD_DIM)
    v = v.reshape(B, NB, BLOCK, KVH, HEAD_DIM)

    def with_prev(t):
        prev = jnp.pad(t, ((0, 0), (1, 0), (0, 0), (0, 0), (0, 0)))[:, :-1]
        return jnp.concatenate([prev, t], axis=2)

    kw, vw = with_prev(k), with_prev(v)
    s = jnp.einsum('bnqkgd,bnskd->bnkgqs', q, kw, preferred_element_type=jnp.float32) * ATTN_SCALE
    blk = jnp.arange(NB)[:, None, None]
    qpos = blk * BLOCK + jnp.arange(BLOCK)[None, :, None]
    kpos = (blk - 1) * BLOCK + jnp.arange(2 * BLOCK)[None, None, :]
    dist = qpos - kpos
    mask = (dist >= 0) & (dist < WINDOW) & (kpos >= PAD_LEFT)
    s = jnp.where(mask[None, :, None, None], s, MASK_VALUE)
    sink = jnp.broadcast_to(sinks.reshape(1, 1, KVH, G, 1, 1), s.shape[:-1] + (1,))
    p = jax.nn.softmax(jnp.concatenate([s, sink], axis=-1), axis=-1)[..., :-1]
    o = jnp.einsum('bnkgqs,bnskd->bnqkgd', p, vw)
    return o.reshape(B, LP, ATTN_WIDTH)


def kernel(x, meta_tokens, norm1_g, w_in, shift_mu, w0, w_up, a0, a_up, g_up, k_k, k_a, r_k, gn_w, gn_b,
           q_gain, k_gain, sinks, w_out, norm2_g, peer_query, peer_sub_keys, peer_down, peer_up):
    B, S, D = x.shape
    LP = S + BLOCK
    T = B * LP
    meta = jnp.broadcast_to(meta_tokens[None], (B, N_META, D))
    h = jnp.concatenate([jnp.zeros((B, PAD_LEFT, D), x.dtype), meta, x], axis=1)
    valid = jnp.arange(LP) >= PAD_LEFT
    ht = h.reshape(T, D)

    w = w_in[0]
    n_lora = DECAY_LORA + AAA_LORA + GATE_LORA
    rwkv0 = ATTN_COLS
    lora0 = rwkv0 + 3 * RWKV_WIDTH
    w_in_b = jnp.concatenate([
        w[:, rwkv0:lora0], w[:, :ATTN_WIDTH],
        jnp.pad(w[:, lora0:lora0 + n_lora], ((0, 0), (0, LORA_PAD - n_lora))),
        w[:, ATTN_WIDTH:ATTN_COLS]], axis=1).astype(jnp.bfloat16)
    col_q = 3 * RWKV_WIDTH
    col_l = col_q + ATTN_WIDTH
    col_kv = col_l + LORA_PAD
    n_in = col_kv + 2 * KV_WIDTH
    p = norm_matmul(ht, norm1_g[0], w_in_b, 512, n_in // 2)

    y_rwkv = rwkv7_mix(p, B, LP, 0, RWKV_WIDTH, 2 * RWKV_WIDTH, col_l,
                       *rwkv_params(shift_mu[0], w0[0], w_up[0], a0[0], a_up[0], g_up[0], k_k[0], k_a[0], r_k[0],
                                    gn_w[0], gn_b[0]))
    p3 = p.reshape(B, LP, n_in)
    q = p3[..., col_q:col_q + ATTN_WIDTH]
    k = p3[..., col_kv:col_kv + KV_WIDTH]
    v = p3[..., col_kv + KV_WIDTH:]
    y_attn = _attention(q, k, v, q_gain[0], k_gain[0], sinks[0])
    mix = jnp.concatenate([y_rwkv, y_attn.reshape(T, ATTN_WIDTH).astype(jnp.bfloat16)], axis=-1)
    h2 = matmul_residual(mix, w_out[0].astype(jnp.bfloat16), ht, 512, 1024)

    pq = norm_matmul(h2, norm2_g[0], peer_query[0].astype(jnp.bfloat16), 512, 1024, jnp.bfloat16)
    xn = _rms(h2, norm2_g[0])
    keys = peer_sub_keys[0].reshape(2 * PEER_HEADS, N_KEYS, D_KEY // 2).astype(jnp.bfloat16)
    idx_t, gate_t = peer_select(pq, keys)
    table = pack_expert_table(peer_down[0], peer_up[0])
    out = peer_mix(idx_t, xn, gate_t, h2, table).reshape(B, LP, D)
    out = jnp.where(valid[None, :, None], out, jnp.zeros_like(out))
    return out[:, BLOCK:]
```

```python
import functools

import jax
import jax.numpy as jnp
from jax import lax
from jax.experimental import pallas as pl
from jax.experimental.pallas import tpu as pltpu

N_META = 16
BLOCK = 128
PAD_LEFT = BLOCK - N_META
HEAD_DIM = 64
RWKV_WIDTH = 1024
RWKV_HEADS = RWKV_WIDTH // HEAD_DIM
ATTN_WIDTH = 1024
ATTN_Q_HEADS = ATTN_WIDTH // HEAD_DIM
ATTN_KV_HEADS = 2
KV_WIDTH = ATTN_KV_HEADS * HEAD_DIM
WINDOW = 128
ATTN_SCALE = HEAD_DIM ** -0.5
MASK_VALUE = -1e30
DECAY_LORA = 64
AAA_LORA = 64
GATE_LORA = 160
RMS_EPS = 1e-6
GN_EPS = 64e-5
ATTN_COLS = ATTN_WIDTH + 2 * KV_WIDTH
PEER_HEADS = 8
N_KEYS = 128
PEER_TOPK = 16
D_KEY = 256
PEER_BLOCK = 128

VMEM_LIMIT_BYTES = 48 * 1024 * 1024


def _norm_matmul_kernel(x_ref, g_ref, w_ref, o_ref):
    x = x_ref[...]
    ms = jnp.mean(x * x, axis=-1, keepdims=True)
    u = x * lax.rsqrt(ms + RMS_EPS) * g_ref[...]
    o_ref[...] = jnp.dot(
        u.astype(jnp.bfloat16), w_ref[...], preferred_element_type=jnp.float32).astype(o_ref.dtype)


def norm_matmul(x, g, w, tm, tn, out_dtype=jnp.float32):
    m, k = x.shape
    n = w.shape[1]
    return pl.pallas_call(
        _norm_matmul_kernel,
        grid=(n // tn, m // tm),
        in_specs=[
            pl.BlockSpec((tm, k), lambda j, i: (i, 0)),
            pl.BlockSpec((1, k), lambda j, i: (0, 0)),
            pl.BlockSpec((k, tn), lambda j, i: (0, j)),
        ],
        out_specs=pl.BlockSpec((tm, tn), lambda j, i: (i, j)),
        out_shape=jax.ShapeDtypeStruct((m, n), out_dtype),
        compiler_params=pltpu.CompilerParams(
            dimension_semantics=("arbitrary", "arbitrary"), vmem_limit_bytes=VMEM_LIMIT_BYTES),
        name="norm_matmul",
    )(x, g.reshape(1, k), w)


def _matmul_residual_kernel(xa_ref, xb_ref, w_ref, r_ref, o_ref):
    ka = xa_ref.shape[1]
    o_ref[...] = (r_ref[...]
                  + jnp.dot(xa_ref[...], w_ref[:ka, :], preferred_element_type=jnp.float32)
                  + jnp.dot(xb_ref[...], w_ref[ka:, :], preferred_element_type=jnp.float32))


def matmul_residual(xa, xb, w, r, tm, tn):
    m, ka = xa.shape
    kb = xb.shape[1]
    n = w.shape[1]
    return pl.pallas_call(
        _matmul_residual_kernel,
        grid=(n // tn, m // tm),
        in_specs=[
            pl.BlockSpec((tm, ka), lambda j, i: (i, 0)),
            pl.BlockSpec((tm, kb), lambda j, i: (i, 0)),
            pl.BlockSpec((ka + kb, tn), lambda j, i: (0, j)),
            pl.BlockSpec((tm, tn), lambda j, i: (i, j)),
        ],
        out_specs=pl.BlockSpec((tm, tn), lambda j, i: (i, j)),
        out_shape=jax.ShapeDtypeStruct((m, n), jnp.float32),
        compiler_params=pltpu.CompilerParams(
            dimension_semantics=("arbitrary", "arbitrary"), vmem_limit_bytes=VMEM_LIMIT_BYTES),
        name="matmul_residual",
    )(xa, xb, w, r)


RWKV_CHUNK = 64
LANES = 128
RWKV_TILES = RWKV_WIDTH // LANES
LORA_PAD = 512


def _dot(a, b, dims):
    return lax.dot_general(a, b, (dims, ((), ())), preferred_element_type=jnp.float32)


def _mm(a, b):
    return _dot(a, b, ((1,), (0,)))


def _mm_nt(a, b):
    return _dot(a, b, ((1,), (1,)))


def _mm_tn(a, b):
    return _dot(a, b, ((0,), (0,)))


def _iota2(shape, axis):
    return lax.broadcasted_iota(jnp.int32, shape, axis)


def _sigmoid(x):
    return 1.0 / (1.0 + jnp.exp(-x))


def _bf16_pieces(x, n):
    pieces = []
    for _ in range(n):
        p = x.astype(jnp.bfloat16)
        pieces.append(p)
        x = x - p.astype(jnp.float32)
    return pieces


def _token_shift(x, prev_ref, mu):
    rolled = pltpu.roll(x, 1, 0)
    prev = jnp.where(_iota2(x.shape, 0) == 0, prev_ref[...], rolled)
    prev_ref[...] = x[x.shape[0] - 1:, :]
    return x + mu * (prev - x)


def _rwkv_kernel(xr_ref, xk_ref, xv_ref, xl_ref, mu_ref, mul_ref, vec_ref, wup_ref, aup_ref, gup_ref,
                 o_ref, s_ref, pr_ref, pk_ref, pv_ref, pl_ref):
    C = RWKV_CHUNK
    f32, bf16 = jnp.float32, jnp.bfloat16

    @pl.when(pl.program_id(1) == 0)
    def _():
        s_ref[...] = jnp.zeros_like(s_ref)
        pr_ref[...] = jnp.zeros_like(pr_ref)
        pk_ref[...] = jnp.zeros_like(pk_ref)
        pv_ref[...] = jnp.zeros_like(pv_ref)
        pl_ref[...] = jnp.zeros_like(pl_ref)

    r = _token_shift(xr_ref[...], pr_ref, mu_ref[0:1, :])
    kraw = _token_shift(xk_ref[...], pk_ref, mu_ref[1:2, :])
    v = _token_shift(xv_ref[...], pv_ref, mu_ref[2:3, :])
    xl = _token_shift(xl_ref[...], pl_ref, mul_ref[...])

    w0, a0, k_k, k_a = vec_ref[0:1, :], vec_ref[1:2, :], vec_ref[2:3, :], vec_ref[3:4, :]
    r_k, gn_w, gn_b = vec_ref[4:5, :], vec_ref[5:6, :], vec_ref[6:7, :]

    x_wa = xl[:, :LANES]
    wl = w0 + _mm(jnp.tanh(x_wa).astype(jnp.bfloat16), wup_ref[...])
    z = -wl
    softplus = jnp.maximum(z, 0.0) + jnp.log(1.0 + jnp.exp(-jnp.abs(z)))
    lw = -jnp.exp(-softplus - 0.5)
    a = _sigmoid(a0 + _mm(x_wa.astype(jnp.bfloat16), aup_ref[...]))
    g = _mm(_sigmoid(xl[:, LANES:LANES + 256]).astype(jnp.bfloat16), gup_ref[...])

    tri = (_iota2((C, C), 0) >= _iota2((C, C), 1)).astype(bf16)
    cl = sum(_mm(tri, piece) for piece in _bf16_pieces(lw, 3))
    e_pos = jnp.exp(cl)
    e_excl = jnp.exp(cl - lw)
    e_neg = 1.0 / e_pos
    e_end = e_pos[C - 1:, :]
    e_tail = e_end * e_neg

    lane = _iota2((LANES, LANES), 1)
    row = _iota2((LANES, LANES), 0)
    same_head = (lane >= HEAD_DIM) == (row >= HEAD_DIM)
    seg = same_head.astype(bf16)
    eye = (lane == row).astype(f32)
    t_row, t_col = row & (C - 1), lane & (C - 1)
    strict = t_col < t_row
    incl = t_col <= t_row
    incl2 = jnp.concatenate([incl, incl], axis=1)
    lane_c = _iota2((C, LANES), 1)
    head0 = lane_c < HEAD_DIM

    def seg_sum(x):
        return sum(_mm(piece, seg) for piece in _bf16_pieces(x, 2))

    def by_head(x):
        return jnp.concatenate([jnp.where(head0, x, 0.0), jnp.where(head0, 0.0, x)], axis=0).astype(bf16)

    def twice(x):
        return jnp.concatenate([x, x], axis=0)

    tiles = range(RWKV_TILES)
    sls = [slice(j * LANES, (j + 1) * LANES) for j in tiles]
    kr = [kraw[:, sl] for sl in sls]
    kkr = [kr[j] * k_k[:, sls[j]] for j in tiles]
    kk_ss = [seg_sum(kkr[j] * kkr[j]) for j in tiles]
    kk = [kkr[j] / jnp.maximum(jnp.sqrt(kk_ss[j]), 1e-12) for j in tiles]
    a_t = [a[:, sl] for sl in sls]
    k2 = [kr[j] * (1.0 + (a_t[j] - 1.0) * k_a[:, sls[j]]) for j in tiles]
    r_t = [r[:, sl] for sl in sls]
    v_t = [v[:, sl] for sl in sls]
    kka = [kk[j] * a_t[j] for j in tiles]
    alpha = [-kk[j] * e_excl[:, sls[j]] for j in tiles]
    r_dec = [r_t[j] * e_pos[:, sls[j]] for j in tiles]
    beta = [kka[j] * e_neg[:, sls[j]] for j in tiles]
    k_neg = [k2[j] * e_neg[:, sls[j]] for j in tiles]
    s_old = [s_ref[j] for j in tiles]

    gram = [_mm_nt(jnp.concatenate([by_head(alpha[j]), by_head(r_dec[j])], axis=0),
                   jnp.concatenate([by_head(beta[j]), by_head(k_neg[j])], axis=0)) for j in tiles]
    m1 = [_mm_nt(jnp.concatenate([alpha[j], r_dec[j]], axis=0).astype(bf16), s_old[j].astype(bf16))
          for j in tiles]
    a_s = [jnp.where(strict, gram[j][:LANES, :LANES], 0.0) for j in tiles]
    b_s = [jnp.where(strict, gram[j][:LANES, LANES:], 0.0).astype(bf16) for j in tiles]
    r_i = [jnp.where(incl2, gram[j][LANES:, :], 0.0).astype(bf16) for j in tiles]
    vv = [twice(v_t[j]).astype(bf16) for j in tiles]
    rhs = [twice(m1[j][:C]) + _mm(b_s[j], vv[j]) for j in tiles]
    t_inv = [eye + a_s[j] for j in tiles]
    pw = [a_s[j].astype(bf16) for j in tiles]
    for _ in range(5):
        pw = [_mm(pw[j], pw[j]).astype(bf16) for j in tiles]
        t_inv = [t_inv[j] + _mm(t_inv[j].astype(bf16), pw[j]) for j in tiles]
    u_st = [_mm(t_inv[j].astype(bf16), rhs[j].astype(bf16)) for j in tiles]
    y_st = [twice(m1[j][C:]) + _mm(r_i[j], jnp.concatenate([u_st[j].astype(bf16), vv[j]], axis=0))
            for j in tiles]
    u_t = [jnp.where(head0, u_st[j][:C], u_st[j][C:]) for j in tiles]
    y_t = [jnp.where(head0, y_st[j][:C], y_st[j][C:]) for j in tiles]

    upd = [_mm_tn(jnp.concatenate([u_t[j], v_t[j]], axis=0).astype(bf16),
                  jnp.concatenate([kka[j] * e_tail[:, sls[j]], k2[j] * e_tail[:, sls[j]]], axis=0).astype(bf16))
           for j in tiles]
    for j in tiles:
        s_ref[j] = s_old[j] * e_end[:, sls[j]] + jnp.where(same_head, upd[j], 0.0)

    mu_y = [seg_sum(y_t[j]) * (1.0 / HEAD_DIM) for j in tiles]
    dev = [y_t[j] - mu_y[j] for j in tiles]
    var = [seg_sum(dev[j] * dev[j]) * (1.0 / HEAD_DIM) for j in tiles]
    bonus = [seg_sum(r_t[j] * k2[j] * r_k[:, sls[j]]) * v_t[j] for j in tiles]
    outs = [(dev[j] * lax.rsqrt(var[j] + GN_EPS) * gn_w[:, sls[j]] + gn_b[:, sls[j]] + bonus[j]) * g[:, sls[j]]
            for j in tiles]
    o_ref[...] = jnp.concatenate(outs, axis=1).astype(o_ref.dtype)


def rwkv_params(shift_mu, w0, w_up, a0, a_up, g_up, k_k, k_a, r_k, gn_w, gn_b):
    W = RWKV_WIDTH
    n_lora = DECAY_LORA + AAA_LORA + GATE_LORA
    mu_rkv = shift_mu[:3 * W].reshape(3, W)
    mu_l = jnp.pad(shift_mu[3 * W:], (0, LORA_PAD - n_lora)).reshape(1, LORA_PAD)
    vecs = jnp.stack([w0, a0, k_k, k_a, r_k.reshape(W), gn_w, gn_b, jnp.zeros_like(w0)])
    w_up_p = jnp.pad(w_up, ((0, LANES - DECAY_LORA), (0, 0))).astype(jnp.bfloat16)
    a_up_p = jnp.pad(a_up, ((DECAY_LORA, LANES - DECAY_LORA - AAA_LORA), (0, 0))).astype(jnp.bfloat16)
    g_up_p = jnp.pad(g_up, ((0, 256 - GATE_LORA), (0, 0))).astype(jnp.bfloat16)
    return mu_rkv, mu_l, vecs, w_up_p, a_up_p, g_up_p


def rwkv7_mix(p, B, LP, col_r, col_k, col_v, col_l, mu_rkv, mu_l, vecs, w_up, a_up, g_up):
    C = RWKV_CHUNK
    nc = LP // C
    W = RWKV_WIDTH

    def col_spec(width, col):
        return pl.BlockSpec((C, width), lambda b, c: (b * nc + c, col // width))

    def full(shape):
        return pl.BlockSpec(shape, lambda b, c: (0,) * len(shape))

    return pl.pallas_call(
        _rwkv_kernel,
        grid=(B, nc),
        in_specs=[
            col_spec(W, col_r), col_spec(W, col_k), col_spec(W, col_v), col_spec(LORA_PAD, col_l),
            full(mu_rkv.shape), full(mu_l.shape), full(vecs.shape),
            full(w_up.shape), full(a_up.shape), full(g_up.shape),
        ],
        out_specs=pl.BlockSpec((C, W), lambda b, c: (b * nc + c, 0)),
        out_shape=jax.ShapeDtypeStruct((B * LP, W), jnp.bfloat16),
        scratch_shapes=[
            pltpu.VMEM((RWKV_TILES, LANES, LANES), jnp.float32),
            pltpu.VMEM((1, W), jnp.float32), pltpu.VMEM((1, W), jnp.float32), pltpu.VMEM((1, W), jnp.float32),
            pltpu.VMEM((1, LORA_PAD), jnp.float32),
        ],
        compiler_params=pltpu.CompilerParams(
            dimension_semantics=("arbitrary", "arbitrary"), vmem_limit_bytes=VMEM_LIMIT_BYTES),
        name="rwkv7_mix",
    )(p, p, p, p, mu_rkv, mu_l, vecs, w_up, a_up, g_up)


PEER_PICKS = PEER_HEADS * PEER_TOPK
PEER_TOKENS_PER_STEP = 128
PEER_GROUP = 4
PEER_GROUPS_PER_STEP = PEER_TOKENS_PER_STEP // PEER_GROUP
PEER_BUFFERS = 3


def pack_expert_table(down, up):
    def bits(t):
        return lax.bitcast_convert_type(t.astype(jnp.bfloat16), jnp.uint16).astype(jnp.uint32)
    return ((bits(down) << 16) | bits(up))[:, None, :]


def _peer_mix_kernel(idx_ref, gate_t_ref, res_ref, g_ref, pos_ref, tab_ref, o_ref, *scratch):
    bufs, sem, x_ref = scratch[:PEER_BUFFERS], scratch[PEER_BUFFERS], scratch[PEER_BUFFERS + 1]
    G = PEER_GROUP
    gate_t = gate_t_ref[0]
    tok_lane = _iota2(gate_t.shape, 1)

    def issue_token(group, s, b):
        for j in range(PEER_PICKS):
            e = idx_ref[0, j, group * G + s]
            pltpu.make_async_copy(tab_ref.at[e], bufs[b].at[s, pl.ds(j, 1), :], sem.at[b]).start(priority=j % 2)

    def wait_group(b):
        pltpu.make_async_copy(bufs[b], bufs[b], sem.at[b]).wait()

    def mix_token(group, s, b):
        tt = group * G + s
        words = bufs[b][s]
        down = lax.bitcast_convert_type(words & jnp.uint32(0xFFFF0000), jnp.float32)
        up = lax.bitcast_convert_type(words << 16, jnp.float32)
        h = jnp.sum(down * x_ref[pl.ds(tt, 1), :], axis=1, keepdims=True)
        gate = jnp.sum(jnp.where(tok_lane == tt, gate_t, 0.0), axis=1, keepdims=True)
        w = 0.5 * h * (1.0 + lax.erf(h * (2.0 ** -0.5))) * gate
        out = res_ref[pl.ds(tt, 1), :] + jnp.sum(w * up, axis=0, keepdims=True)
        o_ref[pl.ds(tt, 1), :] = jnp.where(pos_ref[pl.program_id(0)] + tt >= PAD_LEFT, out, 0.0)

    def step(group, b, prefetch):
        wait_group(b)
        for s in range(G):
            if prefetch:
                issue_token(group + PEER_BUFFERS - 1, s, (b + PEER_BUFFERS - 1) % PEER_BUFFERS)
            mix_token(group, s, b)

    for g in range(PEER_BUFFERS - 1):
        for s in range(G):
            issue_token(g, s, g)
    res = res_ref[...]
    x_ref[...] = res * lax.rsqrt(jnp.mean(res * res, axis=1, keepdims=True) + RMS_EPS) * g_ref[...]
    n_main = (PEER_GROUPS_PER_STEP - (PEER_BUFFERS - 1)) // PEER_BUFFERS

    def rotation(it, carry):
        for b in range(PEER_BUFFERS):
            step(it * PEER_BUFFERS + b, b, True)
        return carry

    lax.fori_loop(0, n_main, rotation, 0)
    for g in range(n_main * PEER_BUFFERS, PEER_GROUPS_PER_STEP):
        step(g, g % PEER_BUFFERS, g + PEER_BUFFERS - 1 < PEER_GROUPS_PER_STEP)


def peer_mix(idx, gate_t, res, norm_g, table, src_block, seq_pos):
    D = res.shape[1]
    n = idx.shape[0]
    tb = PEER_TOKENS_PER_STEP
    picks = pl.BlockSpec((1, PEER_PICKS, tb), lambda i: (i, 0, 0))
    return pl.pallas_call(
        _peer_mix_kernel,
        grid=(n,),
        in_specs=[
            pl.BlockSpec((1, PEER_PICKS, tb), lambda i: (i, 0, 0), memory_space=pltpu.SMEM),
            picks,
            pl.BlockSpec((tb, D), lambda i: (src_block(i), 0)),
            pl.BlockSpec((1, D), lambda i: (0, 0)),
            pl.BlockSpec(memory_space=pltpu.SMEM),
            pl.BlockSpec(memory_space=pl.ANY),
        ],
        out_specs=pl.BlockSpec((tb, D), lambda i: (i, 0)),
        out_shape=jax.ShapeDtypeStruct((n * tb, D), jnp.float32),
        scratch_shapes=[pltpu.VMEM((PEER_GROUP, PEER_PICKS, D), jnp.uint32) for _ in range(PEER_BUFFERS)]
        + [pltpu.SemaphoreType.DMA((PEER_BUFFERS,)), pltpu.VMEM((tb, D), jnp.float32)],
        compiler_params=pltpu.CompilerParams(
            dimension_semantics=("arbitrary",), vmem_limit_bytes=VMEM_LIMIT_BYTES),
        name="peer_mix",
    )(idx, gate_t, res, norm_g.reshape(1, D), seq_pos, table)


def _top_rows(s, k, codes):
    vals, poss = [], []
    for _ in range(k):
        m = jnp.max(s, axis=0, keepdims=True)
        pos = jnp.min(jnp.where(s == m, codes, jnp.inf), axis=0, keepdims=True)
        vals.append(m)
        poss.append(pos)
        s = jnp.where(codes == pos, -jnp.inf, s)
    return jnp.concatenate(vals, axis=0), jnp.concatenate(poss, axis=0)


def _candidate_pairs(k):
    return [(i, j) for i in range(k) for j in range(k) if (i + 1) * (j + 1) <= k]


def candidate_selectors():
    pairs = _candidate_pairs(PEER_TOPK)
    n_rows = -(-len(pairs) // 8) * 8
    first = jnp.zeros((n_rows, PEER_TOPK), jnp.float32).at[
        jnp.arange(len(pairs)), jnp.array([i for i, _ in pairs])].set(1.0)
    second = jnp.zeros((n_rows, PEER_TOPK), jnp.float32).at[
        jnp.arange(len(pairs)), jnp.array([j for _, j in pairs])].set(1.0)
    return first.astype(jnp.bfloat16), second.astype(jnp.bfloat16)


def _pick_rows(table, sel):
    out = jnp.zeros_like(sel)
    for r in range(table.shape[0]):
        out = out + jnp.where(sel == float(r), table[r:r + 1, :], 0.0)
    return out


def _peer_select_kernel(q_ref, keys_ref, sel1_ref, sel2_ref, idx_ref, gate_ref):
    K = PEER_TOPK
    half_w = D_KEY // 2
    tokens = q_ref.shape[0]
    key_rows = _iota2((N_KEYS, tokens), 0).astype(jnp.float32)
    sel1, sel2 = sel1_ref[...], sel2_ref[...]
    n_cand = len(_candidate_pairs(K))
    rank = _iota2((K, tokens), 0).astype(jnp.bfloat16)
    cand_rows = _iota2((sel1.shape[0], tokens), 0)
    cand_code = jnp.where(cand_rows < n_cand, _mm(sel1, rank) * K + _mm(sel2, rank), float(K * K))

    def pick_exact(sel, x):
        return sum(_mm(sel, piece) for piece in _bf16_pieces(x, 3))

    idx_rows, gate_rows = [], []
    for h in range(PEER_HEADS):
        tops = []
        for c in range(2):
            hc = 2 * h + c
            scores = _mm_nt(keys_ref[hc], q_ref[:, hc * half_w:(hc + 1) * half_w])
            tops.append(_top_rows(scores, K, key_rows))
        (s1, i1), (s2, i2) = tops
        cand = jnp.where(cand_rows < n_cand, pick_exact(sel1, s1) + pick_exact(sel2, s2), -jnp.inf)
        top_s, pos = _top_rows(cand, K, cand_code)
        first = jnp.floor(pos * (1.0 / K))
        second = pos - first * K
        expert = _pick_rows(i1, first) * N_KEYS + _pick_rows(i2, second)
        e = jnp.exp(top_s - top_s[0:1, :])
        idx_rows.append(expert.astype(jnp.int32))
        gate_rows.append(e / jnp.sum(e, axis=0, keepdims=True))
    idx_ref[0] = jnp.concatenate(idx_rows, axis=0)
    gate_ref[0] = jnp.concatenate(gate_rows, axis=0)


def peer_select(q, keys, n, src_block):
    W = q.shape[1]
    tb = PEER_TOKENS_PER_STEP
    out_block = pl.BlockSpec((1, PEER_PICKS, tb), lambda i: (i, 0, 0))
    sel1, sel2 = candidate_selectors()
    return pl.pallas_call(
        _peer_select_kernel,
        grid=(n,),
        in_specs=[pl.BlockSpec((tb, W), lambda i: (src_block(i), 0)),
                  pl.BlockSpec(keys.shape, lambda i: (0, 0, 0)),
                  pl.BlockSpec(sel1.shape, lambda i: (0, 0)),
                  pl.BlockSpec(sel2.shape, lambda i: (0, 0))],
        out_specs=[out_block, out_block],
        out_shape=[jax.ShapeDtypeStruct((n, PEER_PICKS, tb), jnp.int32),
                   jax.ShapeDtypeStruct((n, PEER_PICKS, tb), jnp.float32)],
        compiler_params=pltpu.CompilerParams(
            dimension_semantics=("arbitrary",), vmem_limit_bytes=VMEM_LIMIT_BYTES),
        name="peer_select",
    )(q, keys, sel1, sel2)


ATTN_TILES = ATTN_WIDTH // LANES
Q_HEADS_PER_KV = ATTN_Q_HEADS // ATTN_KV_HEADS


def _attn_kernel(q_ref, kvp_ref, kvc_ref, qg_ref, kg_ref, sink_ref, o_ref):
    f32, bf16 = jnp.float32, jnp.bfloat16
    nb = pl.program_id(1)
    lane = _iota2((LANES, LANES), 1)
    row = _iota2((LANES, LANES), 0)
    seg = ((lane >= HEAD_DIM) == (row >= HEAD_DIM)).astype(bf16)

    def head_rms(x, gain):
        ms = sum(_mm(piece, seg) for piece in _bf16_pieces(x * x, 2)) * (1.0 / HEAD_DIM)
        return x * lax.rsqrt(ms + RMS_EPS) * gain

    kv = jnp.concatenate([kvp_ref[...], kvc_ref[...]], axis=0)
    k = head_rms(kv[:, :LANES], kg_ref[...])
    v = kv[:, LANES:]
    lane_k = _iota2(k.shape, 1)
    k_sw, v_sw = pltpu.roll(k, HEAD_DIM, 1), pltpu.roll(v, HEAD_DIM, 1)
    k_dup = [jnp.where(lane_k < HEAD_DIM, k, k_sw).astype(bf16), jnp.where(lane_k < HEAD_DIM, k_sw, k).astype(bf16)]
    v_dup = [jnp.where(lane_k < HEAD_DIM, v, v_sw).astype(bf16), jnp.where(lane_k < HEAD_DIM, v_sw, v).astype(bf16)]

    qi = _iota2((2 * BLOCK, 2 * BLOCK), 0) & (BLOCK - 1)
    ks = _iota2((2 * BLOCK, 2 * BLOCK), 1)
    dist = qi + BLOCK - ks
    key_pos = (nb - 1) * BLOCK + ks
    allowed = (dist >= 0) & (dist < WINDOW) & (key_pos >= PAD_LEFT)
    upper = _iota2((2 * BLOCK, 1), 0) < BLOCK
    head0 = _iota2((BLOCK, LANES), 1) < HEAD_DIM

    tiles = range(ATTN_TILES)
    q_t = [head_rms(q_ref[:, t * LANES:(t + 1) * LANES], qg_ref[...]) for t in tiles]
    q2 = [jnp.concatenate([jnp.where(head0, q_t[t], 0.0), jnp.where(head0, 0.0, q_t[t])], axis=0).astype(bf16)
          for t in tiles]
    s = [_mm_nt(q2[t], k_dup[(2 * t) // Q_HEADS_PER_KV]) * ATTN_SCALE for t in tiles]
    s = [jnp.where(allowed, s[t], MASK_VALUE) for t in tiles]
    sink = [jnp.where(upper, sink_ref[2 * t], sink_ref[2 * t + 1]) for t in tiles]
    m = [jnp.maximum(jnp.max(s[t], axis=1, keepdims=True), sink[t]) for t in tiles]
    e = [jnp.exp(s[t] - m[t]) for t in tiles]
    denom = [jnp.sum(e[t], axis=1, keepdims=True) + jnp.exp(sink[t] - m[t]) for t in tiles]
    p = [(e[t] / denom[t]).astype(bf16) for t in tiles]
    o2 = [_mm(p[t], v_dup[(2 * t) // Q_HEADS_PER_KV]) for t in tiles]
    o_ref[...] = jnp.concatenate(
        [jnp.where(head0, o2[t][:BLOCK], o2[t][BLOCK:]) for t in tiles], axis=1).astype(o_ref.dtype)


def swa_sink_attention(p, B, LP, col_q, col_kv, q_gain, k_gain, sinks):
    nblk = LP // BLOCK
    kvw = 2 * KV_WIDTH
    qg = jnp.tile(q_gain, LANES // HEAD_DIM).reshape(1, LANES)
    kg = jnp.tile(k_gain, LANES // HEAD_DIM).reshape(1, LANES)
    return pl.pallas_call(
        _attn_kernel,
        grid=(B, nblk),
        in_specs=[
            pl.BlockSpec((BLOCK, ATTN_WIDTH), lambda b, n: (b * nblk + n, col_q // ATTN_WIDTH)),
            pl.BlockSpec((BLOCK, kvw), lambda b, n: (b * nblk + jnp.maximum(n - 1, 0), col_kv // kvw)),
            pl.BlockSpec((BLOCK, kvw), lambda b, n: (b * nblk + n, col_kv // kvw)),
            pl.BlockSpec((1, LANES), lambda b, n: (0, 0)),
            pl.BlockSpec((1, LANES), lambda b, n: (0, 0)),
            pl.BlockSpec(memory_space=pltpu.SMEM),
        ],
        out_specs=pl.BlockSpec((BLOCK, ATTN_WIDTH), lambda b, n: (b * nblk + n, 0)),
        out_shape=jax.ShapeDtypeStruct((B * LP, ATTN_WIDTH), jnp.bfloat16),
        compiler_params=pltpu.CompilerParams(
            dimension_semantics=("arbitrary", "arbitrary"), vmem_limit_bytes=VMEM_LIMIT_BYTES),
        name="swa_sink_attention",
    )(p, p, p, qg, kg, sinks)


def kernel(x, meta_tokens, norm1_g, w_in, shift_mu, w0, w_up, a0, a_up, g_up, k_k, k_a, r_k, gn_w, gn_b,
           q_gain, k_gain, sinks, w_out, norm2_g, peer_query, peer_sub_keys, peer_down, peer_up):
    B, S, D = x.shape
    LP = S + BLOCK
    T = B * LP
    meta = jnp.broadcast_to(meta_tokens[None], (B, N_META, D))
    h = jnp.concatenate([jnp.zeros((B, PAD_LEFT, D), x.dtype), meta, x], axis=1)
    ht = h.reshape(T, D)

    w = w_in[0]
    n_lora = DECAY_LORA + AAA_LORA + GATE_LORA
    rwkv0 = ATTN_COLS
    lora0 = rwkv0 + 3 * RWKV_WIDTH
    w_in_b = jnp.concatenate([
        w[:, rwkv0:lora0], w[:, :ATTN_WIDTH],
        jnp.pad(w[:, lora0:lora0 + n_lora], ((0, 0), (0, LORA_PAD - n_lora))),
        w[:, ATTN_WIDTH:ATTN_COLS]], axis=1).astype(jnp.bfloat16)
    col_q = 3 * RWKV_WIDTH
    col_l = col_q + ATTN_WIDTH
    col_kv = col_l + LORA_PAD
    n_in = col_kv + 2 * KV_WIDTH
    p = norm_matmul(ht, norm1_g[0], w_in_b, 512, n_in // 2)

    y_rwkv = rwkv7_mix(p, B, LP, 0, RWKV_WIDTH, 2 * RWKV_WIDTH, col_l,
                       *rwkv_params(shift_mu[0], w0[0], w_up[0], a0[0], a_up[0], g_up[0], k_k[0], k_a[0], r_k[0],
                                    gn_w[0], gn_b[0]))
    y_attn = swa_sink_attention(p, B, LP, col_q, col_kv, q_gain[0], k_gain[0], sinks[0])
    h2 = matmul_residual(y_rwkv, y_attn, w_out[0].astype(jnp.bfloat16), ht, 512, 1024)

    pq = norm_matmul(h2, norm2_g[0], peer_query[0].astype(jnp.bfloat16), 512, 1024, jnp.bfloat16)
    keys = peer_sub_keys[0].reshape(2 * PEER_HEADS, N_KEYS, D_KEY // 2).astype(jnp.bfloat16)
    table = pack_expert_table(peer_down[0], peer_up[0])

    blocks_per_seq = LP // BLOCK
    kept_per_seq = blocks_per_seq - 1

    def kept_block(i):
        return (i // kept_per_seq) * blocks_per_seq + 1 + i % kept_per_seq

    idx_t, gate_t = peer_select(pq, keys, B * kept_per_seq, kept_block)
    seq_pos = (1 + jnp.arange(B * kept_per_seq, dtype=jnp.int32) % kept_per_seq) * BLOCK
    return peer_mix(idx_t, gate_t, h2, norm2_g[0], table, kept_block, seq_pos).reshape(B, S, D)
```

```python
import functools

import jax
import jax.numpy as jnp
from jax import lax
from jax.experimental import pallas as pl
from jax.experimental.pallas import tpu as pltpu

N_META = 16
BLOCK = 128
PAD_LEFT = BLOCK - N_META
HEAD_DIM = 64
RWKV_WIDTH = 1024
RWKV_HEADS = RWKV_WIDTH // HEAD_DIM
ATTN_WIDTH = 1024
ATTN_Q_HEADS = ATTN_WIDTH // HEAD_DIM
ATTN_KV_HEADS = 2
KV_WIDTH = ATTN_KV_HEADS * HEAD_DIM
WINDOW = 128
ATTN_SCALE = HEAD_DIM ** -0.5
MASK_VALUE = -1e30
DECAY_LORA = 64
AAA_LORA = 64
GATE_LORA = 160
RMS_EPS = 1e-6
GN_EPS = 64e-5
ATTN_COLS = ATTN_WIDTH + 2 * KV_WIDTH
PEER_HEADS = 8
N_KEYS = 128
PEER_TOPK = 16
D_KEY = 256
PEER_BLOCK = 128

VMEM_LIMIT_BYTES = 48 * 1024 * 1024


def _norm_matmul_kernel(x_ref, g_ref, w_ref, o_ref):
    x = x_ref[...]
    ms = jnp.mean(x * x, axis=-1, keepdims=True)
    u = x * lax.rsqrt(ms + RMS_EPS) * g_ref[...]
    o_ref[...] = jnp.dot(
        u.astype(jnp.bfloat16), w_ref[...], preferred_element_type=jnp.float32).astype(o_ref.dtype)


def norm_matmul(x, g, w, tm, tn, out_dtype=jnp.float32):
    m, k = x.shape
    n = w.shape[1]
    return pl.pallas_call(
        _norm_matmul_kernel,
        grid=(n // tn, m // tm),
        in_specs=[
            pl.BlockSpec((tm, k), lambda j, i: (i, 0)),
            pl.BlockSpec((1, k), lambda j, i: (0, 0)),
            pl.BlockSpec((k, tn), lambda j, i: (0, j)),
        ],
        out_specs=pl.BlockSpec((tm, tn), lambda j, i: (i, j)),
        out_shape=jax.ShapeDtypeStruct((m, n), out_dtype),
        compiler_params=pltpu.CompilerParams(
            dimension_semantics=("arbitrary", "arbitrary"), vmem_limit_bytes=VMEM_LIMIT_BYTES),
        name="norm_matmul",
    )(x, g.reshape(1, k), w)


def _matmul_residual_kernel(xa_ref, xb_ref, w_ref, r_ref, o_ref):
    ka = xa_ref.shape[1]
    o_ref[...] = (r_ref[...]
                  + jnp.dot(xa_ref[...], w_ref[:ka, :], preferred_element_type=jnp.float32)
                  + jnp.dot(xb_ref[...], w_ref[ka:, :], preferred_element_type=jnp.float32))


def matmul_residual(xa, xb, w, r, tm, tn):
    m, ka = xa.shape
    kb = xb.shape[1]
    n = w.shape[1]
    return pl.pallas_call(
        _matmul_residual_kernel,
        grid=(n // tn, m // tm),
        in_specs=[
            pl.BlockSpec((tm, ka), lambda j, i: (i, 0)),
            pl.BlockSpec((tm, kb), lambda j, i: (i, 0)),
            pl.BlockSpec((ka + kb, tn), lambda j, i: (0, j)),
            pl.BlockSpec((tm, tn), lambda j, i: (i, j)),
        ],
        out_specs=pl.BlockSpec((tm, tn), lambda j, i: (i, j)),
        out_shape=jax.ShapeDtypeStruct((m, n), jnp.float32),
        compiler_params=pltpu.CompilerParams(
            dimension_semantics=("arbitrary", "arbitrary"), vmem_limit_bytes=VMEM_LIMIT_BYTES),
        name="matmul_residual",
    )(xa, xb, w, r)


RWKV_CHUNK = 64
LANES = 128
RWKV_TILES = RWKV_WIDTH // LANES
LORA_PAD = 512


def _dot(a, b, dims):
    return lax.dot_general(a, b, (dims, ((), ())), preferred_element_type=jnp.float32)


def _mm(a, b):
    return _dot(a, b, ((1,), (0,)))


def _mm_nt(a, b):
    return _dot(a, b, ((1,), (1,)))


def _mm_tn(a, b):
    return _dot(a, b, ((0,), (0,)))


def _iota2(shape, axis):
    return lax.broadcasted_iota(jnp.int32, shape, axis)


def _sigmoid(x):
    return 1.0 / (1.0 + jnp.exp(-x))


def _bf16_pieces(x, n):
    pieces = []
    for _ in range(n):
        p = x.astype(jnp.bfloat16)
        pieces.append(p)
        x = x - p.astype(jnp.float32)
    return pieces


def _token_shift(x, prev_ref, mu):
    rolled = pltpu.roll(x, 1, 0)
    prev = jnp.where(_iota2(x.shape, 0) == 0, prev_ref[...], rolled)
    prev_ref[...] = x[x.shape[0] - 1:, :]
    return x + mu * (prev - x)


def _rwkv_kernel(xr_ref, xk_ref, xv_ref, xl_ref, mu_ref, mul_ref, vec_ref, wup_ref, aup_ref, gup_ref,
                 o_ref, s_ref, pr_ref, pk_ref, pv_ref, pl_ref):
    C = RWKV_CHUNK
    f32, bf16 = jnp.float32, jnp.bfloat16

    @pl.when(pl.program_id(1) == 0)
    def _():
        s_ref[...] = jnp.zeros_like(s_ref)
        pr_ref[...] = jnp.zeros_like(pr_ref)
        pk_ref[...] = jnp.zeros_like(pk_ref)
        pv_ref[...] = jnp.zeros_like(pv_ref)
        pl_ref[...] = jnp.zeros_like(pl_ref)

    r = _token_shift(xr_ref[...], pr_ref, mu_ref[0:1, :])
    kraw = _token_shift(xk_ref[...], pk_ref, mu_ref[1:2, :])
    v = _token_shift(xv_ref[...], pv_ref, mu_ref[2:3, :])
    xl = _token_shift(xl_ref[...], pl_ref, mul_ref[...])

    w0, a0, k_k, k_a = vec_ref[0:1, :], vec_ref[1:2, :], vec_ref[2:3, :], vec_ref[3:4, :]
    r_k, gn_w, gn_b = vec_ref[4:5, :], vec_ref[5:6, :], vec_ref[6:7, :]

    x_wa = xl[:, :LANES]
    wl = w0 + _mm(jnp.tanh(x_wa).astype(jnp.bfloat16), wup_ref[...])
    z = -wl
    softplus = jnp.maximum(z, 0.0) + jnp.log(1.0 + jnp.exp(-jnp.abs(z)))
    lw = -jnp.exp(-softplus - 0.5)
    a = _sigmoid(a0 + _mm(x_wa.astype(jnp.bfloat16), aup_ref[...]))
    g = _mm(_sigmoid(xl[:, LANES:LANES + 256]).astype(jnp.bfloat16), gup_ref[...])

    tri = (_iota2((C, C), 0) >= _iota2((C, C), 1)).astype(bf16)
    cl = sum(_mm(tri, piece) for piece in _bf16_pieces(lw, 3))
    e_pos = jnp.exp(cl)
    e_excl = jnp.exp(cl - lw)
    e_neg = 1.0 / e_pos
    e_end = e_pos[C - 1:, :]
    e_tail = e_end * e_neg

    lane = _iota2((LANES, LANES), 1)
    row = _iota2((LANES, LANES), 0)
    same_head = (lane >= HEAD_DIM) == (row >= HEAD_DIM)
    seg = same_head.astype(bf16)
    eye = (lane == row).astype(f32)
    t_row, t_col = row & (C - 1), lane & (C - 1)
    strict = t_col < t_row
    in_block4 = (t_row >> 2) == (t_col >> 2)
    in_block16 = (t_row >> 4) == (t_col >> 4)
    incl = t_col <= t_row
    incl2 = jnp.concatenate([incl, incl], axis=1)
    lane_c = _iota2((C, LANES), 1)
    head0 = lane_c < HEAD_DIM

    def seg_sum(x):
        return sum(_mm(piece, seg) for piece in _bf16_pieces(x, 2))

    def by_head(x):
        return jnp.concatenate([jnp.where(head0, x, 0.0), jnp.where(head0, 0.0, x)], axis=0).astype(bf16)

    def twice(x):
        return jnp.concatenate([x, x], axis=0)

    tiles = range(RWKV_TILES)
    sls = [slice(j * LANES, (j + 1) * LANES) for j in tiles]
    kr = [kraw[:, sl] for sl in sls]
    kkr = [kr[j] * k_k[:, sls[j]] for j in tiles]
    kk_ss = [seg_sum(kkr[j] * kkr[j]) for j in tiles]
    kk = [kkr[j] / jnp.maximum(jnp.sqrt(kk_ss[j]), 1e-12) for j in tiles]
    a_t = [a[:, sl] for sl in sls]
    k2 = [kr[j] * (1.0 + (a_t[j] - 1.0) * k_a[:, sls[j]]) for j in tiles]
    r_t = [r[:, sl] for sl in sls]
    v_t = [v[:, sl] for sl in sls]
    kka = [kk[j] * a_t[j] for j in tiles]
    alpha = [-kk[j] * e_excl[:, sls[j]] for j in tiles]
    r_dec = [r_t[j] * e_pos[:, sls[j]] for j in tiles]
    beta = [kka[j] * e_neg[:, sls[j]] for j in tiles]
    k_neg = [k2[j] * e_neg[:, sls[j]] for j in tiles]
    s_old = [s_ref[j] for j in tiles]

    gram = [_mm_nt(jnp.concatenate([by_head(alpha[j]), by_head(r_dec[j])], axis=0),
                   jnp.concatenate([by_head(beta[j]), by_head(k_neg[j])], axis=0)) for j in tiles]
    m1 = [_mm_nt(jnp.concatenate([alpha[j], r_dec[j]], axis=0).astype(bf16), s_old[j].astype(bf16))
          for j in tiles]
    a_s = [jnp.where(strict, gram[j][:LANES, :LANES], 0.0) for j in tiles]
    b_s = [jnp.where(strict, gram[j][:LANES, LANES:], 0.0).astype(bf16) for j in tiles]
    r_i = [jnp.where(incl2, gram[j][LANES:, :], 0.0).astype(bf16) for j in tiles]
    vv = [twice(v_t[j]).astype(bf16) for j in tiles]
    rhs = [twice(m1[j][:C]) + _mm(b_s[j], vv[j]) for j in tiles]
    def mmb(x, y):
        return _mm(x.astype(bf16), y.astype(bf16))

    def inv_index4(nil):
        sq = [mmb(nil[j], nil[j]) for j in tiles]
        return [eye + nil[j] + sq[j] + mmb(sq[j], nil[j]) for j in tiles]

    t0 = inv_index4([jnp.where(in_block4, a_s[j], 0.0) for j in tiles])
    n1 = [mmb(t0[j], jnp.where(in_block16 & ~in_block4, a_s[j], 0.0)) for j in tiles]
    p1 = inv_index4(n1)
    t1 = [mmb(p1[j], t0[j]) for j in tiles]
    n2 = [mmb(t1[j], jnp.where(in_block16, 0.0, a_s[j])) for j in tiles]
    p2 = inv_index4(n2)
    t1_rhs = [mmb(t1[j], rhs[j]) for j in tiles]
    u_st = [mmb(p2[j], t1_rhs[j]) for j in tiles]
    y_st = [twice(m1[j][C:]) + _mm(r_i[j], jnp.concatenate([u_st[j].astype(bf16), vv[j]], axis=0))
            for j in tiles]
    u_t = [jnp.where(head0, u_st[j][:C], u_st[j][C:]) for j in tiles]
    y_t = [jnp.where(head0, y_st[j][:C], y_st[j][C:]) for j in tiles]

    upd = [_mm_tn(jnp.concatenate([u_t[j], v_t[j]], axis=0).astype(bf16),
                  jnp.concatenate([kka[j] * e_tail[:, sls[j]], k2[j] * e_tail[:, sls[j]]], axis=0).astype(bf16))
           for j in tiles]
    for j in tiles:
        s_ref[j] = s_old[j] * e_end[:, sls[j]] + jnp.where(same_head, upd[j], 0.0)

    mu_y = [seg_sum(y_t[j]) * (1.0 / HEAD_DIM) for j in tiles]
    dev = [y_t[j] - mu_y[j] for j in tiles]
    var = [seg_sum(dev[j] * dev[j]) * (1.0 / HEAD_DIM) for j in tiles]
    bonus = [seg_sum(r_t[j] * k2[j] * r_k[:, sls[j]]) * v_t[j] for j in tiles]
    outs = [(dev[j] * lax.rsqrt(var[j] + GN_EPS) * gn_w[:, sls[j]] + gn_b[:, sls[j]] + bonus[j]) * g[:, sls[j]]
            for j in tiles]
    o_ref[...] = jnp.concatenate(outs, axis=1).astype(o_ref.dtype)


def rwkv_params(shift_mu, w0, w_up, a0, a_up, g_up, k_k, k_a, r_k, gn_w, gn_b):
    W = RWKV_WIDTH
    n_lora = DECAY_LORA + AAA_LORA + GATE_LORA
    mu_rkv = shift_mu[:3 * W].reshape(3, W)
    mu_l = jnp.pad(shift_mu[3 * W:], (0, LORA_PAD - n_lora)).reshape(1, LORA_PAD)
    vecs = jnp.stack([w0, a0, k_k, k_a, r_k.reshape(W), gn_w, gn_b, jnp.zeros_like(w0)])
    w_up_p = jnp.pad(w_up, ((0, LANES - DECAY_LORA), (0, 0))).astype(jnp.bfloat16)
    a_up_p = jnp.pad(a_up, ((DECAY_LORA, LANES - DECAY_LORA - AAA_LORA), (0, 0))).astype(jnp.bfloat16)
    g_up_p = jnp.pad(g_up, ((0, 256 - GATE_LORA), (0, 0))).astype(jnp.bfloat16)
    return mu_rkv, mu_l, vecs, w_up_p, a_up_p, g_up_p


def rwkv7_mix(p, B, LP, col_r, col_k, col_v, col_l, mu_rkv, mu_l, vecs, w_up, a_up, g_up):
    C = RWKV_CHUNK
    nc = LP // C
    W = RWKV_WIDTH

    def col_spec(width, col):
        return pl.BlockSpec((C, width), lambda b, c: (b * nc + c, col // width))

    def full(shape):
        return pl.BlockSpec(shape, lambda b, c: (0,) * len(shape))

    return pl.pallas_call(
        _rwkv_kernel,
        grid=(B, nc),
        in_specs=[
            col_spec(W, col_r), col_spec(W, col_k), col_spec(W, col_v), col_spec(LORA_PAD, col_l),
            full(mu_rkv.shape), full(mu_l.shape), full(vecs.shape),
            full(w_up.shape), full(a_up.shape), full(g_up.shape),
        ],
        out_specs=pl.BlockSpec((C, W), lambda b, c: (b * nc + c, 0)),
        out_shape=jax.ShapeDtypeStruct((B * LP, W), jnp.bfloat16),
        scratch_shapes=[
            pltpu.VMEM((RWKV_TILES, LANES, LANES), jnp.float32),
            pltpu.VMEM((1, W), jnp.float32), pltpu.VMEM((1, W), jnp.float32), pltpu.VMEM((1, W), jnp.float32),
            pltpu.VMEM((1, LORA_PAD), jnp.float32),
        ],
        compiler_params=pltpu.CompilerParams(
            dimension_semantics=("arbitrary", "arbitrary"), vmem_limit_bytes=VMEM_LIMIT_BYTES),
        name="rwkv7_mix",
    )(p, p, p, p, mu_rkv, mu_l, vecs, w_up, a_up, g_up)


PEER_PICKS = PEER_HEADS * PEER_TOPK
PEER_TOKENS_PER_STEP = 128
PEER_GROUP = 4
PEER_GROUPS_PER_STEP = PEER_TOKENS_PER_STEP // PEER_GROUP
PEER_BUFFERS = 3


def pack_expert_table(down, up):
    def bits(t):
        return lax.bitcast_convert_type(t.astype(jnp.bfloat16), jnp.uint16).astype(jnp.uint32)
    return ((bits(down) << 16) | bits(up))[:, None, :]


def _peer_mix_kernel(idx_ref, gate_t_ref, res_ref, g_ref, pos_ref, tab_ref, o_ref, *scratch):
    bufs, sem, x_ref = scratch[:PEER_BUFFERS], scratch[PEER_BUFFERS], scratch[PEER_BUFFERS + 1]
    G = PEER_GROUP
    gate_t = gate_t_ref[0]
    tok_lane = _iota2(gate_t.shape, 1)

    def issue_token(group, s, b):
        for j in range(PEER_PICKS):
            e = idx_ref[0, j, group * G + s]
            pltpu.make_async_copy(tab_ref.at[e], bufs[b].at[s, pl.ds(j, 1), :], sem.at[b]).start(priority=j % 2)

    def wait_group(b):
        pltpu.make_async_copy(bufs[b], bufs[b], sem.at[b]).wait()

    def mix_token(group, s, b):
        tt = group * G + s
        words = bufs[b][s]
        down = lax.bitcast_convert_type(words & jnp.uint32(0xFFFF0000), jnp.float32)
        up = lax.bitcast_convert_type(words << 16, jnp.float32)
        h = jnp.sum(down * x_ref[pl.ds(tt, 1), :], axis=1, keepdims=True)
        gate = jnp.sum(jnp.where(tok_lane == tt, gate_t, 0.0), axis=1, keepdims=True)
        w = 0.5 * h * (1.0 + lax.erf(h * (2.0 ** -0.5))) * gate
        out = res_ref[pl.ds(tt, 1), :] + jnp.sum(w * up, axis=0, keepdims=True)
        o_ref[pl.ds(tt, 1), :] = jnp.where(pos_ref[pl.program_id(0)] + tt >= PAD_LEFT, out, 0.0)

    def step(group, b, prefetch):
        wait_group(b)
        for s in range(G):
            if prefetch:
                issue_token(group + PEER_BUFFERS - 1, s, (b + PEER_BUFFERS - 1) % PEER_BUFFERS)
            mix_token(group, s, b)

    for g in range(PEER_BUFFERS - 1):
        for s in range(G):
            issue_token(g, s, g)
    res = res_ref[...]
    x_ref[...] = res * lax.rsqrt(jnp.mean(res * res, axis=1, keepdims=True) + RMS_EPS) * g_ref[...]
    n_main = (PEER_GROUPS_PER_STEP - (PEER_BUFFERS - 1)) // PEER_BUFFERS

    def rotation(it, carry):
        for b in range(PEER_BUFFERS):
            step(it * PEER_BUFFERS + b, b, True)
        return carry

    lax.fori_loop(0, n_main, rotation, 0)
    for g in range(n_main * PEER_BUFFERS, PEER_GROUPS_PER_STEP):
        step(g, g % PEER_BUFFERS, g + PEER_BUFFERS - 1 < PEER_GROUPS_PER_STEP)


def peer_mix(idx, gate_t, res, norm_g, table, src_block, seq_pos):
    D = res.shape[1]
    n = idx.shape[0]
    tb = PEER_TOKENS_PER_STEP
    picks = pl.BlockSpec((1, PEER_PICKS, tb), lambda i: (i, 0, 0))
    return pl.pallas_call(
        _peer_mix_kernel,
        grid=(n,),
        in_specs=[
            pl.BlockSpec((1, PEER_PICKS, tb), lambda i: (i, 0, 0), memory_space=pltpu.SMEM),
            picks,
            pl.BlockSpec((tb, D), lambda i: (src_block(i), 0)),
            pl.BlockSpec((1, D), lambda i: (0, 0)),
            pl.BlockSpec(memory_space=pltpu.SMEM),
            pl.BlockSpec(memory_space=pl.ANY),
        ],
        out_specs=pl.BlockSpec((tb, D), lambda i: (i, 0)),
        out_shape=jax.ShapeDtypeStruct((n * tb, D), jnp.float32),
        scratch_shapes=[pltpu.VMEM((PEER_GROUP, PEER_PICKS, D), jnp.uint32) for _ in range(PEER_BUFFERS)]
        + [pltpu.SemaphoreType.DMA((PEER_BUFFERS,)), pltpu.VMEM((tb, D), jnp.float32)],
        compiler_params=pltpu.CompilerParams(
            dimension_semantics=("arbitrary",), vmem_limit_bytes=VMEM_LIMIT_BYTES),
        name="peer_mix",
    )(idx, gate_t, res, norm_g.reshape(1, D), seq_pos, table)


def _top_rows(s, k, codes):
    vals, poss = [], []
    for _ in range(k):
        m = jnp.max(s, axis=0, keepdims=True)
        pos = jnp.min(jnp.where(s == m, codes, jnp.inf), axis=0, keepdims=True)
        vals.append(m)
        poss.append(pos)
        s = jnp.where(codes == pos, -jnp.inf, s)
    return jnp.concatenate(vals, axis=0), jnp.concatenate(poss, axis=0)


def _candidate_pairs(k):
    return [(i, j) for i in range(k) for j in range(k) if (i + 1) * (j + 1) <= k]


def candidate_selectors():
    pairs = _candidate_pairs(PEER_TOPK)
    n_rows = -(-len(pairs) // 8) * 8
    first = jnp.zeros((n_rows, PEER_TOPK), jnp.float32).at[
        jnp.arange(len(pairs)), jnp.array([i for i, _ in pairs])].set(1.0)
    second = jnp.zeros((n_rows, PEER_TOPK), jnp.float32).at[
        jnp.arange(len(pairs)), jnp.array([j for _, j in pairs])].set(1.0)
    return first.astype(jnp.bfloat16), second.astype(jnp.bfloat16)


def _pick_rows(table, sel):
    out = jnp.zeros_like(sel)
    for r in range(table.shape[0]):
        out = out + jnp.where(sel == float(r), table[r:r + 1, :], 0.0)
    return out


def _peer_select_kernel(q_ref, keys_ref, sel1_ref, sel2_ref, idx_ref, gate_ref):
    K = PEER_TOPK
    half_w = D_KEY // 2
    tokens = q_ref.shape[0]
    key_rows = _iota2((N_KEYS, tokens), 0).astype(jnp.float32)
    sel1, sel2 = sel1_ref[...], sel2_ref[...]
    n_cand = len(_candidate_pairs(K))
    rank = _iota2((K, tokens), 0).astype(jnp.bfloat16)
    cand_rows = _iota2((sel1.shape[0], tokens), 0)
    cand_code = jnp.where(cand_rows < n_cand, _mm(sel1, rank) * K + _mm(sel2, rank), float(K * K))

    def pick_exact(sel, x):
        return sum(_mm(sel, piece) for piece in _bf16_pieces(x, 3))

    idx_rows, gate_rows = [], []
    for h in range(PEER_HEADS):
        tops = []
        for c in range(2):
            hc = 2 * h + c
            scores = _mm_nt(keys_ref[hc], q_ref[:, hc * half_w:(hc + 1) * half_w])
            tops.append(_top_rows(scores, K, key_rows))
        (s1, i1), (s2, i2) = tops
        cand = jnp.where(cand_rows < n_cand, pick_exact(sel1, s1) + pick_exact(sel2, s2), -jnp.inf)
        top_s, pos = _top_rows(cand, K, cand_code)
        first = jnp.floor(pos * (1.0 / K))
        second = pos - first * K
        expert = _pick_rows(i1, first) * N_KEYS + _pick_rows(i2, second)
        e = jnp.exp(top_s - top_s[0:1, :])
        idx_rows.append(expert.astype(jnp.int32))
        gate_rows.append(e / jnp.sum(e, axis=0, keepdims=True))
    idx_ref[0] = jnp.concatenate(idx_rows, axis=0)
    gate_ref[0] = jnp.concatenate(gate_rows, axis=0)


def peer_select(q, keys, n, src_block):
    W = q.shape[1]
    tb = PEER_TOKENS_PER_STEP
    out_block = pl.BlockSpec((1, PEER_PICKS, tb), lambda i: (i, 0, 0))
    sel1, sel2 = candidate_selectors()
    return pl.pallas_call(
        _peer_select_kernel,
        grid=(n,),
        in_specs=[pl.BlockSpec((tb, W), lambda i: (src_block(i), 0)),
                  pl.BlockSpec(keys.shape, lambda i: (0, 0, 0)),
                  pl.BlockSpec(sel1.shape, lambda i: (0, 0)),
                  pl.BlockSpec(sel2.shape, lambda i: (0, 0))],
        out_specs=[out_block, out_block],
        out_shape=[jax.ShapeDtypeStruct((n, PEER_PICKS, tb), jnp.int32),
                   jax.ShapeDtypeStruct((n, PEER_PICKS, tb), jnp.float32)],
        compiler_params=pltpu.CompilerParams(
            dimension_semantics=("arbitrary",), vmem_limit_bytes=VMEM_LIMIT_BYTES),
        name="peer_select",
    )(q, keys, sel1, sel2)


ATTN_TILES = ATTN_WIDTH // LANES
Q_HEADS_PER_KV = ATTN_Q_HEADS // ATTN_KV_HEADS


def _attn_kernel(q_ref, kvp_ref, kvc_ref, qg_ref, kg_ref, sink_ref, o_ref):
    f32, bf16 = jnp.float32, jnp.bfloat16
    nb = pl.program_id(1)
    lane = _iota2((LANES, LANES), 1)
    row = _iota2((LANES, LANES), 0)
    seg = ((lane >= HEAD_DIM) == (row >= HEAD_DIM)).astype(bf16)

    def head_rms(x, gain):
        ms = sum(_mm(piece, seg) for piece in _bf16_pieces(x * x, 2)) * (1.0 / HEAD_DIM)
        return x * lax.rsqrt(ms + RMS_EPS) * gain

    kv = jnp.concatenate([kvp_ref[...], kvc_ref[...]], axis=0)
    k = head_rms(kv[:, :LANES], kg_ref[...])
    v = kv[:, LANES:]
    lane_k = _iota2(k.shape, 1)
    k_sw, v_sw = pltpu.roll(k, HEAD_DIM, 1), pltpu.roll(v, HEAD_DIM, 1)
    k_dup = [jnp.where(lane_k < HEAD_DIM, k, k_sw).astype(bf16), jnp.where(lane_k < HEAD_DIM, k_sw, k).astype(bf16)]
    v_dup = [jnp.where(lane_k < HEAD_DIM, v, v_sw).astype(bf16), jnp.where(lane_k < HEAD_DIM, v_sw, v).astype(bf16)]

    qi = _iota2((2 * BLOCK, 2 * BLOCK), 0) & (BLOCK - 1)
    ks = _iota2((2 * BLOCK, 2 * BLOCK), 1)
    dist = qi + BLOCK - ks
    key_pos = (nb - 1) * BLOCK + ks
    allowed = (dist >= 0) & (dist < WINDOW) & (key_pos >= PAD_LEFT)
    upper = _iota2((2 * BLOCK, 1), 0) < BLOCK
    head0 = _iota2((BLOCK, LANES), 1) < HEAD_DIM

    tiles = range(ATTN_TILES)
    q_t = [head_rms(q_ref[:, t * LANES:(t + 1) * LANES], qg_ref[...]) for t in tiles]
    q2 = [jnp.concatenate([jnp.where(head0, q_t[t], 0.0), jnp.where(head0, 0.0, q_t[t])], axis=0).astype(bf16)
          for t in tiles]
    s = [_mm_nt(q2[t], k_dup[(2 * t) // Q_HEADS_PER_KV]) * ATTN_SCALE for t in tiles]
    s = [jnp.where(allowed, s[t], MASK_VALUE) for t in tiles]
    sink = [jnp.where(upper, sink_ref[2 * t], sink_ref[2 * t + 1]) for t in tiles]
    m = [jnp.maximum(jnp.max(s[t], axis=1, keepdims=True), sink[t]) for t in tiles]
    e = [jnp.exp(s[t] - m[t]) for t in tiles]
    denom = [jnp.sum(e[t], axis=1, keepdims=True) + jnp.exp(sink[t] - m[t]) for t in tiles]
    p = [(e[t] / denom[t]).astype(bf16) for t in tiles]
    o2 = [_mm(p[t], v_dup[(2 * t) // Q_HEADS_PER_KV]) for t in tiles]
    o_ref[...] = jnp.concatenate(
        [jnp.where(head0, o2[t][:BLOCK], o2[t][BLOCK:]) for t in tiles], axis=1).astype(o_ref.dtype)


def swa_sink_attention(p, B, LP, col_q, col_kv, q_gain, k_gain, sinks):
    nblk = LP // BLOCK
    kvw = 2 * KV_WIDTH
    qg = jnp.tile(q_gain, LANES // HEAD_DIM).reshape(1, LANES)
    kg = jnp.tile(k_gain, LANES // HEAD_DIM).reshape(1, LANES)
    return pl.pallas_call(
        _attn_kernel,
        grid=(B, nblk),
        in_specs=[
            pl.BlockSpec((BLOCK, ATTN_WIDTH), lambda b, n: (b * nblk + n, col_q // ATTN_WIDTH)),
            pl.BlockSpec((BLOCK, kvw), lambda b, n: (b * nblk + jnp.maximum(n - 1, 0), col_kv // kvw)),
            pl.BlockSpec((BLOCK, kvw), lambda b, n: (b * nblk + n, col_kv // kvw)),
            pl.BlockSpec((1, LANES), lambda b, n: (0, 0)),
            pl.BlockSpec((1, LANES), lambda b, n: (0, 0)),
            pl.BlockSpec(memory_space=pltpu.SMEM),
        ],
        out_specs=pl.BlockSpec((BLOCK, ATTN_WIDTH), lambda b, n: (b * nblk + n, 0)),
        out_shape=jax.ShapeDtypeStruct((B * LP, ATTN_WIDTH), jnp.bfloat16),
        compiler_params=pltpu.CompilerParams(
            dimension_semantics=("arbitrary", "arbitrary"), vmem_limit_bytes=VMEM_LIMIT_BYTES),
        name="swa_sink_attention",
    )(p, p, p, qg, kg, sinks)


def kernel(x, meta_tokens, norm1_g, w_in, shift_mu, w0, w_up, a0, a_up, g_up, k_k, k_a, r_k, gn_w, gn_b,
           q_gain, k_gain, sinks, w_out, norm2_g, peer_query, peer_sub_keys, peer_down, peer_up):
    B, S, D = x.shape
    LP = S + BLOCK
    T = B * LP
    meta = jnp.broadcast_to(meta_tokens[None], (B, N_META, D))
    h = jnp.concatenate([jnp.zeros((B, PAD_LEFT, D), x.dtype), meta, x], axis=1)
    ht = h.reshape(T, D)

    w = w_in[0]
    n_lora = DECAY_LORA + AAA_LORA + GATE_LORA
    rwkv0 = ATTN_COLS
    lora0 = rwkv0 + 3 * RWKV_WIDTH
    w_in_b = jnp.concatenate([
        w[:, rwkv0:lora0], w[:, :ATTN_WIDTH],
        jnp.pad(w[:, lora0:lora0 + n_lora], ((0, 0), (0, LORA_PAD - n_lora))),
        w[:, ATTN_WIDTH:ATTN_COLS]], axis=1).astype(jnp.bfloat16)
    col_q = 3 * RWKV_WIDTH
    col_l = col_q + ATTN_WIDTH
    col_kv = col_l + LORA_PAD
    n_in = col_kv + 2 * KV_WIDTH
    p = norm_matmul(ht, norm1_g[0], w_in_b, 512, n_in // 2)

    y_rwkv = rwkv7_mix(p, B, LP, 0, RWKV_WIDTH, 2 * RWKV_WIDTH, col_l,
                       *rwkv_params(shift_mu[0], w0[0], w_up[0], a0[0], a_up[0], g_up[0], k_k[0], k_a[0], r_k[0],
                                    gn_w[0], gn_b[0]))
    y_attn = swa_sink_attention(p, B, LP, col_q, col_kv, q_gain[0], k_gain[0], sinks[0])
    h2 = matmul_residual(y_rwkv, y_attn, w_out[0].astype(jnp.bfloat16), ht, 512, 1024)

    pq = norm_matmul(h2, norm2_g[0], peer_query[0].astype(jnp.bfloat16), 512, 1024, jnp.bfloat16)
    keys = peer_sub_keys[0].reshape(2 * PEER_HEADS, N_KEYS, D_KEY // 2).astype(jnp.bfloat16)
    table = pack_expert_table(peer_down[0], peer_up[0])

    blocks_per_seq = LP // BLOCK
    kept_per_seq = blocks_per_seq - 1

    def kept_block(i):
        return (i // kept_per_seq) * blocks_per_seq + 1 + i % kept_per_seq

    idx_t, gate_t = peer_select(pq, keys, B * kept_per_seq, kept_block)
    seq_pos = (1 + jnp.arange(B * kept_per_seq, dtype=jnp.int32) % kept_per_seq) * BLOCK
    return peer_mix(idx_t, gate_t, h2, norm2_g[0], table, kept_block, seq_pos).reshape(B, S, D)
```

```python
import functools

import jax
import jax.numpy as jnp
from jax import lax
from jax.experimental import pallas as pl
from jax.experimental.pallas import tpu as pltpu

N_META = 16
BLOCK = 128
PAD_LEFT = BLOCK - N_META
HEAD_DIM = 64
RWKV_WIDTH = 1024
RWKV_HEADS = RWKV_WIDTH // HEAD_DIM
ATTN_WIDTH = 1024
ATTN_Q_HEADS = ATTN_WIDTH // HEAD_DIM
ATTN_KV_HEADS = 2
KV_WIDTH = ATTN_KV_HEADS * HEAD_DIM
WINDOW = 128
ATTN_SCALE = HEAD_DIM ** -0.5
MASK_VALUE = -1e30
DECAY_LORA = 64
AAA_LORA = 64
GATE_LORA = 160
RMS_EPS = 1e-6
GN_EPS = 64e-5
ATTN_COLS = ATTN_WIDTH + 2 * KV_WIDTH
PEER_HEADS = 8
N_KEYS = 128
PEER_TOPK = 16
D_KEY = 256

VMEM_LIMIT_BYTES = 48 * 1024 * 1024
MATMUL_ROWS = 512
MATMUL_COLS = 1024


def _norm_matmul_kernel(x_ref, g_ref, w_ref, o_ref):
    x = x_ref[...]
    ms = jnp.mean(x * x, axis=-1, keepdims=True)
    u = x * lax.rsqrt(ms + RMS_EPS) * g_ref[...]
    o_ref[...] = jnp.dot(
        u.astype(jnp.bfloat16), w_ref[...], preferred_element_type=jnp.float32).astype(o_ref.dtype)


def norm_matmul(x, g, w, tm, tn, out_dtype=jnp.float32):
    m, k = x.shape
    n = w.shape[1]
    return pl.pallas_call(
        _norm_matmul_kernel,
        grid=(n // tn, m // tm),
        in_specs=[
            pl.BlockSpec((tm, k), lambda j, i: (i, 0)),
            pl.BlockSpec((1, k), lambda j, i: (0, 0)),
            pl.BlockSpec((k, tn), lambda j, i: (0, j)),
        ],
        out_specs=pl.BlockSpec((tm, tn), lambda j, i: (i, j)),
        out_shape=jax.ShapeDtypeStruct((m, n), out_dtype),
        compiler_params=pltpu.CompilerParams(
            dimension_semantics=("arbitrary", "arbitrary"), vmem_limit_bytes=VMEM_LIMIT_BYTES),
        name="norm_matmul",
    )(x, g.reshape(1, k), w)


def _matmul_residual_kernel(xa_ref, xb_ref, w_ref, r_ref, o_ref):
    ka = xa_ref.shape[1]
    o_ref[...] = (r_ref[...]
                  + jnp.dot(xa_ref[...], w_ref[:ka, :], preferred_element_type=jnp.float32)
                  + jnp.dot(xb_ref[...], w_ref[ka:, :], preferred_element_type=jnp.float32))


def matmul_residual(xa, xb, w, r, tm, tn):
    m, ka = xa.shape
    kb = xb.shape[1]
    n = w.shape[1]
    return pl.pallas_call(
        _matmul_residual_kernel,
        grid=(n // tn, m // tm),
        in_specs=[
            pl.BlockSpec((tm, ka), lambda j, i: (i, 0)),
            pl.BlockSpec((tm, kb), lambda j, i: (i, 0)),
            pl.BlockSpec((ka + kb, tn), lambda j, i: (0, j)),
            pl.BlockSpec((tm, tn), lambda j, i: (i, j)),
        ],
        out_specs=pl.BlockSpec((tm, tn), lambda j, i: (i, j)),
        out_shape=jax.ShapeDtypeStruct((m, n), jnp.float32),
        compiler_params=pltpu.CompilerParams(
            dimension_semantics=("arbitrary", "arbitrary"), vmem_limit_bytes=VMEM_LIMIT_BYTES),
        name="matmul_residual",
    )(xa, xb, w, r)


RWKV_CHUNK = 64
RWKV_SEQS_PER_STEP = 2
LANES = 128
RWKV_TILES = RWKV_WIDTH // LANES
LORA_PAD = 512


def _dot(a, b, dims):
    return lax.dot_general(a, b, (dims, ((), ())), preferred_element_type=jnp.float32)


def _mm(a, b):
    return _dot(a, b, ((1,), (0,)))


def _mm_nt(a, b):
    return _dot(a, b, ((1,), (1,)))


def _mm_tn(a, b):
    return _dot(a, b, ((0,), (0,)))


def _iota2(shape, axis):
    return lax.broadcasted_iota(jnp.int32, shape, axis)


def _sigmoid(x):
    return 1.0 / (1.0 + jnp.exp(-x))


def _bf16_pieces(x, n):
    pieces = []
    for _ in range(n):
        p = x.astype(jnp.bfloat16)
        pieces.append(p)
        x = x - p.astype(jnp.float32)
    return pieces


def _token_shift(x, prev_ref, mu):
    rolled = pltpu.roll(x, 1, 0)
    prev = jnp.where(_iota2(x.shape, 0) == 0, prev_ref[...], rolled)
    prev_ref[...] = x[x.shape[0] - 1:, :]
    return x + mu * (prev - x)


def _rwkv_kernel(xr_ref, xk_ref, xv_ref, xl_ref, mu_ref, mul_ref, vec_ref, wup_ref, aup_ref, gup_ref,
                 o_ref, s_ref, pr_ref, pk_ref, pv_ref, pl_ref):
    C = RWKV_CHUNK
    NS = RWKV_SEQS_PER_STEP
    f32, bf16 = jnp.float32, jnp.bfloat16

    @pl.when(pl.program_id(1) == 0)
    def _():
        s_ref[...] = jnp.zeros_like(s_ref)
        pr_ref[...] = jnp.zeros_like(pr_ref)
        pk_ref[...] = jnp.zeros_like(pk_ref)
        pv_ref[...] = jnp.zeros_like(pv_ref)
        pl_ref[...] = jnp.zeros_like(pl_ref)

    def shifted(x_ref, prev_ref, mu):
        return jnp.concatenate([_token_shift(x_ref[i], prev_ref.at[i], mu) for i in range(NS)], axis=0)

    r = shifted(xr_ref, pr_ref, mu_ref[0:1, :])
    kraw = shifted(xk_ref, pk_ref, mu_ref[1:2, :])
    v = shifted(xv_ref, pv_ref, mu_ref[2:3, :])
    xl = shifted(xl_ref, pl_ref, mul_ref[...])

    w0, a0, k_k, k_a = vec_ref[0:1, :], vec_ref[1:2, :], vec_ref[2:3, :], vec_ref[3:4, :]
    r_k, gn_w, gn_b = vec_ref[4:5, :], vec_ref[5:6, :], vec_ref[6:7, :]

    x_wa = xl[:, :LANES]
    wl = w0 + _mm(jnp.tanh(x_wa).astype(jnp.bfloat16), wup_ref[...])
    z = -wl
    softplus = jnp.maximum(z, 0.0) + jnp.log(1.0 + jnp.exp(-jnp.abs(z)))
    lw = -jnp.exp(-softplus - 0.5)
    a = _sigmoid(a0 + _mm(x_wa.astype(jnp.bfloat16), aup_ref[...]))
    g = _mm(_sigmoid(xl[:, LANES:LANES + 256]).astype(jnp.bfloat16), gup_ref[...])

    tr, tc = _iota2((NS * C, NS * C), 0), _iota2((NS * C, NS * C), 1)
    tri = ((tr >= tc) & ((tr & -C) == (tc & -C))).astype(bf16)
    cl = sum(_mm(tri, piece) for piece in _bf16_pieces(lw, 3))
    e_pos = jnp.exp(cl)
    e_excl = jnp.exp(cl - lw)
    e_neg = 1.0 / e_pos
    e_end = [e_pos[(i + 1) * C - 1:(i + 1) * C, :] for i in range(NS)]
    e_tail = jnp.concatenate([e_end[i] * e_neg[i * C:(i + 1) * C, :] for i in range(NS)], axis=0)

    lane = _iota2((LANES, LANES), 1)
    row = _iota2((LANES, LANES), 0)
    same_head = (lane >= HEAD_DIM) == (row >= HEAD_DIM)
    seg = same_head.astype(bf16)
    eye = (lane == row).astype(f32)
    t_row, t_col = row & (C - 1), lane & (C - 1)
    strict = t_col < t_row
    in_block4 = (t_row >> 2) == (t_col >> 2)
    in_block16 = (t_row >> 4) == (t_col >> 4)
    incl = t_col <= t_row
    incl2 = jnp.concatenate([incl, incl], axis=1)
    lane_c = _iota2((C, LANES), 1)
    head0 = lane_c < HEAD_DIM

    def seg_sum(x):
        return sum(_mm(piece, seg) for piece in _bf16_pieces(x, 2))

    def by_head(x):
        return jnp.concatenate([jnp.where(head0, x, 0.0), jnp.where(head0, 0.0, x)], axis=0).astype(bf16)

    def twice(x):
        return jnp.concatenate([x, x], axis=0)

    tiles = range(NS * RWKV_TILES)
    sls = [slice((j % RWKV_TILES) * LANES, (j % RWKV_TILES + 1) * LANES) for j in tiles]
    seq = [j // RWKV_TILES for j in tiles]

    def tile(x, j):
        return x[seq[j] * C:(seq[j] + 1) * C, sls[j]]

    kr = [tile(kraw, j) for j in tiles]
    kkr = [kr[j] * k_k[:, sls[j]] for j in tiles]
    kk_ss = [seg_sum(kkr[j] * kkr[j]) for j in tiles]
    kk = [kkr[j] / jnp.maximum(jnp.sqrt(kk_ss[j]), 1e-12) for j in tiles]
    a_t = [tile(a, j) for j in tiles]
    k2 = [kr[j] * (1.0 + (a_t[j] - 1.0) * k_a[:, sls[j]]) for j in tiles]
    r_t = [tile(r, j) for j in tiles]
    v_t = [tile(v, j) for j in tiles]
    kka = [kk[j] * a_t[j] for j in tiles]
    alpha = [-kk[j] * tile(e_excl, j) for j in tiles]
    r_dec = [r_t[j] * tile(e_pos, j) for j in tiles]
    beta = [kka[j] * tile(e_neg, j) for j in tiles]
    k_neg = [k2[j] * tile(e_neg, j) for j in tiles]
    s_old = [s_ref[j] for j in tiles]

    gram = [_mm_nt(jnp.concatenate([by_head(alpha[j]), by_head(r_dec[j])], axis=0),
                   jnp.concatenate([by_head(beta[j]), by_head(k_neg[j])], axis=0)) for j in tiles]
    m1 = [_mm_nt(jnp.concatenate([alpha[j], r_dec[j]], axis=0).astype(bf16), s_old[j].astype(bf16))
          for j in tiles]
    a_s = [jnp.where(strict, gram[j][:LANES, :LANES], 0.0) for j in tiles]
    b_s = [jnp.where(strict, gram[j][:LANES, LANES:], 0.0).astype(bf16) for j in tiles]
    r_i = [jnp.where(incl2, gram[j][LANES:, :], 0.0).astype(bf16) for j in tiles]
    vv = [twice(v_t[j]).astype(bf16) for j in tiles]
    rhs = [twice(m1[j][:C]) + _mm(b_s[j], vv[j]) for j in tiles]
    def mmb(x, y):
        return _mm(x.astype(bf16), y.astype(bf16))

    def inv_index4(nil):
        sq = [mmb(nil[j], nil[j]) for j in tiles]
        return [eye + nil[j] + sq[j] + mmb(sq[j], nil[j]) for j in tiles]

    t0 = inv_index4([jnp.where(in_block4, a_s[j], 0.0) for j in tiles])
    n1 = [mmb(t0[j], jnp.where(in_block16 & ~in_block4, a_s[j], 0.0)) for j in tiles]
    p1 = inv_index4(n1)
    t1 = [mmb(p1[j], t0[j]) for j in tiles]
    n2 = [mmb(t1[j], jnp.where(in_block16, 0.0, a_s[j])) for j in tiles]
    p2 = inv_index4(n2)
    t1_rhs = [mmb(t1[j], rhs[j]) for j in tiles]
    u_st = [mmb(p2[j], t1_rhs[j]) for j in tiles]
    y_st = [twice(m1[j][C:]) + _mm(r_i[j], jnp.concatenate([u_st[j].astype(bf16), vv[j]], axis=0))
            for j in tiles]
    u_t = [jnp.where(head0, u_st[j][:C], u_st[j][C:]) for j in tiles]
    y_t = [jnp.where(head0, y_st[j][:C], y_st[j][C:]) for j in tiles]

    upd = [_mm_tn(jnp.concatenate([u_t[j], v_t[j]], axis=0).astype(bf16),
                  jnp.concatenate([kka[j] * tile(e_tail, j), k2[j] * tile(e_tail, j)], axis=0).astype(bf16))
           for j in tiles]
    for j in tiles:
        s_ref[j] = s_old[j] * e_end[seq[j]][:, sls[j]] + jnp.where(same_head, upd[j], 0.0)

    mu_y = [seg_sum(y_t[j]) * (1.0 / HEAD_DIM) for j in tiles]
    dev = [y_t[j] - mu_y[j] for j in tiles]
    var = [seg_sum(dev[j] * dev[j]) * (1.0 / HEAD_DIM) for j in tiles]
    bonus = [seg_sum(r_t[j] * k2[j] * r_k[:, sls[j]]) * v_t[j] for j in tiles]
    outs = [(dev[j] * lax.rsqrt(var[j] + GN_EPS) * gn_w[:, sls[j]] + gn_b[:, sls[j]] + bonus[j]) * tile(g, j)
            for j in tiles]
    for i in range(NS):
        o_ref[i] = jnp.concatenate(outs[i * RWKV_TILES:(i + 1) * RWKV_TILES], axis=1).astype(o_ref.dtype)


def rwkv_params(shift_mu, w0, w_up, a0, a_up, g_up, k_k, k_a, r_k, gn_w, gn_b):
    W = RWKV_WIDTH
    n_lora = DECAY_LORA + AAA_LORA + GATE_LORA
    mu_rkv = shift_mu[:3 * W].reshape(3, W)
    mu_l = jnp.pad(shift_mu[3 * W:], (0, LORA_PAD - n_lora)).reshape(1, LORA_PAD)
    vecs = jnp.stack([w0, a0, k_k, k_a, r_k.reshape(W), gn_w, gn_b, jnp.zeros_like(w0)])
    w_up_p = jnp.pad(w_up, ((0, LANES - DECAY_LORA), (0, 0))).astype(jnp.bfloat16)
    a_up_p = jnp.pad(a_up, ((DECAY_LORA, LANES - DECAY_LORA - AAA_LORA), (0, 0))).astype(jnp.bfloat16)
    g_up_p = jnp.pad(g_up, ((0, 256 - GATE_LORA), (0, 0))).astype(jnp.bfloat16)
    return mu_rkv, mu_l, vecs, w_up_p, a_up_p, g_up_p


def rwkv7_mix(p, B, LP, col_r, col_k, col_v, col_l, mu_rkv, mu_l, vecs, w_up, a_up, g_up):
    C = RWKV_CHUNK
    NS = RWKV_SEQS_PER_STEP
    nc = LP // C
    W = RWKV_WIDTH
    p3 = p.reshape(B, LP, p.shape[1])

    def col_spec(width, col):
        return pl.BlockSpec((NS, C, width), lambda b, c: (b, c, col // width))

    def full(shape):
        return pl.BlockSpec(shape, lambda b, c: (0,) * len(shape))

    def carried_row(width):
        return pltpu.VMEM((NS, 1, width), jnp.float32)

    out = pl.pallas_call(
        _rwkv_kernel,
        grid=(B // NS, nc),
        in_specs=[
            col_spec(W, col_r), col_spec(W, col_k), col_spec(W, col_v), col_spec(LORA_PAD, col_l),
            full(mu_rkv.shape), full(mu_l.shape), full(vecs.shape),
            full(w_up.shape), full(a_up.shape), full(g_up.shape),
        ],
        out_specs=pl.BlockSpec((NS, C, W), lambda b, c: (b, c, 0)),
        out_shape=jax.ShapeDtypeStruct((B, LP, W), jnp.bfloat16),
        scratch_shapes=[
            pltpu.VMEM((NS * RWKV_TILES, LANES, LANES), jnp.float32),
            carried_row(W), carried_row(W), carried_row(W), carried_row(LORA_PAD),
        ],
        compiler_params=pltpu.CompilerParams(
            dimension_semantics=("arbitrary", "arbitrary"), vmem_limit_bytes=VMEM_LIMIT_BYTES),
        name="rwkv7_mix",
    )(p3, p3, p3, p3, mu_rkv, mu_l, vecs, w_up, a_up, g_up)
    return out.reshape(B * LP, W)


PEER_PICKS = PEER_HEADS * PEER_TOPK
PEER_TOKENS_PER_STEP = 128
PEER_GROUP = 4
PEER_GROUPS_PER_STEP = PEER_TOKENS_PER_STEP // PEER_GROUP
PEER_BUFFERS = 3


PACK_ROWS_PER_STEP = 256


def _pack_kernel(down_ref, up_ref, o_ref):
    def bf16_bits_high(t):
        return lax.bitcast_convert_type(t.astype(jnp.bfloat16).astype(jnp.float32), jnp.uint32)
    words = bf16_bits_high(down_ref[...]) | (bf16_bits_high(up_ref[...]) >> 16)
    o_ref[...] = words.reshape(o_ref.shape)


def pack_expert_table(down, up):
    E, D = down.shape
    r = PACK_ROWS_PER_STEP
    return pl.pallas_call(
        _pack_kernel,
        grid=(E // r,),
        in_specs=[pl.BlockSpec((r, D), lambda i: (i, 0)), pl.BlockSpec((r, D), lambda i: (i, 0))],
        out_specs=pl.BlockSpec((r, 1, D), lambda i: (i, 0, 0)),
        out_shape=jax.ShapeDtypeStruct((E, 1, D), jnp.uint32),
        compiler_params=pltpu.CompilerParams(
            dimension_semantics=("arbitrary",), vmem_limit_bytes=VMEM_LIMIT_BYTES),
        name="pack_expert_table",
    )(down, up)


def _peer_mix_kernel(idx_ref, gate_t_ref, res_ref, g_ref, pos_ref, tab_ref, o_ref, *scratch):
    bufs, sem, x_ref = scratch[:PEER_BUFFERS], scratch[PEER_BUFFERS], scratch[PEER_BUFFERS + 1]
    G = PEER_GROUP
    gate_t = gate_t_ref[0]
    tok_lane = _iota2(gate_t.shape, 1)

    def issue_token(group, s, b):
        for j in range(PEER_PICKS):
            e = idx_ref[0, j, group * G + s]
            pltpu.make_async_copy(tab_ref.at[e], bufs[b].at[s, pl.ds(j, 1), :], sem.at[b]).start(priority=j % 2)

    def wait_group(b):
        pltpu.make_async_copy(bufs[b], bufs[b], sem.at[b]).wait()

    def mix_token(group, s, b):
        tt = group * G + s
        words = bufs[b][s]
        down = lax.bitcast_convert_type(words & jnp.uint32(0xFFFF0000), jnp.float32)
        up = lax.bitcast_convert_type(words << 16, jnp.float32)
        h = jnp.sum(down * x_ref[pl.ds(tt, 1), :], axis=1, keepdims=True)
        gate = jnp.sum(jnp.where(tok_lane == tt, gate_t, 0.0), axis=1, keepdims=True)
        w = 0.5 * h * (1.0 + lax.erf(h * (2.0 ** -0.5))) * gate
        out = res_ref[pl.ds(tt, 1), :] + jnp.sum(w * up, axis=0, keepdims=True)
        o_ref[pl.ds(tt, 1), :] = jnp.where(pos_ref[pl.program_id(0)] + tt >= PAD_LEFT, out, 0.0)

    def step(group, b, prefetch):
        wait_group(b)
        for s in range(G):
            if prefetch:
                issue_token(group + PEER_BUFFERS - 1, s, (b + PEER_BUFFERS - 1) % PEER_BUFFERS)
            mix_token(group, s, b)

    for g in range(PEER_BUFFERS - 1):
        for s in range(G):
            issue_token(g, s, g)
    res = res_ref[...]
    x_ref[...] = res * lax.rsqrt(jnp.mean(res * res, axis=1, keepdims=True) + RMS_EPS) * g_ref[...]
    n_main = (PEER_GROUPS_PER_STEP - (PEER_BUFFERS - 1)) // PEER_BUFFERS

    def rotation(it, carry):
        for b in range(PEER_BUFFERS):
            step(it * PEER_BUFFERS + b, b, True)
        return carry

    lax.fori_loop(0, n_main, rotation, 0)
    for g in range(n_main * PEER_BUFFERS, PEER_GROUPS_PER_STEP):
        step(g, g % PEER_BUFFERS, g + PEER_BUFFERS - 1 < PEER_GROUPS_PER_STEP)


def peer_mix(idx, gate_t, res, norm_g, table, src_block, seq_pos):
    D = res.shape[1]
    n = idx.shape[0]
    tb = PEER_TOKENS_PER_STEP
    picks = pl.BlockSpec((1, PEER_PICKS, tb), lambda i: (i, 0, 0))
    return pl.pallas_call(
        _peer_mix_kernel,
        grid=(n,),
        in_specs=[
            pl.BlockSpec((1, PEER_PICKS, tb), lambda i: (i, 0, 0), memory_space=pltpu.SMEM),
            picks,
            pl.BlockSpec((tb, D), lambda i: (src_block(i), 0)),
            pl.BlockSpec((1, D), lambda i: (0, 0)),
            pl.BlockSpec(memory_space=pltpu.SMEM),
            pl.BlockSpec(memory_space=pl.ANY),
        ],
        out_specs=pl.BlockSpec((tb, D), lambda i: (i, 0)),
        out_shape=jax.ShapeDtypeStruct((n * tb, D), jnp.float32),
        scratch_shapes=[pltpu.VMEM((PEER_GROUP, PEER_PICKS, D), jnp.uint32) for _ in range(PEER_BUFFERS)]
        + [pltpu.SemaphoreType.DMA((PEER_BUFFERS,)), pltpu.VMEM((tb, D), jnp.float32)],
        compiler_params=pltpu.CompilerParams(
            dimension_semantics=("arbitrary",), vmem_limit_bytes=VMEM_LIMIT_BYTES),
        name="peer_mix",
    )(idx, gate_t, res, norm_g.reshape(1, D), seq_pos, table)


def _top_rows(s, k, codes):
    vals, poss = [], []
    for _ in range(k):
        m = jnp.max(s, axis=0, keepdims=True)
        pos = jnp.min(jnp.where(s == m, codes, jnp.inf), axis=0, keepdims=True)
        vals.append(m)
        poss.append(pos)
        s = jnp.where(codes == pos, -jnp.inf, s)
    return jnp.concatenate(vals, axis=0), jnp.concatenate(poss, axis=0)


def _candidate_pairs(k):
    return [(i, j) for i in range(k) for j in range(k) if (i + 1) * (j + 1) <= k]


def candidate_selectors():
    pairs = _candidate_pairs(PEER_TOPK)
    n_rows = -(-len(pairs) // 8) * 8
    first = jnp.zeros((n_rows, PEER_TOPK), jnp.float32).at[
        jnp.arange(len(pairs)), jnp.array([i for i, _ in pairs])].set(1.0)
    second = jnp.zeros((n_rows, PEER_TOPK), jnp.float32).at[
        jnp.arange(len(pairs)), jnp.array([j for _, j in pairs])].set(1.0)
    return first.astype(jnp.bfloat16), second.astype(jnp.bfloat16)


def _pick_rows(table, sel):
    out = jnp.zeros_like(sel)
    for r in range(table.shape[0]):
        out = out + jnp.where(sel == float(r), table[r:r + 1, :], 0.0)
    return out


def _peer_select_kernel(q_ref, keys_ref, sel1_ref, sel2_ref, idx_ref, gate_ref):
    K = PEER_TOPK
    half_w = D_KEY // 2
    tokens = q_ref.shape[0]
    key_rows = _iota2((N_KEYS, tokens), 0).astype(jnp.float32)
    sel1, sel2 = sel1_ref[...], sel2_ref[...]
    n_cand = len(_candidate_pairs(K))
    rank = _iota2((K, tokens), 0).astype(jnp.bfloat16)
    cand_rows = _iota2((sel1.shape[0], tokens), 0)
    cand_code = jnp.where(cand_rows < n_cand, _mm(sel1, rank) * K + _mm(sel2, rank), float(K * K))

    def pick_exact(sel, x):
        return sum(_mm(sel, piece) for piece in _bf16_pieces(x, 3))

    idx_rows, gate_rows = [], []
    for h in range(PEER_HEADS):
        tops = []
        for c in range(2):
            hc = 2 * h + c
            scores = _mm_nt(keys_ref[hc], q_ref[:, hc * half_w:(hc + 1) * half_w])
            tops.append(_top_rows(scores, K, key_rows))
        (s1, i1), (s2, i2) = tops
        cand = jnp.where(cand_rows < n_cand, pick_exact(sel1, s1) + pick_exact(sel2, s2), -jnp.inf)
        top_s, pos = _top_rows(cand, K, cand_code)
        first = jnp.floor(pos * (1.0 / K))
        second = pos - first * K
        expert = _pick_rows(i1, first) * N_KEYS + _pick_rows(i2, second)
        e = jnp.exp(top_s - top_s[0:1, :])
        idx_rows.append(expert.astype(jnp.int32))
        gate_rows.append(e / jnp.sum(e, axis=0, keepdims=True))
    idx_ref[0] = jnp.concatenate(idx_rows, axis=0)
    gate_ref[0] = jnp.concatenate(gate_rows, axis=0)


def peer_select(q, keys, n, src_block):
    W = q.shape[1]
    tb = PEER_TOKENS_PER_STEP
    out_block = pl.BlockSpec((1, PEER_PICKS, tb), lambda i: (i, 0, 0))
    sel1, sel2 = candidate_selectors()
    return pl.pallas_call(
        _peer_select_kernel,
        grid=(n,),
        in_specs=[pl.BlockSpec((tb, W), lambda i: (src_block(i), 0)),
                  pl.BlockSpec(keys.shape, lambda i: (0, 0, 0)),
                  pl.BlockSpec(sel1.shape, lambda i: (0, 0)),
                  pl.BlockSpec(sel2.shape, lambda i: (0, 0))],
        out_specs=[out_block, out_block],
        out_shape=[jax.ShapeDtypeStruct((n, PEER_PICKS, tb), jnp.int32),
                   jax.ShapeDtypeStruct((n, PEER_PICKS, tb), jnp.float32)],
        compiler_params=pltpu.CompilerParams(
            dimension_semantics=("arbitrary",), vmem_limit_bytes=VMEM_LIMIT_BYTES),
        name="peer_select",
    )(q, keys, sel1, sel2)


ATTN_TILES = ATTN_WIDTH // LANES
Q_HEADS_PER_KV = ATTN_Q_HEADS // ATTN_KV_HEADS


def _attn_kernel(q_ref, kvp_ref, kvc_ref, qg_ref, kg_ref, sink_ref, o_ref):
    f32, bf16 = jnp.float32, jnp.bfloat16
    nb = pl.program_id(1)
    lane = _iota2((LANES, LANES), 1)
    row = _iota2((LANES, LANES), 0)
    seg = ((lane >= HEAD_DIM) == (row >= HEAD_DIM)).astype(bf16)

    def head_rms(x, gain):
        ms = sum(_mm(piece, seg) for piece in _bf16_pieces(x * x, 2)) * (1.0 / HEAD_DIM)
        return x * lax.rsqrt(ms + RMS_EPS) * gain

    kv = jnp.concatenate([kvp_ref[...], kvc_ref[...]], axis=0)
    k = head_rms(kv[:, :LANES], kg_ref[...])
    v = kv[:, LANES:]
    lane_k = _iota2(k.shape, 1)
    k_sw, v_sw = pltpu.roll(k, HEAD_DIM, 1), pltpu.roll(v, HEAD_DIM, 1)
    k_dup = [jnp.where(lane_k < HEAD_DIM, k, k_sw).astype(bf16), jnp.where(lane_k < HEAD_DIM, k_sw, k).astype(bf16)]
    v_dup = [jnp.where(lane_k < HEAD_DIM, v, v_sw).astype(bf16), jnp.where(lane_k < HEAD_DIM, v_sw, v).astype(bf16)]

    qi = _iota2((2 * BLOCK, 2 * BLOCK), 0) & (BLOCK - 1)
    ks = _iota2((2 * BLOCK, 2 * BLOCK), 1)
    dist = qi + BLOCK - ks
    key_pos = (nb - 1) * BLOCK + ks
    allowed = (dist >= 0) & (dist < WINDOW) & (key_pos >= PAD_LEFT)
    upper = _iota2((2 * BLOCK, 1), 0) < BLOCK
    head0 = _iota2((BLOCK, LANES), 1) < HEAD_DIM

    tiles = range(ATTN_TILES)
    q_t = [head_rms(q_ref[:, t * LANES:(t + 1) * LANES], qg_ref[...]) for t in tiles]
    q2 = [jnp.concatenate([jnp.where(head0, q_t[t], 0.0), jnp.where(head0, 0.0, q_t[t])], axis=0).astype(bf16)
          for t in tiles]
    s = [_mm_nt(q2[t], k_dup[(2 * t) // Q_HEADS_PER_KV]) * ATTN_SCALE for t in tiles]
    s = [jnp.where(allowed, s[t], MASK_VALUE) for t in tiles]
    sink = [jnp.where(upper, sink_ref[2 * t], sink_ref[2 * t + 1]) for t in tiles]
    m = [jnp.maximum(jnp.max(s[t], axis=1, keepdims=True), sink[t]) for t in tiles]
    e = [jnp.exp(s[t] - m[t]) for t in tiles]
    denom = [jnp.sum(e[t], axis=1, keepdims=True) + jnp.exp(sink[t] - m[t]) for t in tiles]
    p = [(e[t] / denom[t]).astype(bf16) for t in tiles]
    o2 = [_mm(p[t], v_dup[(2 * t) // Q_HEADS_PER_KV]) for t in tiles]
    o_ref[...] = jnp.concatenate(
        [jnp.where(head0, o2[t][:BLOCK], o2[t][BLOCK:]) for t in tiles], axis=1).astype(o_ref.dtype)


def swa_sink_attention(p, B, LP, col_q, col_kv, q_gain, k_gain, sinks):
    nblk = LP // BLOCK
    kvw = 2 * KV_WIDTH
    qg = jnp.tile(q_gain, LANES // HEAD_DIM).reshape(1, LANES)
    kg = jnp.tile(k_gain, LANES // HEAD_DIM).reshape(1, LANES)
    return pl.pallas_call(
        _attn_kernel,
        grid=(B, nblk),
        in_specs=[
            pl.BlockSpec((BLOCK, ATTN_WIDTH), lambda b, n: (b * nblk + n, col_q // ATTN_WIDTH)),
            pl.BlockSpec((BLOCK, kvw), lambda b, n: (b * nblk + jnp.maximum(n - 1, 0), col_kv // kvw)),
            pl.BlockSpec((BLOCK, kvw), lambda b, n: (b * nblk + n, col_kv // kvw)),
            pl.BlockSpec((1, LANES), lambda b, n: (0, 0)),
            pl.BlockSpec((1, LANES), lambda b, n: (0, 0)),
            pl.BlockSpec(memory_space=pltpu.SMEM),
        ],
        out_specs=pl.BlockSpec((BLOCK, ATTN_WIDTH), lambda b, n: (b * nblk + n, 0)),
        out_shape=jax.ShapeDtypeStruct((B * LP, ATTN_WIDTH), jnp.bfloat16),
        compiler_params=pltpu.CompilerParams(
            dimension_semantics=("arbitrary", "arbitrary"), vmem_limit_bytes=VMEM_LIMIT_BYTES),
        name="swa_sink_attention",
    )(p, p, p, qg, kg, sinks)


def kernel(x, meta_tokens, norm1_g, w_in, shift_mu, w0, w_up, a0, a_up, g_up, k_k, k_a, r_k, gn_w, gn_b,
           q_gain, k_gain, sinks, w_out, norm2_g, peer_query, peer_sub_keys, peer_down, peer_up):
    B, S, D = x.shape
    LP = S + BLOCK
    T = B * LP
    meta = jnp.broadcast_to(meta_tokens[None], (B, N_META, D))
    h = jnp.concatenate([jnp.zeros((B, PAD_LEFT, D), x.dtype), meta, x], axis=1)
    ht = h.reshape(T, D)

    w = w_in[0]
    n_lora = DECAY_LORA + AAA_LORA + GATE_LORA
    rwkv0 = ATTN_COLS
    lora0 = rwkv0 + 3 * RWKV_WIDTH
    w_in_b = jnp.concatenate([
        w[:, rwkv0:lora0], w[:, :ATTN_WIDTH],
        jnp.pad(w[:, lora0:lora0 + n_lora], ((0, 0), (0, LORA_PAD - n_lora))),
        w[:, ATTN_WIDTH:ATTN_COLS]], axis=1).astype(jnp.bfloat16)
    col_q = 3 * RWKV_WIDTH
    col_l = col_q + ATTN_WIDTH
    col_kv = col_l + LORA_PAD
    n_in = col_kv + 2 * KV_WIDTH
    p = norm_matmul(ht, norm1_g[0], w_in_b, MATMUL_ROWS, n_in // 2)

    y_rwkv = rwkv7_mix(p, B, LP, 0, RWKV_WIDTH, 2 * RWKV_WIDTH, col_l,
                       *rwkv_params(shift_mu[0], w0[0], w_up[0], a0[0], a_up[0], g_up[0], k_k[0], k_a[0], r_k[0],
                                    gn_w[0], gn_b[0]))
    y_attn = swa_sink_attention(p, B, LP, col_q, col_kv, q_gain[0], k_gain[0], sinks[0])
    h2 = matmul_residual(y_rwkv, y_attn, w_out[0].astype(jnp.bfloat16), ht, MATMUL_ROWS, MATMUL_COLS)

    pq = norm_matmul(h2, norm2_g[0], peer_query[0].astype(jnp.bfloat16), MATMUL_ROWS, MATMUL_COLS, jnp.bfloat16)
    keys = peer_sub_keys[0].reshape(2 * PEER_HEADS, N_KEYS, D_KEY // 2).astype(jnp.bfloat16)
    table = pack_expert_table(peer_down[0], peer_up[0])

    blocks_per_seq = LP // BLOCK
    kept_per_seq = blocks_per_seq - 1

    def kept_block(i):
        return (i // kept_per_seq) * blocks_per_seq + 1 + i % kept_per_seq

    idx_t, gate_t = peer_select(pq, keys, B * kept_per_seq, kept_block)
    seq_pos = (1 + jnp.arange(B * kept_per_seq, dtype=jnp.int32) % kept_per_seq) * BLOCK
    return peer_mix(idx_t, gate_t, h2, norm2_g[0], table, kept_block, seq_pos).reshape(B, S, D)
```

```python
import jax
import jax.numpy as jnp
from jax import lax
from jax.experimental import pallas as pl
from jax.experimental.pallas import tpu as pltpu

N_META = 16
BLOCK = 128
PAD_LEFT = BLOCK - N_META
HEAD_DIM = 64
RWKV_WIDTH = 1024
ATTN_WIDTH = 1024
ATTN_Q_HEADS = ATTN_WIDTH // HEAD_DIM
ATTN_KV_HEADS = 2
KV_WIDTH = ATTN_KV_HEADS * HEAD_DIM
WINDOW = 128
ATTN_SCALE = HEAD_DIM ** -0.5
MASK_VALUE = -1e30
DECAY_LORA = 64
AAA_LORA = 64
GATE_LORA = 160
RMS_EPS = 1e-6
GN_EPS = 64e-5
ATTN_COLS = ATTN_WIDTH + 2 * KV_WIDTH
PEER_HEADS = 8
N_KEYS = 128
PEER_TOPK = 16
D_KEY = 256

VMEM_LIMIT_BYTES = 48 * 1024 * 1024
MATMUL_ROWS = 512
MATMUL_COLS = 1024


def _norm_matmul_kernel(x_ref, g_ref, w_ref, o_ref):
    x = x_ref[...]
    ms = jnp.mean(x * x, axis=-1, keepdims=True)
    u = x * lax.rsqrt(ms + RMS_EPS) * g_ref[...]
    o_ref[...] = jnp.dot(
        u.astype(jnp.bfloat16), w_ref[...], preferred_element_type=jnp.float32).astype(o_ref.dtype)


def norm_matmul(x, g, w, tm, tn, out_dtype=jnp.float32):
    m, k = x.shape
    n = w.shape[1]
    return pl.pallas_call(
        _norm_matmul_kernel,
        grid=(n // tn, m // tm),
        in_specs=[
            pl.BlockSpec((tm, k), lambda j, i: (i, 0)),
            pl.BlockSpec((1, k), lambda j, i: (0, 0)),
            pl.BlockSpec((k, tn), lambda j, i: (0, j)),
        ],
        out_specs=pl.BlockSpec((tm, tn), lambda j, i: (i, j)),
        out_shape=jax.ShapeDtypeStruct((m, n), out_dtype),
        compiler_params=pltpu.CompilerParams(
            dimension_semantics=("arbitrary", "arbitrary"), vmem_limit_bytes=VMEM_LIMIT_BYTES),
        name="norm_matmul",
    )(x, g.reshape(1, k), w)


def _matmul_residual_kernel(xa_ref, xb_ref, w_ref, r_ref, o_ref):
    ka = xa_ref.shape[1]
    o_ref[...] = (r_ref[...]
                  + jnp.dot(xa_ref[...], w_ref[:ka, :], preferred_element_type=jnp.float32)
                  + jnp.dot(xb_ref[...], w_ref[ka:, :], preferred_element_type=jnp.float32))


def matmul_residual(xa, xb, w, r, tm, tn):
    m, ka = xa.shape
    kb = xb.shape[1]
    n = w.shape[1]
    return pl.pallas_call(
        _matmul_residual_kernel,
        grid=(n // tn, m // tm),
        in_specs=[
            pl.BlockSpec((tm, ka), lambda j, i: (i, 0)),
            pl.BlockSpec((tm, kb), lambda j, i: (i, 0)),
            pl.BlockSpec((ka + kb, tn), lambda j, i: (0, j)),
            pl.BlockSpec((tm, tn), lambda j, i: (i, j)),
        ],
        out_specs=pl.BlockSpec((tm, tn), lambda j, i: (i, j)),
        out_shape=jax.ShapeDtypeStruct((m, n), jnp.float32),
        compiler_params=pltpu.CompilerParams(
            dimension_semantics=("arbitrary", "arbitrary"), vmem_limit_bytes=VMEM_LIMIT_BYTES),
        name="matmul_residual",
    )(xa, xb, w, r)


RWKV_CHUNK = 64
RWKV_SEQS_PER_STEP = 2
LANES = 128
RWKV_TILES = RWKV_WIDTH // LANES
LORA_PAD = 512
GATE_PAD = 2 * LANES


def _dot(a, b, dims):
    return lax.dot_general(a, b, (dims, ((), ())), preferred_element_type=jnp.float32)


def _mm(a, b):
    return _dot(a, b, ((1,), (0,)))


def _mm_nt(a, b):
    return _dot(a, b, ((1,), (1,)))


def _mm_tn(a, b):
    return _dot(a, b, ((0,), (0,)))


def _iota2(shape, axis):
    return lax.broadcasted_iota(jnp.int32, shape, axis)


def _sigmoid(x):
    return 1.0 / (1.0 + jnp.exp(-x))


def _bf16_pieces(x, n):
    pieces = []
    for _ in range(n):
        p = x.astype(jnp.bfloat16)
        pieces.append(p)
        x = x - p.astype(jnp.float32)
    return pieces


def _token_shift(x, prev_ref, mu):
    rolled = pltpu.roll(x, 1, 0)
    prev = jnp.where(_iota2(x.shape, 0) == 0, prev_ref[...], rolled)
    prev_ref[...] = x[x.shape[0] - 1:, :]
    return x + mu * (prev - x)


def _rwkv_kernel(xr_ref, xk_ref, xv_ref, xl_ref, mu_ref, mul_ref, vec_ref, wup_ref, aup_ref, gup_ref,
                 o_ref, s_ref, pr_ref, pk_ref, pv_ref, pl_ref):
    C = RWKV_CHUNK
    NS = RWKV_SEQS_PER_STEP
    f32, bf16 = jnp.float32, jnp.bfloat16

    @pl.when(pl.program_id(1) == 0)
    def _():
        s_ref[...] = jnp.zeros_like(s_ref)
        pr_ref[...] = jnp.zeros_like(pr_ref)
        pk_ref[...] = jnp.zeros_like(pk_ref)
        pv_ref[...] = jnp.zeros_like(pv_ref)
        pl_ref[...] = jnp.zeros_like(pl_ref)

    def shifted(x_ref, prev_ref, mu):
        return jnp.concatenate([_token_shift(x_ref[i], prev_ref.at[i], mu) for i in range(NS)], axis=0)

    r = shifted(xr_ref, pr_ref, mu_ref[0:1, :])
    kraw = shifted(xk_ref, pk_ref, mu_ref[1:2, :])
    v = shifted(xv_ref, pv_ref, mu_ref[2:3, :])
    xl = shifted(xl_ref, pl_ref, mul_ref[...])

    w0, a0, k_k, k_a = vec_ref[0:1, :], vec_ref[1:2, :], vec_ref[2:3, :], vec_ref[3:4, :]
    r_k, gn_w, gn_b = vec_ref[4:5, :], vec_ref[5:6, :], vec_ref[6:7, :]

    x_wa = xl[:, :LANES]
    wl = w0 + _mm(jnp.tanh(x_wa).astype(jnp.bfloat16), wup_ref[...])
    z = -wl
    softplus = jnp.maximum(z, 0.0) + jnp.log(1.0 + jnp.exp(-jnp.abs(z)))
    lw = -jnp.exp(-softplus - 0.5)
    a = _sigmoid(a0 + _mm(x_wa.astype(jnp.bfloat16), aup_ref[...]))
    g = _mm(_sigmoid(xl[:, LANES:LANES + GATE_PAD]).astype(jnp.bfloat16), gup_ref[...])

    tr, tc = _iota2((NS * C, NS * C), 0), _iota2((NS * C, NS * C), 1)
    tri = ((tr >= tc) & ((tr & -C) == (tc & -C))).astype(bf16)
    cl = sum(_mm(tri, piece) for piece in _bf16_pieces(lw, 3))
    e_pos = jnp.exp(cl)
    e_excl = jnp.exp(cl - lw)
    e_neg = 1.0 / e_pos
    e_end = [e_pos[(i + 1) * C - 1:(i + 1) * C, :] for i in range(NS)]
    e_tail = jnp.concatenate([e_end[i] * e_neg[i * C:(i + 1) * C, :] for i in range(NS)], axis=0)

    lane = _iota2((LANES, LANES), 1)
    row = _iota2((LANES, LANES), 0)
    same_head = (lane >= HEAD_DIM) == (row >= HEAD_DIM)
    seg = same_head.astype(bf16)
    eye = (lane == row).astype(f32)
    t_row, t_col = row & (C - 1), lane & (C - 1)
    strict = t_col < t_row
    in_block4 = (t_row >> 2) == (t_col >> 2)
    in_block16 = (t_row >> 4) == (t_col >> 4)
    incl = t_col <= t_row
    incl2 = jnp.concatenate([incl, incl], axis=1)
    lane_c = _iota2((C, LANES), 1)
    head0 = lane_c < HEAD_DIM

    def seg_sum(x):
        return sum(_mm(piece, seg) for piece in _bf16_pieces(x, 2))

    def by_head(x):
        return jnp.concatenate([jnp.where(head0, x, 0.0), jnp.where(head0, 0.0, x)], axis=0).astype(bf16)

    def twice(x):
        return jnp.concatenate([x, x], axis=0)

    tiles = range(NS * RWKV_TILES)
    sls = [slice((j % RWKV_TILES) * LANES, (j % RWKV_TILES + 1) * LANES) for j in tiles]
    seq = [j // RWKV_TILES for j in tiles]

    def tile(x, j):
        return x[seq[j] * C:(seq[j] + 1) * C, sls[j]]

    kr = [tile(kraw, j) for j in tiles]
    kkr = [kr[j] * k_k[:, sls[j]] for j in tiles]
    kk_ss = [seg_sum(kkr[j] * kkr[j]) for j in tiles]
    kk = [kkr[j] / jnp.maximum(jnp.sqrt(kk_ss[j]), 1e-12) for j in tiles]
    a_t = [tile(a, j) for j in tiles]
    k2 = [kr[j] * (1.0 + (a_t[j] - 1.0) * k_a[:, sls[j]]) for j in tiles]
    r_t = [tile(r, j) for j in tiles]
    v_t = [tile(v, j) for j in tiles]
    kka = [kk[j] * a_t[j] for j in tiles]
    alpha = [-kk[j] * tile(e_excl, j) for j in tiles]
    r_dec = [r_t[j] * tile(e_pos, j) for j in tiles]
    beta = [kka[j] * tile(e_neg, j) for j in tiles]
    k_neg = [k2[j] * tile(e_neg, j) for j in tiles]
    s_old = [s_ref[j] for j in tiles]

    gram = [_mm_nt(jnp.concatenate([by_head(alpha[j]), by_head(r_dec[j])], axis=0),
                   jnp.concatenate([by_head(beta[j]), by_head(k_neg[j])], axis=0)) for j in tiles]
    m1 = [_mm_nt(jnp.concatenate([alpha[j], r_dec[j]], axis=0).astype(bf16), s_old[j].astype(bf16))
          for j in tiles]
    a_s = [jnp.where(strict, gram[j][:LANES, :LANES], 0.0) for j in tiles]
    b_s = [jnp.where(strict, gram[j][:LANES, LANES:], 0.0).astype(bf16) for j in tiles]
    r_i = [jnp.where(incl2, gram[j][LANES:, :], 0.0).astype(bf16) for j in tiles]
    vv = [twice(v_t[j]).astype(bf16) for j in tiles]
    rhs = [twice(m1[j][:C]) + _mm(b_s[j], vv[j]) for j in tiles]
    def mmb(x, y):
        return _mm(x.astype(bf16), y.astype(bf16))

    def inv_index4(nil):
        sq = [mmb(nil[j], nil[j]) for j in tiles]
        return [eye + nil[j] + sq[j] + mmb(sq[j], nil[j]) for j in tiles]

    t0 = inv_index4([jnp.where(in_block4, a_s[j], 0.0) for j in tiles])
    n1 = [mmb(t0[j], jnp.where(in_block16 & ~in_block4, a_s[j], 0.0)) for j in tiles]
    p1 = inv_index4(n1)
    t1 = [mmb(p1[j], t0[j]) for j in tiles]
    n2 = [mmb(t1[j], jnp.where(in_block16, 0.0, a_s[j])) for j in tiles]
    p2 = inv_index4(n2)
    t1_rhs = [mmb(t1[j], rhs[j]) for j in tiles]
    u_st = [mmb(p2[j], t1_rhs[j]) for j in tiles]
    y_st = [twice(m1[j][C:]) + _mm(r_i[j], jnp.concatenate([u_st[j].astype(bf16), vv[j]], axis=0))
            for j in tiles]
    u_t = [jnp.where(head0, u_st[j][:C], u_st[j][C:]) for j in tiles]
    y_t = [jnp.where(head0, y_st[j][:C], y_st[j][C:]) for j in tiles]

    upd = [_mm_tn(jnp.concatenate([u_t[j], v_t[j]], axis=0).astype(bf16),
                  jnp.concatenate([kka[j] * tile(e_tail, j), k2[j] * tile(e_tail, j)], axis=0).astype(bf16))
           for j in tiles]
    for j in tiles:
        s_ref[j] = s_old[j] * e_end[seq[j]][:, sls[j]] + jnp.where(same_head, upd[j], 0.0)

    mu_y = [seg_sum(y_t[j]) * (1.0 / HEAD_DIM) for j in tiles]
    dev = [y_t[j] - mu_y[j] for j in tiles]
    var = [seg_sum(dev[j] * dev[j]) * (1.0 / HEAD_DIM) for j in tiles]
    bonus = [seg_sum(r_t[j] * k2[j] * r_k[:, sls[j]]) * v_t[j] for j in tiles]
    outs = [(dev[j] * lax.rsqrt(var[j] + GN_EPS) * gn_w[:, sls[j]] + gn_b[:, sls[j]] + bonus[j]) * tile(g, j)
            for j in tiles]
    for i in range(NS):
        o_ref[i] = jnp.concatenate(outs[i * RWKV_TILES:(i + 1) * RWKV_TILES], axis=1).astype(o_ref.dtype)


def rwkv_params(shift_mu, w0, w_up, a0, a_up, g_up, k_k, k_a, r_k, gn_w, gn_b):
    W = RWKV_WIDTH
    n_lora = DECAY_LORA + AAA_LORA + GATE_LORA
    mu_rkv = shift_mu[:3 * W].reshape(3, W)
    mu_l = jnp.pad(shift_mu[3 * W:], (0, LORA_PAD - n_lora)).reshape(1, LORA_PAD)
    vecs = jnp.stack([w0, a0, k_k, k_a, r_k.reshape(W), gn_w, gn_b, jnp.zeros_like(w0)])
    w_up_p = jnp.pad(w_up, ((0, LANES - DECAY_LORA), (0, 0))).astype(jnp.bfloat16)
    a_up_p = jnp.pad(a_up, ((DECAY_LORA, LANES - DECAY_LORA - AAA_LORA), (0, 0))).astype(jnp.bfloat16)
    g_up_p = jnp.pad(g_up, ((0, GATE_PAD - GATE_LORA), (0, 0))).astype(jnp.bfloat16)
    return mu_rkv, mu_l, vecs, w_up_p, a_up_p, g_up_p


def rwkv7_mix(p, B, LP, col_r, col_k, col_v, col_l, mu_rkv, mu_l, vecs, w_up, a_up, g_up):
    C = RWKV_CHUNK
    NS = RWKV_SEQS_PER_STEP
    assert B % NS == 0 and LP % C == 0, (B, LP)
    nc = LP // C
    W = RWKV_WIDTH
    p3 = p.reshape(B, LP, p.shape[1])

    def col_spec(width, col):
        return pl.BlockSpec((NS, C, width), lambda b, c: (b, c, col // width))

    def full(shape):
        return pl.BlockSpec(shape, lambda b, c: (0,) * len(shape))

    def carried_row(width):
        return pltpu.VMEM((NS, 1, width), jnp.float32)

    out = pl.pallas_call(
        _rwkv_kernel,
        grid=(B // NS, nc),
        in_specs=[
            col_spec(W, col_r), col_spec(W, col_k), col_spec(W, col_v), col_spec(LORA_PAD, col_l),
            full(mu_rkv.shape), full(mu_l.shape), full(vecs.shape),
            full(w_up.shape), full(a_up.shape), full(g_up.shape),
        ],
        out_specs=pl.BlockSpec((NS, C, W), lambda b, c: (b, c, 0)),
        out_shape=jax.ShapeDtypeStruct((B, LP, W), jnp.bfloat16),
        scratch_shapes=[
            pltpu.VMEM((NS * RWKV_TILES, LANES, LANES), jnp.float32),
            carried_row(W), carried_row(W), carried_row(W), carried_row(LORA_PAD),
        ],
        compiler_params=pltpu.CompilerParams(
            dimension_semantics=("arbitrary", "arbitrary"), vmem_limit_bytes=VMEM_LIMIT_BYTES),
        name="rwkv7_mix",
    )(p3, p3, p3, p3, mu_rkv, mu_l, vecs, w_up, a_up, g_up)
    return out.reshape(B * LP, W)


PEER_PICKS = PEER_HEADS * PEER_TOPK
PEER_TOKENS_PER_STEP = 128
PEER_GROUP = 4
PEER_GROUPS_PER_STEP = PEER_TOKENS_PER_STEP // PEER_GROUP
PEER_BUFFERS = 4
assert PEER_GROUPS_PER_STEP % PEER_BUFFERS == 0


PACK_ROWS_PER_STEP = 256


def _pack_kernel(down_ref, up_ref, o_ref):
    def bf16_bits_high(t):
        return lax.bitcast_convert_type(t.astype(jnp.bfloat16).astype(jnp.float32), jnp.uint32)
    words = bf16_bits_high(down_ref[...]) | (bf16_bits_high(up_ref[...]) >> 16)
    o_ref[...] = words.reshape(o_ref.shape)


def pack_expert_table(down, up):
    E, D = down.shape
    r = PACK_ROWS_PER_STEP
    return pl.pallas_call(
        _pack_kernel,
        grid=(E // r,),
        in_specs=[pl.BlockSpec((r, D), lambda i: (i, 0)), pl.BlockSpec((r, D), lambda i: (i, 0))],
        out_specs=pl.BlockSpec((r, 1, D), lambda i: (i, 0, 0)),
        out_shape=jax.ShapeDtypeStruct((E, 1, D), jnp.uint32),
        compiler_params=pltpu.CompilerParams(
            dimension_semantics=("arbitrary",), vmem_limit_bytes=VMEM_LIMIT_BYTES),
        name="pack_expert_table",
    )(down, up)


def _peer_mix_kernel(idx_ref, idx_next_ref, gate_t_ref, res_ref, g_ref, pos_ref, tab_ref, o_ref, *scratch):
    bufs, sem, x_ref = scratch[:PEER_BUFFERS], scratch[PEER_BUFFERS], scratch[PEER_BUFFERS + 1]
    G = PEER_GROUP
    ahead = PEER_BUFFERS - 1
    gate_t = gate_t_ref[0]
    tok_lane = _iota2(gate_t.shape, 1)
    block = pl.program_id(0)

    def issue_token(picks_ref, group, s, b):
        for j in range(PEER_PICKS):
            e = picks_ref[0, j, group * G + s]
            pltpu.make_async_copy(tab_ref.at[e], bufs[b].at[s, pl.ds(j, 1), :], sem.at[b]).start(priority=j % 2)

    def wait_group(b):
        pltpu.make_async_copy(bufs[b], bufs[b], sem.at[b]).wait()

    def mix_token(group, s, b):
        tt = group * G + s
        words = bufs[b][s]
        down = lax.bitcast_convert_type(words & jnp.uint32(0xFFFF0000), jnp.float32)
        up = lax.bitcast_convert_type(words << 16, jnp.float32)
        h = jnp.sum(down * x_ref[pl.ds(tt, 1), :], axis=1, keepdims=True)
        gate = jnp.sum(jnp.where(tok_lane == tt, gate_t, 0.0), axis=1, keepdims=True)
        w = 0.5 * h * (1.0 + lax.erf(h * (2.0 ** -0.5))) * gate
        out = res_ref[pl.ds(tt, 1), :] + jnp.sum(w * up, axis=0, keepdims=True)
        o_ref[pl.ds(tt, 1), :] = jnp.where(pos_ref[block] + tt >= PAD_LEFT, out, 0.0)

    def step(group, b):
        wait_group(b)
        target = (b + ahead) % PEER_BUFFERS
        for s in range(G):
            if isinstance(group, int) and group + ahead >= PEER_GROUPS_PER_STEP:
                @pl.when(block + 1 < pl.num_programs(0))
                def _():
                    issue_token(idx_next_ref, group + ahead - PEER_GROUPS_PER_STEP, s, target)
            else:
                issue_token(idx_ref, group + ahead, s, target)
            mix_token(group, s, b)

    @pl.when(block == 0)
    def _():
        for g in range(ahead):
            for s in range(G):
                issue_token(idx_ref, g, s, g)

    res = res_ref[...]
    x_ref[...] = res * lax.rsqrt(jnp.mean(res * res, axis=1, keepdims=True) + RMS_EPS) * g_ref[...]
    n_main = (PEER_GROUPS_PER_STEP - ahead) // PEER_BUFFERS

    def rotation(it, carry):
        for b in range(PEER_BUFFERS):
            step(it * PEER_BUFFERS + b, b)
        return carry

    lax.fori_loop(0, n_main, rotation, 0)
    for g in range(n_main * PEER_BUFFERS, PEER_GROUPS_PER_STEP):
        step(g, g % PEER_BUFFERS)


def peer_mix(idx, gate_t, res, norm_g, table, src_block, seq_pos):
    D = res.shape[1]
    n = idx.shape[0]
    tb = PEER_TOKENS_PER_STEP
    picks = pl.BlockSpec((1, PEER_PICKS, tb), lambda i: (i, 0, 0))
    return pl.pallas_call(
        _peer_mix_kernel,
        grid=(n,),
        in_specs=[
            pl.BlockSpec((1, PEER_PICKS, tb), lambda i: (i, 0, 0), memory_space=pltpu.SMEM),
            pl.BlockSpec((1, PEER_PICKS, tb), lambda i: (jnp.minimum(i + 1, n - 1), 0, 0),
                         memory_space=pltpu.SMEM),
            picks,
            pl.BlockSpec((tb, D), lambda i: (src_block(i), 0)),
            pl.BlockSpec((1, D), lambda i: (0, 0)),
            pl.BlockSpec(memory_space=pltpu.SMEM),
            pl.BlockSpec(memory_space=pl.ANY),
        ],
        out_specs=pl.BlockSpec((tb, D), lambda i: (i, 0)),
        out_shape=jax.ShapeDtypeStruct((n * tb, D), jnp.float32),
        scratch_shapes=[pltpu.VMEM((PEER_GROUP, PEER_PICKS, D), jnp.uint32) for _ in range(PEER_BUFFERS)]
        + [pltpu.SemaphoreType.DMA((PEER_BUFFERS,)), pltpu.VMEM((tb, D), jnp.float32)],
        compiler_params=pltpu.CompilerParams(
            dimension_semantics=("arbitrary",), vmem_limit_bytes=VMEM_LIMIT_BYTES),
        name="peer_mix",
    )(idx, idx, gate_t, res, norm_g.reshape(1, D), seq_pos, table)


def _top_rows(s, k, codes):
    vals, poss = [], []
    for _ in range(k):
        m = jnp.max(s, axis=0, keepdims=True)
        pos = jnp.min(jnp.where(s == m, codes, jnp.inf), axis=0, keepdims=True)
        vals.append(m)
        poss.append(pos)
        s = jnp.where(codes == pos, -jnp.inf, s)
    return jnp.concatenate(vals, axis=0), jnp.concatenate(poss, axis=0)


def _candidate_pairs(k):
    return [(i, j) for i in range(k) for j in range(k) if (i + 1) * (j + 1) <= k]


def candidate_selectors():
    pairs = _candidate_pairs(PEER_TOPK)
    n_rows = -(-len(pairs) // 8) * 8
    first = jnp.zeros((n_rows, PEER_TOPK), jnp.float32).at[
        jnp.arange(len(pairs)), jnp.array([i for i, _ in pairs])].set(1.0)
    second = jnp.zeros((n_rows, PEER_TOPK), jnp.float32).at[
        jnp.arange(len(pairs)), jnp.array([j for _, j in pairs])].set(1.0)
    return first.astype(jnp.bfloat16), second.astype(jnp.bfloat16)


def _pick_rows(table, sel):
    out = jnp.zeros_like(sel)
    for r in range(table.shape[0]):
        out = out + jnp.where(sel == float(r), table[r:r + 1, :], 0.0)
    return out


def _peer_select_kernel(q_ref, keys_ref, sel1_ref, sel2_ref, idx_ref, gate_ref):
    K = PEER_TOPK
    half_w = D_KEY // 2
    tokens = q_ref.shape[0]
    key_rows = _iota2((N_KEYS, tokens), 0).astype(jnp.float32)
    sel1, sel2 = sel1_ref[...], sel2_ref[...]
    n_cand = len(_candidate_pairs(K))
    rank = _iota2((K, tokens), 0).astype(jnp.bfloat16)
    cand_rows = _iota2((sel1.shape[0], tokens), 0)
    cand_code = jnp.where(cand_rows < n_cand, _mm(sel1, rank) * K + _mm(sel2, rank), float(K * K))

    def pick_exact(sel, x):
        return sum(_mm(sel, piece) for piece in _bf16_pieces(x, 3))

    idx_rows, gate_rows = [], []
    for h in range(PEER_HEADS):
        tops = []
        for c in range(2):
            hc = 2 * h + c
            scores = _mm_nt(keys_ref[hc], q_ref[:, hc * half_w:(hc + 1) * half_w])
            tops.append(_top_rows(scores, K, key_rows))
        (s1, i1), (s2, i2) = tops
        cand = jnp.where(cand_rows < n_cand, pick_exact(sel1, s1) + pick_exact(sel2, s2), -jnp.inf)
        top_s, pos = _top_rows(cand, K, cand_code)
        first = jnp.floor(pos * (1.0 / K))
        second = pos - first * K
        expert = _pick_rows(i1, first) * N_KEYS + _pick_rows(i2, second)
        e = jnp.exp(top_s - top_s[0:1, :])
        idx_rows.append(expert.astype(jnp.int32))
        gate_rows.append(e / jnp.sum(e, axis=0, keepdims=True))
    idx_ref[0] = jnp.concatenate(idx_rows, axis=0)
    gate_ref[0] = jnp.concatenate(gate_rows, axis=0)


def peer_select(q, keys, n, src_block):
    W = q.shape[1]
    tb = PEER_TOKENS_PER_STEP
    out_block = pl.BlockSpec((1, PEER_PICKS, tb), lambda i: (i, 0, 0))
    sel1, sel2 = candidate_selectors()
    return pl.pallas_call(
        _peer_select_kernel,
        grid=(n,),
        in_specs=[pl.BlockSpec((tb, W), lambda i: (src_block(i), 0)),
                  pl.BlockSpec(keys.shape, lambda i: (0, 0, 0)),
                  pl.BlockSpec(sel1.shape, lambda i: (0, 0)),
                  pl.BlockSpec(sel2.shape, lambda i: (0, 0))],
        out_specs=[out_block, out_block],
        out_shape=[jax.ShapeDtypeStruct((n, PEER_PICKS, tb), jnp.int32),
                   jax.ShapeDtypeStruct((n, PEER_PICKS, tb), jnp.float32)],
        compiler_params=pltpu.CompilerParams(
            dimension_semantics=("arbitrary",), vmem_limit_bytes=VMEM_LIMIT_BYTES),
        name="peer_select",
    )(q, keys, sel1, sel2)


ATTN_TILES = ATTN_WIDTH // LANES
Q_HEADS_PER_KV = ATTN_Q_HEADS // ATTN_KV_HEADS


def _attn_kernel(q_ref, kvp_ref, kvc_ref, qg_ref, kg_ref, sink_ref, o_ref):
    f32, bf16 = jnp.float32, jnp.bfloat16
    nb = pl.program_id(1)
    lane = _iota2((LANES, LANES), 1)
    row = _iota2((LANES, LANES), 0)
    seg = ((lane >= HEAD_DIM) == (row >= HEAD_DIM)).astype(bf16)

    def head_rms(x, gain):
        ms = sum(_mm(piece, seg) for piece in _bf16_pieces(x * x, 2)) * (1.0 / HEAD_DIM)
        return x * lax.rsqrt(ms + RMS_EPS) * gain

    kv = jnp.concatenate([kvp_ref[...], kvc_ref[...]], axis=0)
    k = head_rms(kv[:, :LANES], kg_ref[...])
    v = kv[:, LANES:]
    lane_k = _iota2(k.shape, 1)
    k_sw, v_sw = pltpu.roll(k, HEAD_DIM, 1), pltpu.roll(v, HEAD_DIM, 1)
    k_dup = [jnp.where(lane_k < HEAD_DIM, k, k_sw).astype(bf16), jnp.where(lane_k < HEAD_DIM, k_sw, k).astype(bf16)]
    v_dup = [jnp.where(lane_k < HEAD_DIM, v, v_sw).astype(bf16), jnp.where(lane_k < HEAD_DIM, v_sw, v).astype(bf16)]

    qi = _iota2((2 * BLOCK, 2 * BLOCK), 0) & (BLOCK - 1)
    ks = _iota2((2 * BLOCK, 2 * BLOCK), 1)
    dist = qi + BLOCK - ks
    key_pos = (nb - 1) * BLOCK + ks
    allowed = (dist >= 0) & (dist < WINDOW) & (key_pos >= PAD_LEFT)
    upper = _iota2((2 * BLOCK, 1), 0) < BLOCK
    head0 = _iota2((BLOCK, LANES), 1) < HEAD_DIM

    tiles = range(ATTN_TILES)
    q_t = [head_rms(q_ref[:, t * LANES:(t + 1) * LANES], qg_ref[...]) for t in tiles]
    q2 = [jnp.concatenate([jnp.where(head0, q_t[t], 0.0), jnp.where(head0, 0.0, q_t[t])], axis=0).astype(bf16)
          for t in tiles]
    s = [_mm_nt(q2[t], k_dup[(2 * t) // Q_HEADS_PER_KV]) * ATTN_SCALE for t in tiles]
    s = [jnp.where(allowed, s[t], MASK_VALUE) for t in tiles]
    sink = [jnp.where(upper, sink_ref[2 * t], sink_ref[2 * t + 1]) for t in tiles]
    m = [jnp.maximum(jnp.max(s[t], axis=1, keepdims=True), sink[t]) for t in tiles]
    e = [jnp.exp(s[t] - m[t]) for t in tiles]
    denom = [jnp.sum(e[t], axis=1, keepdims=True) + jnp.exp(sink[t] - m[t]) for t in tiles]
    p = [(e[t] / denom[t]).astype(bf16) for t in tiles]
    o2 = [_mm(p[t], v_dup[(2 * t) // Q_HEADS_PER_KV]) for t in tiles]
    o_ref[...] = jnp.concatenate(
        [jnp.where(head0, o2[t][:BLOCK], o2[t][BLOCK:]) for t in tiles], axis=1).astype(o_ref.dtype)


def swa_sink_attention(p, B, LP, col_q, col_kv, q_gain, k_gain, sinks):
    nblk = LP // BLOCK
    kvw = 2 * KV_WIDTH
    qg = jnp.tile(q_gain, LANES // HEAD_DIM).reshape(1, LANES)
    kg = jnp.tile(k_gain, LANES // HEAD_DIM).reshape(1, LANES)
    return pl.pallas_call(
        _attn_kernel,
        grid=(B, nblk),
        in_specs=[
            pl.BlockSpec((BLOCK, ATTN_WIDTH), lambda b, n: (b * nblk + n, col_q // ATTN_WIDTH)),
            pl.BlockSpec((BLOCK, kvw), lambda b, n: (b * nblk + jnp.maximum(n - 1, 0), col_kv // kvw)),
            pl.BlockSpec((BLOCK, kvw), lambda b, n: (b * nblk + n, col_kv // kvw)),
            pl.BlockSpec((1, LANES), lambda b, n: (0, 0)),
            pl.BlockSpec((1, LANES), lambda b, n: (0, 0)),
            pl.BlockSpec(memory_space=pltpu.SMEM),
        ],
        out_specs=pl.BlockSpec((BLOCK, ATTN_WIDTH), lambda b, n: (b * nblk + n, 0)),
        out_shape=jax.ShapeDtypeStruct((B * LP, ATTN_WIDTH), jnp.bfloat16),
        compiler_params=pltpu.CompilerParams(
            dimension_semantics=("arbitrary", "arbitrary"), vmem_limit_bytes=VMEM_LIMIT_BYTES),
        name="swa_sink_attention",
    )(p, p, p, qg, kg, sinks)


def kernel(x, meta_tokens, norm1_g, w_in, shift_mu, w0, w_up, a0, a_up, g_up, k_k, k_a, r_k, gn_w, gn_b,
           q_gain, k_gain, sinks, w_out, norm2_g, peer_query, peer_sub_keys, peer_down, peer_up):
    B, S, D = x.shape
    assert w_in.shape[0] == 1, "single-layer stack"
    LP = S + BLOCK
    T = B * LP
    meta =jnp.broadcast_to(meta_tokens[None], (B, N_META, D))
    h = jnp.concatenate([jnp.zeros((B, PAD_LEFT, D), x.dtype), meta, x], axis=1)
    ht = h.reshape(T, D)

    w = w_in[0]
    n_lora = DECAY_LORA + AAA_LORA + GATE_LORA
    rwkv0 = ATTN_COLS
    lora0 = rwkv0 + 3 * RWKV_WIDTH
    w_in_b = jnp.concatenate([
        w[:, rwkv0:lora0], w[:, :ATTN_WIDTH],
        jnp.pad(w[:, lora0:lora0 + n_lora], ((0, 0), (0, LORA_PAD - n_lora))),
        w[:, ATTN_WIDTH:ATTN_COLS]], axis=1).astype(jnp.bfloat16)
    col_q = 3 * RWKV_WIDTH
    col_l = col_q + ATTN_WIDTH
    col_kv = col_l + LORA_PAD
    n_in = col_kv + 2 * KV_WIDTH
    p = norm_matmul(ht, norm1_g[0], w_in_b, MATMUL_ROWS, n_in // 2)

    y_rwkv = rwkv7_mix(p, B, LP, 0, RWKV_WIDTH, 2 * RWKV_WIDTH, col_l,
                       *rwkv_params(shift_mu[0], w0[0], w_up[0], a0[0], a_up[0], g_up[0], k_k[0], k_a[0], r_k[0],
                                    gn_w[0], gn_b[0]))
    y_attn = swa_sink_attention(p, B, LP, col_q, col_kv, q_gain[0], k_gain[0], sinks[0])
    h2 = matmul_residual(y_rwkv, y_attn, w_out[0].astype(jnp.bfloat16), ht, MATMUL_ROWS, MATMUL_COLS)

    pq = norm_matmul(h2, norm2_g[0], peer_query[0].astype(jnp.bfloat16), MATMUL_ROWS, MATMUL_COLS, jnp.bfloat16)
    keys = peer_sub_keys[0].reshape(2 * PEER_HEADS, N_KEYS, D_KEY // 2).astype(jnp.bfloat16)
    table = pack_expert_table(peer_down[0], peer_up[0])

    blocks_per_seq = LP // BLOCK
    kept_per_seq = blocks_per_seq - 1

    def kept_block(i):
        return (i // kept_per_seq) * blocks_per_seq + 1 + i % kept_per_seq

    idx_t, gate_t = peer_select(pq, keys, B * kept_per_seq, kept_block)
    seq_pos = (1 + jnp.arange(B * kept_per_seq, dtype=jnp.int32) % kept_per_seq) * BLOCK
    return peer_mix(idx_t, gate_t, h2, norm2_g[0], table, kept_block, seq_pos).reshape(B, S, D)
```

```python
import jax
import jax.numpy as jnp
from jax import lax
from jax.experimental import pallas as pl
from jax.experimental.pallas import tpu as pltpu

N_META = 16
BLOCK = 128
PAD_LEFT = BLOCK - N_META
HEAD_DIM = 64
RWKV_WIDTH = 1024
ATTN_WIDTH = 1024
ATTN_Q_HEADS = ATTN_WIDTH // HEAD_DIM
ATTN_KV_HEADS = 2
KV_WIDTH = ATTN_KV_HEADS * HEAD_DIM
WINDOW = 128
ATTN_SCALE = HEAD_DIM ** -0.5
MASK_VALUE = -1e30
DECAY_LORA = 64
AAA_LORA = 64
GATE_LORA = 160
RMS_EPS = 1e-6
GN_EPS = 64e-5
ATTN_COLS = ATTN_WIDTH + 2 * KV_WIDTH
PEER_HEADS = 8
N_KEYS = 128
PEER_TOPK = 16
D_KEY = 256

VMEM_LIMIT_BYTES = 48 * 1024 * 1024
MATMUL_ROWS = 512
MATMUL_COLS = 1024


def _norm_matmul_kernel(x_ref, g_ref, w_ref, o_ref):
    x = x_ref[...]
    ms = jnp.mean(x * x, axis=-1, keepdims=True)
    u = x * lax.rsqrt(ms + RMS_EPS) * g_ref[...]
    o_ref[...] = jnp.dot(
        u.astype(jnp.bfloat16), w_ref[...], preferred_element_type=jnp.float32).astype(o_ref.dtype)


def norm_matmul(x, g, w, tm, tn, out_dtype=jnp.float32):
    m, k = x.shape
    n = w.shape[1]
    return pl.pallas_call(
        _norm_matmul_kernel,
        grid=(n // tn, m // tm),
        in_specs=[
            pl.BlockSpec((tm, k), lambda j, i: (i, 0)),
            pl.BlockSpec((1, k), lambda j, i: (0, 0)),
            pl.BlockSpec((k, tn), lambda j, i: (0, j)),
        ],
        out_specs=pl.BlockSpec((tm, tn), lambda j, i: (i, j)),
        out_shape=jax.ShapeDtypeStruct((m, n), out_dtype),
        compiler_params=pltpu.CompilerParams(
            dimension_semantics=("arbitrary", "arbitrary"), vmem_limit_bytes=VMEM_LIMIT_BYTES),
        name="norm_matmul",
    )(x, g.reshape(1, k), w)


def _matmul_residual_kernel(xa_ref, xb_ref, w_ref, r_ref, o_ref):
    ka = xa_ref.shape[1]
    o_ref[...] = (r_ref[...]
                  + jnp.dot(xa_ref[...], w_ref[:ka, :], preferred_element_type=jnp.float32)
                  + jnp.dot(xb_ref[...], w_ref[ka:, :], preferred_element_type=jnp.float32))


def matmul_residual(xa, xb, w, r, tm, tn):
    m, ka = xa.shape
    kb = xb.shape[1]
    n = w.shape[1]
    return pl.pallas_call(
        _matmul_residual_kernel,
        grid=(n // tn, m // tm),
        in_specs=[
            pl.BlockSpec((tm, ka), lambda j, i: (i, 0)),
            pl.BlockSpec((tm, kb), lambda j, i: (i, 0)),
            pl.BlockSpec((ka + kb, tn), lambda j, i: (0, j)),
            pl.BlockSpec((tm, tn), lambda j, i: (i, j)),
        ],
        out_specs=pl.BlockSpec((tm, tn), lambda j, i: (i, j)),
        out_shape=jax.ShapeDtypeStruct((m, n), jnp.float32),
        compiler_params=pltpu.CompilerParams(
            dimension_semantics=("arbitrary", "arbitrary"), vmem_limit_bytes=VMEM_LIMIT_BYTES),
        name="matmul_residual",
    )(xa, xb, w, r)


RWKV_CHUNK = 64
RWKV_SEQS_PER_STEP = 2
LANES = 128
RWKV_TILES = RWKV_WIDTH // LANES
LORA_PAD = 512
GATE_PAD = 2 * LANES


def _dot(a, b, dims):
    return lax.dot_general(a, b, (dims, ((), ())), preferred_element_type=jnp.float32)


def _mm(a, b):
    return _dot(a, b, ((1,), (0,)))


def _mm_nt(a, b):
    return _dot(a, b, ((1,), (1,)))


def _mm_tn(a, b):
    return _dot(a, b, ((0,), (0,)))


def _iota2(shape, axis):
    return lax.broadcasted_iota(jnp.int32, shape, axis)


def _sigmoid(x):
    return 1.0 / (1.0 + jnp.exp(-x))


def _bf16_pieces(x, n):
    pieces = []
    for _ in range(n):
        p = x.astype(jnp.bfloat16)
        pieces.append(p)
        x = x - p.astype(jnp.float32)
    return pieces


def _token_shift(x, prev_ref, mu):
    rolled = pltpu.roll(x, 1, 0)
    prev = jnp.where(_iota2(x.shape, 0) == 0, prev_ref[...], rolled)
    prev_ref[...] = x[x.shape[0] - 1:, :]
    return x + mu * (prev - x)


def _rwkv_kernel(xr_ref, xk_ref, xv_ref, xl_ref, mu_ref, mul_ref, vec_ref, wup_ref, aup_ref, gup_ref,
                 o_ref, s_ref, pr_ref, pk_ref, pv_ref, pl_ref):
    C = RWKV_CHUNK
    NS = RWKV_SEQS_PER_STEP
    f32, bf16 = jnp.float32, jnp.bfloat16

    @pl.when(pl.program_id(1) == 0)
    def _():
        s_ref[...] = jnp.zeros_like(s_ref)
        pr_ref[...] = jnp.zeros_like(pr_ref)
        pk_ref[...] = jnp.zeros_like(pk_ref)
        pv_ref[...] = jnp.zeros_like(pv_ref)
        pl_ref[...] = jnp.zeros_like(pl_ref)

    def shifted(x_ref, prev_ref, mu):
        return jnp.concatenate([_token_shift(x_ref[i], prev_ref.at[i], mu) for i in range(NS)], axis=0)

    r = shifted(xr_ref, pr_ref, mu_ref[0:1, :])
    kraw = shifted(xk_ref, pk_ref, mu_ref[1:2, :])
    v = shifted(xv_ref, pv_ref, mu_ref[2:3, :])
    xl = shifted(xl_ref, pl_ref, mul_ref[...])

    w0, a0, k_k, k_a = vec_ref[0:1, :], vec_ref[1:2, :], vec_ref[2:3, :], vec_ref[3:4, :]
    r_k, gn_w, gn_b = vec_ref[4:5, :], vec_ref[5:6, :], vec_ref[6:7, :]

    x_wa = xl[:, :LANES]
    wl = w0 + _mm(jnp.tanh(x_wa).astype(jnp.bfloat16), wup_ref[...])
    z = -wl
    softplus = jnp.maximum(z, 0.0) + jnp.log(1.0 + jnp.exp(-jnp.abs(z)))
    lw = -jnp.exp(-softplus - 0.5)
    a = _sigmoid(a0 + _mm(x_wa.astype(jnp.bfloat16), aup_ref[...]))
    g = _mm(_sigmoid(xl[:, LANES:LANES + GATE_PAD]).astype(jnp.bfloat16), gup_ref[...])

    tr, tc = _iota2((NS * C, NS * C), 0), _iota2((NS * C, NS * C), 1)
    tri = ((tr >= tc) & ((tr & -C) == (tc & -C))).astype(bf16)
    cl = sum(_mm(tri, piece) for piece in _bf16_pieces(lw, 3))
    e_pos = jnp.exp(cl)
    e_excl = jnp.exp(cl - lw)
    e_neg = 1.0 / e_pos
    e_end = [e_pos[(i + 1) * C - 1:(i + 1) * C, :] for i in range(NS)]
    e_tail = jnp.concatenate([e_end[i] * e_neg[i * C:(i + 1) * C, :] for i in range(NS)], axis=0)

    lane = _iota2((LANES, LANES), 1)
    row = _iota2((LANES, LANES), 0)
    same_head = (lane >= HEAD_DIM) == (row >= HEAD_DIM)
    seg = same_head.astype(bf16)
    eye = (lane == row).astype(f32)
    t_row, t_col = row & (C - 1), lane & (C - 1)
    strict = t_col < t_row
    in_block4 = (t_row >> 2) == (t_col >> 2)
    in_block16 = (t_row >> 4) == (t_col >> 4)
    incl = t_col <= t_row
    incl2 = jnp.concatenate([incl, incl], axis=1)
    lane_c = _iota2((C, LANES), 1)
    head0 = lane_c < HEAD_DIM

    def seg_sum(x):
        return sum(_mm(piece, seg) for piece in _bf16_pieces(x, 2))

    def by_head(x):
        return jnp.concatenate([jnp.where(head0, x, 0.0), jnp.where(head0, 0.0, x)], axis=0).astype(bf16)

    def twice(x):
        return jnp.concatenate([x, x], axis=0)

    tiles = range(NS * RWKV_TILES)
    sls = [slice((j % RWKV_TILES) * LANES, (j % RWKV_TILES + 1) * LANES) for j in tiles]
    seq = [j // RWKV_TILES for j in tiles]

    def tile(x, j):
        return x[seq[j] * C:(seq[j] + 1) * C, sls[j]]

    kr = [tile(kraw, j) for j in tiles]
    kkr = [kr[j] * k_k[:, sls[j]] for j in tiles]
    kk_ss = [seg_sum(kkr[j] * kkr[j]) for j in tiles]
    kk = [kkr[j] / jnp.maximum(jnp.sqrt(kk_ss[j]), 1e-12) for j in tiles]
    a_t = [tile(a, j) for j in tiles]
    k2 = [kr[j] * (1.0 + (a_t[j] - 1.0) * k_a[:, sls[j]]) for j in tiles]
    r_t = [tile(r, j) for j in tiles]
    v_t = [tile(v, j) for j in tiles]
    kka = [kk[j] * a_t[j] for j in tiles]
    alpha = [-kk[j] * tile(e_excl, j) for j in tiles]
    r_dec = [r_t[j] * tile(e_pos, j) for j in tiles]
    beta = [kka[j] * tile(e_neg, j) for j in tiles]
    k_neg = [k2[j] * tile(e_neg, j) for j in tiles]
    s_old = [s_ref[j] for j in tiles]

    gram = [_mm_nt(jnp.concatenate([by_head(alpha[j]), by_head(r_dec[j])], axis=0),
                   jnp.concatenate([by_head(beta[j]), by_head(k_neg[j])], axis=0)) for j in tiles]
    m1 = [_mm_nt(jnp.concatenate([alpha[j], r_dec[j]], axis=0).astype(bf16), s_old[j].astype(bf16))
          for j in tiles]
    a_s = [jnp.where(strict, gram[j][:LANES, :LANES], 0.0) for j in tiles]
    b_s = [jnp.where(strict, gram[j][:LANES, LANES:], 0.0).astype(bf16) for j in tiles]
    r_i = [jnp.where(incl2, gram[j][LANES:, :], 0.0).astype(bf16) for j in tiles]
    vv = [twice(v_t[j]).astype(bf16) for j in tiles]
    rhs = [twice(m1[j][:C]) + _mm(b_s[j], vv[j]) for j in tiles]
    def mmb(x, y):
        return _mm(x.astype(bf16), y.astype(bf16))

    def inv_index4(nil):
        sq = [mmb(nil[j], nil[j]) for j in tiles]
        return [eye + nil[j] + sq[j] + mmb(sq[j], nil[j]) for j in tiles]

    t0 = inv_index4([jnp.where(in_block4, a_s[j], 0.0) for j in tiles])
    n1 = [mmb(t0[j], jnp.where(in_block16 & ~in_block4, a_s[j], 0.0)) for j in tiles]
    p1 = inv_index4(n1)
    t1 = [mmb(p1[j], t0[j]) for j in tiles]
    n2 = [mmb(t1[j], jnp.where(in_block16, 0.0, a_s[j])) for j in tiles]
    p2 = inv_index4(n2)
    t1_rhs = [mmb(t1[j], rhs[j]) for j in tiles]
    u_st = [mmb(p2[j], t1_rhs[j]) for j in tiles]
    y_st = [twice(m1[j][C:]) + _mm(r_i[j], jnp.concatenate([u_st[j].astype(bf16), vv[j]], axis=0))
            for j in tiles]
    u_t = [jnp.where(head0, u_st[j][:C], u_st[j][C:]) for j in tiles]
    y_t = [jnp.where(head0, y_st[j][:C], y_st[j][C:]) for j in tiles]

    upd = [_mm_tn(jnp.concatenate([u_t[j], v_t[j]], axis=0).astype(bf16),
                  jnp.concatenate([kka[j] * tile(e_tail, j), k2[j] * tile(e_tail, j)], axis=0).astype(bf16))
           for j in tiles]
    for j in tiles:
        s_ref[j] = s_old[j] * e_end[seq[j]][:, sls[j]] + jnp.where(same_head, upd[j], 0.0)

    mu_y = [seg_sum(y_t[j]) * (1.0 / HEAD_DIM) for j in tiles]
    dev = [y_t[j] - mu_y[j] for j in tiles]
    var = [seg_sum(dev[j] * dev[j]) * (1.0 / HEAD_DIM) for j in tiles]
    bonus = [seg_sum(r_t[j] * k2[j] * r_k[:, sls[j]]) * v_t[j] for j in tiles]
    outs = [(dev[j] * lax.rsqrt(var[j] + GN_EPS) * gn_w[:, sls[j]] + gn_b[:, sls[j]] + bonus[j]) * tile(g, j)
            for j in tiles]
    for i in range(NS):
        o_ref[i] = jnp.concatenate(outs[i * RWKV_TILES:(i + 1) * RWKV_TILES], axis=1).astype(o_ref.dtype)


def rwkv_params(shift_mu, w0, w_up, a0, a_up, g_up, k_k, k_a, r_k, gn_w, gn_b):
    W = RWKV_WIDTH
    n_lora = DECAY_LORA + AAA_LORA + GATE_LORA
    mu_rkv = shift_mu[:3 * W].reshape(3, W)
    mu_l = jnp.pad(shift_mu[3 * W:], (0, LORA_PAD - n_lora)).reshape(1, LORA_PAD)
    vecs = jnp.stack([w0, a0, k_k, k_a, r_k.reshape(W), gn_w, gn_b, jnp.zeros_like(w0)])
    w_up_p = jnp.pad(w_up, ((0, LANES - DECAY_LORA), (0, 0))).astype(jnp.bfloat16)
    a_up_p = jnp.pad(a_up, ((DECAY_LORA, LANES - DECAY_LORA - AAA_LORA), (0, 0))).astype(jnp.bfloat16)
    g_up_p = jnp.pad(g_up, ((0, GATE_PAD - GATE_LORA), (0, 0))).astype(jnp.bfloat16)
    return mu_rkv, mu_l, vecs, w_up_p, a_up_p, g_up_p


def rwkv7_mix(p, B, LP, col_r, col_k, col_v, col_l, mu_rkv, mu_l, vecs, w_up, a_up, g_up):
    C = RWKV_CHUNK
    NS = RWKV_SEQS_PER_STEP
    assert B % NS == 0 and LP % C == 0, (B, LP)
    nc = LP // C
    W = RWKV_WIDTH
    p3 = p.reshape(B, LP, p.shape[1])

    def col_spec(width, col):
        return pl.BlockSpec((NS, C, width), lambda b, c: (b, c, col // width))

    def full(shape):
        return pl.BlockSpec(shape, lambda b, c: (0,) * len(shape))

    def carried_row(width):
        return pltpu.VMEM((NS, 1, width), jnp.float32)

    out = pl.pallas_call(
        _rwkv_kernel,
        grid=(B // NS, nc),
        in_specs=[
            col_spec(W, col_r), col_spec(W, col_k), col_spec(W, col_v), col_spec(LORA_PAD, col_l),
            full(mu_rkv.shape), full(mu_l.shape), full(vecs.shape),
            full(w_up.shape), full(a_up.shape), full(g_up.shape),
        ],
        out_specs=pl.BlockSpec((NS, C, W), lambda b, c: (b, c, 0)),
        out_shape=jax.ShapeDtypeStruct((B, LP, W), jnp.bfloat16),
        scratch_shapes=[
            pltpu.VMEM((NS * RWKV_TILES, LANES, LANES), jnp.float32),
            carried_row(W), carried_row(W), carried_row(W), carried_row(LORA_PAD),
        ],
        compiler_params=pltpu.CompilerParams(
            dimension_semantics=("arbitrary", "arbitrary"), vmem_limit_bytes=VMEM_LIMIT_BYTES),
        name="rwkv7_mix",
    )(p3, p3, p3, p3, mu_rkv, mu_l, vecs, w_up, a_up, g_up)
    return out.reshape(B * LP, W)


PEER_PICKS = PEER_HEADS * PEER_TOPK
PEER_TOKENS_PER_STEP = 128
PEER_GROUP = 4
PEER_GROUPS_PER_STEP = PEER_TOKENS_PER_STEP // PEER_GROUP
PEER_BUFFERS = 4
EXPERT_TILE_ROWS = 16
EXPERT_TILE_PITCH = 20
assert PEER_GROUPS_PER_STEP % PEER_BUFFERS == 0


PACK_ROWS_PER_STEP = 256


def _pack_kernel(down_ref, up_ref, o_ref):
    def bf16_bits_high(t):
        return lax.bitcast_convert_type(t.astype(jnp.bfloat16).astype(jnp.float32), jnp.uint32)
    words = bf16_bits_high(down_ref[...]) | (bf16_bits_high(up_ref[...]) >> 16)
    o_ref[...] = words.reshape(o_ref.shape)


def pack_expert_table(down, up):
    E, D = down.shape
    r = PACK_ROWS_PER_STEP
    return pl.pallas_call(
        _pack_kernel,
        grid=(E // r,),
        in_specs=[pl.BlockSpec((r, D), lambda i: (i, 0)), pl.BlockSpec((r, D), lambda i: (i, 0))],
        out_specs=pl.BlockSpec((r, EXPERT_TILE_ROWS, D // EXPERT_TILE_ROWS), lambda i: (i, 0, 0)),
        out_shape=jax.ShapeDtypeStruct((E, EXPERT_TILE_ROWS, D // EXPERT_TILE_ROWS), jnp.uint32),
        compiler_params=pltpu.CompilerParams(
            dimension_semantics=("arbitrary",), vmem_limit_bytes=VMEM_LIMIT_BYTES),
        name="pack_expert_table",
    )(down, up)


def _peer_mix_kernel(idx_ref, idx_next_ref, gate_t_ref, res_ref, g_ref, pos_ref, tab_ref, o_ref, *scratch):
    bufs, sem, x_ref = scratch[:PEER_BUFFERS], scratch[PEER_BUFFERS], scratch[PEER_BUFFERS + 1]
    G = PEER_GROUP
    ahead = PEER_BUFFERS - 1
    block = pl.program_id(0)
    gate_t = gate_t_ref[0]
    tok_lane = _iota2(gate_t.shape, 1)

    def issue_token(picks_ref, group, s, b):
        for j in range(PEER_PICKS):
            e = picks_ref[0, j, group * G + s]
            tile_rows = pl.ds(j * EXPERT_TILE_PITCH, EXPERT_TILE_ROWS)
            pltpu.make_async_copy(tab_ref.at[e], bufs[b].at[s, tile_rows, :], sem.at[b]).start(priority=j % 2)

    def wait_group(b):
        view = bufs[b].at[:, pl.ds(0, PEER_PICKS * EXPERT_TILE_ROWS), :]
        pltpu.make_async_copy(view, view, sem.at[b]).wait()

    def words(b, s, r):
        return bufs[b][s, pl.ds(r, PEER_PICKS, stride=EXPERT_TILE_PITCH), :]

    def mix_token(group, s, b):
        tt = group * G + s
        seg_w = x_ref.shape[1] // EXPERT_TILE_ROWS
        x_row = x_ref[pl.ds(tt, 1), :]
        acc = jnp.zeros((PEER_PICKS, seg_w), jnp.float32)
        for r in range(EXPERT_TILE_ROWS):
            down = lax.bitcast_convert_type(words(b, s, r) & jnp.uint32(0xFFFF0000), jnp.float32)
            acc = acc + down * x_row[:, r * seg_w:(r + 1) * seg_w]
        h = jnp.sum(acc, axis=1, keepdims=True)
        gate = jnp.sum(jnp.where(tok_lane == tt, gate_t, 0.0), axis=1, keepdims=True)
        w = 0.5 * h * (1.0 + lax.erf(h * (2.0 ** -0.5))) * gate
        mix = [jnp.sum(w * lax.bitcast_convert_type(words(b, s, r) << 16, jnp.float32), axis=0, keepdims=True)
               for r in range(EXPERT_TILE_ROWS)]
        out = res_ref[pl.ds(tt, 1), :] + jnp.concatenate(mix, axis=1)
        o_ref[pl.ds(tt, 1), :] = jnp.where(pos_ref[block] + tt >= PAD_LEFT, out, 0.0)

    def step(group, b):
        wait_group(b)
        target = (b + ahead) % PEER_BUFFERS
        for s in range(G):
            if isinstance(group, int) and group + ahead >= PEER_GROUPS_PER_STEP:
                @pl.when(block + 1 < pl.num_programs(0))
                def _():
                    issue_token(idx_next_ref, group + ahead - PEER_GROUPS_PER_STEP, s, target)
            else:
                issue_token(idx_ref, group + ahead, s, target)
            mix_token(group, s, b)

    @pl.when(block == 0)
    def _():
        for g in range(ahead):
            for s in range(G):
                issue_token(idx_ref, g, s, g)

    res = res_ref[...]
    x_ref[...] = res * lax.rsqrt(jnp.mean(res * res, axis=1, keepdims=True) + RMS_EPS) * g_ref[...]
    n_main = (PEER_GROUPS_PER_STEP - ahead) // PEER_BUFFERS

    def rotation(it, carry):
        for b in range(PEER_BUFFERS):
            step(it * PEER_BUFFERS + b, b)
        return carry

    lax.fori_loop(0, n_main, rotation, 0)
    for g in range(n_main * PEER_BUFFERS, PEER_GROUPS_PER_STEP):
        step(g, g % PEER_BUFFERS)


def peer_mix(idx, gate, res, norm_g, table, src_block, seq_pos):
    D = res.shape[1]
    n = idx.shape[0]
    tb = PEER_TOKENS_PER_STEP
    tile = (EXPERT_TILE_ROWS, D // EXPERT_TILE_ROWS)
    return pl.pallas_call(
        _peer_mix_kernel,
        grid=(n,),
        in_specs=[
            pl.BlockSpec((1, PEER_PICKS, tb), lambda i: (i, 0, 0), memory_space=pltpu.SMEM),
            pl.BlockSpec((1, PEER_PICKS, tb), lambda i: (jnp.minimum(i + 1, n - 1), 0, 0),
                         memory_space=pltpu.SMEM),
            pl.BlockSpec((1, PEER_PICKS, tb), lambda i: (i, 0, 0)),
            pl.BlockSpec((tb, D), lambda i: (src_block(i), 0)),
            pl.BlockSpec((1, D), lambda i: (0, 0)),
            pl.BlockSpec(memory_space=pltpu.SMEM),
            pl.BlockSpec(memory_space=pl.ANY),
        ],
        out_specs=pl.BlockSpec((tb, D), lambda i: (i, 0)),
        out_shape=jax.ShapeDtypeStruct((n * tb, D), jnp.float32),
        scratch_shapes=[pltpu.VMEM((PEER_GROUP, PEER_PICKS * EXPERT_TILE_PITCH, tile[1]), jnp.uint32)
                        for _ in range(PEER_BUFFERS)]
        + [pltpu.SemaphoreType.DMA((PEER_BUFFERS,)), pltpu.VMEM((tb, D), jnp.float32)],
        compiler_params=pltpu.CompilerParams(
            dimension_semantics=("arbitrary",), vmem_limit_bytes=VMEM_LIMIT_BYTES),
        name="peer_mix",
    )(idx, idx, gate, res, norm_g.reshape(1, D), seq_pos, table)


def _top_rows(s, k, codes):
    vals, poss = [], []
    for _ in range(k):
        m = jnp.max(s, axis=0, keepdims=True)
        pos = jnp.min(jnp.where(s == m, codes, jnp.inf), axis=0, keepdims=True)
        vals.append(m)
        poss.append(pos)
        s = jnp.where(codes == pos, -jnp.inf, s)
    return jnp.concatenate(vals, axis=0), jnp.concatenate(poss, axis=0)


def _candidate_pairs(k):
    return [(i, j) for i in range(k) for j in range(k) if (i + 1) * (j + 1) <= k]


def candidate_selectors():
    pairs = _candidate_pairs(PEER_TOPK)
    n_rows = -(-len(pairs) // 8) * 8
    first = jnp.zeros((n_rows, PEER_TOPK), jnp.float32).at[
        jnp.arange(len(pairs)), jnp.array([i for i, _ in pairs])].set(1.0)
    second = jnp.zeros((n_rows, PEER_TOPK), jnp.float32).at[
        jnp.arange(len(pairs)), jnp.array([j for _, j in pairs])].set(1.0)
    return first.astype(jnp.bfloat16), second.astype(jnp.bfloat16)


def _pick_rows(table, sel):
    out = jnp.zeros_like(sel)
    for r in range(table.shape[0]):
        out = out + jnp.where(sel == float(r), table[r:r + 1, :], 0.0)
    return out


def _peer_select_kernel(q_ref, keys_ref, sel1_ref, sel2_ref, idx_ref, gate_ref):
    K = PEER_TOPK
    half_w = D_KEY // 2
    tokens = q_ref.shape[0]
    key_rows = _iota2((N_KEYS, tokens), 0).astype(jnp.float32)
    sel1, sel2 = sel1_ref[...], sel2_ref[...]
    n_cand = len(_candidate_pairs(K))
    rank = _iota2((K, tokens), 0).astype(jnp.bfloat16)
    cand_rows = _iota2((sel1.shape[0], tokens), 0)
    cand_code = jnp.where(cand_rows < n_cand, _mm(sel1, rank) * K + _mm(sel2, rank), float(K * K))

    def pick_exact(sel, x):
        return sum(_mm(sel, piece) for piece in _bf16_pieces(x, 3))

    idx_rows, gate_rows = [], []
    for h in range(PEER_HEADS):
        tops = []
        for c in range(2):
            hc = 2 * h + c
            scores = _mm_nt(keys_ref[hc], q_ref[:, hc * half_w:(hc + 1) * half_w])
            tops.append(_top_rows(scores, K, key_rows))
        (s1, i1), (s2, i2) = tops
        cand = jnp.where(cand_rows < n_cand, pick_exact(sel1, s1) + pick_exact(sel2, s2), -jnp.inf)
        top_s, pos = _top_rows(cand, K, cand_code)
        first = jnp.floor(pos * (1.0 / K))
        second = pos - first * K
        expert = _pick_rows(i1, first) * N_KEYS + _pick_rows(i2, second)
        e = jnp.exp(top_s - top_s[0:1, :])
        idx_rows.append(expert.astype(jnp.int32))
        gate_rows.append(e / jnp.sum(e, axis=0, keepdims=True))
    idx_ref[0] = jnp.concatenate(idx_rows, axis=0)
    gate_ref[0] = jnp.concatenate(gate_rows, axis=0)


def peer_select(q, keys, n, src_block):
    W = q.shape[1]
    tb = PEER_TOKENS_PER_STEP
    out_block = pl.BlockSpec((1, PEER_PICKS, tb), lambda i: (i, 0, 0))
    sel1, sel2 = candidate_selectors()
    return pl.pallas_call(
        _peer_select_kernel,
        grid=(n,),
        in_specs=[pl.BlockSpec((tb, W), lambda i: (src_block(i), 0)),
                  pl.BlockSpec(keys.shape, lambda i: (0, 0, 0)),
                  pl.BlockSpec(sel1.shape, lambda i: (0, 0)),
                  pl.BlockSpec(sel2.shape, lambda i: (0, 0))],
        out_specs=[out_block, out_block],
        out_shape=[jax.ShapeDtypeStruct((n, PEER_PICKS, tb), jnp.int32),
                   jax.ShapeDtypeStruct((n, PEER_PICKS, tb), jnp.float32)],
        compiler_params=pltpu.CompilerParams(
            dimension_semantics=("arbitrary",), vmem_limit_bytes=VMEM_LIMIT_BYTES),
        name="peer_select",
    )(q, keys, sel1, sel2)


ATTN_TILES = ATTN_WIDTH // LANES
Q_HEADS_PER_KV = ATTN_Q_HEADS // ATTN_KV_HEADS


def _attn_kernel(q_ref, kvp_ref, kvc_ref, qg_ref, kg_ref, sink_ref, o_ref):
    f32, bf16 = jnp.float32, jnp.bfloat16
    nb = pl.program_id(1)
    lane = _iota2((LANES, LANES), 1)
    row = _iota2((LANES, LANES), 0)
    seg = ((lane >= HEAD_DIM) == (row >= HEAD_DIM)).astype(bf16)

    def head_rms(x, gain):
        ms = sum(_mm(piece, seg) for piece in _bf16_pieces(x * x, 2)) * (1.0 / HEAD_DIM)
        return x * lax.rsqrt(ms + RMS_EPS) * gain

    kv = jnp.concatenate([kvp_ref[...], kvc_ref[...]], axis=0)
    k = head_rms(kv[:, :LANES], kg_ref[...])
    v = kv[:, LANES:]
    lane_k = _iota2(k.shape, 1)
    k_sw, v_sw = pltpu.roll(k, HEAD_DIM, 1), pltpu.roll(v, HEAD_DIM, 1)
    k_dup = [jnp.where(lane_k < HEAD_DIM, k, k_sw).astype(bf16), jnp.where(lane_k < HEAD_DIM, k_sw, k).astype(bf16)]
    v_dup = [jnp.where(lane_k < HEAD_DIM, v, v_sw).astype(bf16), jnp.where(lane_k < HEAD_DIM, v_sw, v).astype(bf16)]

    qi = _iota2((2 * BLOCK, 2 * BLOCK), 0) & (BLOCK - 1)
    ks = _iota2((2 * BLOCK, 2 * BLOCK), 1)
    dist = qi + BLOCK - ks
    key_pos = (nb - 1) * BLOCK + ks
    allowed = (dist >= 0) & (dist < WINDOW) & (key_pos >= PAD_LEFT)
    upper = _iota2((2 * BLOCK, 1), 0) < BLOCK
    head0 = _iota2((BLOCK, LANES), 1) < HEAD_DIM

    tiles = range(ATTN_TILES)
    q_t = [head_rms(q_ref[:, t * LANES:(t + 1) * LANES], qg_ref[...]) for t in tiles]
    q2 = [jnp.concatenate([jnp.where(head0, q_t[t], 0.0), jnp.where(head0, 0.0, q_t[t])], axis=0).astype(bf16)
          for t in tiles]
    s = [_mm_nt(q2[t], k_dup[(2 * t) // Q_HEADS_PER_KV]) * ATTN_SCALE for t in tiles]
    s = [jnp.where(allowed, s[t], MASK_VALUE) for t in tiles]
    sink = [jnp.where(upper, sink_ref[2 * t], sink_ref[2 * t + 1]) for t in tiles]
    m = [jnp.maximum(jnp.max(s[t], axis=1, keepdims=True), sink[t]) for t in tiles]
    e = [jnp.exp(s[t] - m[t]) for t in tiles]
    denom = [jnp.sum(e[t], axis=1, keepdims=True) + jnp.exp(sink[t] - m[t]) for t in tiles]
    p = [(e[t] / denom[t]).astype(bf16) for t in tiles]
    o2 = [_mm(p[t], v_dup[(2 * t) // Q_HEADS_PER_KV]) for t in tiles]
    o_ref[...] = jnp.concatenate(
        [jnp.where(head0, o2[t][:BLOCK], o2[t][BLOCK:]) for t in tiles], axis=1).astype(o_ref.dtype)


def swa_sink_attention(p, B, LP, col_q, col_kv, q_gain, k_gain, sinks):
    nblk = LP // BLOCK
    kvw = 2 * KV_WIDTH
    qg = jnp.tile(q_gain, LANES // HEAD_DIM).reshape(1, LANES)
    kg = jnp.tile(k_gain, LANES // HEAD_DIM).reshape(1, LANES)
    return pl.pallas_call(
        _attn_kernel,
        grid=(B, nblk),
        in_specs=[
            pl.BlockSpec((BLOCK, ATTN_WIDTH), lambda b, n: (b * nblk + n, col_q // ATTN_WIDTH)),
            pl.BlockSpec((BLOCK, kvw), lambda b, n: (b * nblk + jnp.maximum(n - 1, 0), col_kv // kvw)),
            pl.BlockSpec((BLOCK, kvw), lambda b, n: (b * nblk + n, col_kv // kvw)),
            pl.BlockSpec((1, LANES), lambda b, n: (0, 0)),
            pl.BlockSpec((1, LANES), lambda b, n: (0, 0)),
            pl.BlockSpec(memory_space=pltpu.SMEM),
        ],
        out_specs=pl.BlockSpec((BLOCK, ATTN_WIDTH), lambda b, n: (b * nblk + n, 0)),
        out_shape=jax.ShapeDtypeStruct((B * LP, ATTN_WIDTH), jnp.bfloat16),
        compiler_params=pltpu.CompilerParams(
            dimension_semantics=("arbitrary", "arbitrary"), vmem_limit_bytes=VMEM_LIMIT_BYTES),
        name="swa_sink_attention",
    )(p, p, p, qg, kg, sinks)


def kernel(x, meta_tokens, norm1_g, w_in, shift_mu, w0, w_up, a0, a_up, g_up, k_k, k_a, r_k, gn_w, gn_b,
           q_gain, k_gain, sinks, w_out, norm2_g, peer_query, peer_sub_keys, peer_down, peer_up):
    B, S, D = x.shape
    assert w_in.shape[0] == 1, "single-layer stack"
    LP = S + BLOCK
    T = B * LP
    meta =jnp.broadcast_to(meta_tokens[None], (B, N_META, D))
    h = jnp.concatenate([jnp.zeros((B, PAD_LEFT, D), x.dtype), meta, x], axis=1)
    ht = h.reshape(T, D)

    w = w_in[0]
    n_lora = DECAY_LORA + AAA_LORA + GATE_LORA
    rwkv0 = ATTN_COLS
    lora0 = rwkv0 + 3 * RWKV_WIDTH
    w_in_b = jnp.concatenate([
        w[:, rwkv0:lora0], w[:, :ATTN_WIDTH],
        jnp.pad(w[:, lora0:lora0 + n_lora], ((0, 0), (0, LORA_PAD - n_lora))),
        w[:, ATTN_WIDTH:ATTN_COLS]], axis=1).astype(jnp.bfloat16)
    col_q = 3 * RWKV_WIDTH
    col_l = col_q + ATTN_WIDTH
    col_kv = col_l + LORA_PAD
    n_in = col_kv + 2 * KV_WIDTH
    p = norm_matmul(ht, norm1_g[0], w_in_b, MATMUL_ROWS, n_in // 2)

    y_rwkv = rwkv7_mix(p, B, LP, 0, RWKV_WIDTH, 2 * RWKV_WIDTH, col_l,
                       *rwkv_params(shift_mu[0], w0[0], w_up[0], a0[0], a_up[0], g_up[0], k_k[0], k_a[0], r_k[0],
                                    gn_w[0], gn_b[0]))
    y_attn = swa_sink_attention(p, B, LP, col_q, col_kv, q_gain[0], k_gain[0], sinks[0])
    h2 = matmul_residual(y_rwkv, y_attn, w_out[0].astype(jnp.bfloat16), ht, MATMUL_ROWS, MATMUL_COLS)

    pq = norm_matmul(h2, norm2_g[0], peer_query[0].astype(jnp.bfloat16), MATMUL_ROWS, MATMUL_COLS, jnp.bfloat16)
    keys = peer_sub_keys[0].reshape(2 * PEER_HEADS, N_KEYS, D_KEY // 2).astype(jnp.bfloat16)
    table = pack_expert_table(peer_down[0], peer_up[0])

    blocks_per_seq = LP // BLOCK
    kept_per_seq = blocks_per_seq - 1

    def kept_block(i):
        return (i // kept_per_seq) * blocks_per_seq + 1 + i % kept_per_seq

    idx_t, gate = peer_select(pq, keys, B * kept_per_seq, kept_block)
    seq_pos = (1 + jnp.arange(B * kept_per_seq, dtype=jnp.int32) % kept_per_seq) * BLOCK
    return peer_mix(idx_t, gate, h2, norm2_g[0], table, kept_block, seq_pos).reshape(B, S, D)
```

```python
import jax
import jax.numpy as jnp
from jax import lax
from jax.experimental import pallas as pl
from jax.experimental.pallas import tpu as pltpu

N_META = 16
BLOCK = 128
PAD_LEFT = BLOCK - N_META
HEAD_DIM = 64
RWKV_WIDTH = 1024
ATTN_WIDTH = 1024
ATTN_Q_HEADS = ATTN_WIDTH // HEAD_DIM
ATTN_KV_HEADS = 2
KV_WIDTH = ATTN_KV_HEADS * HEAD_DIM
WINDOW = 128
ATTN_SCALE = HEAD_DIM ** -0.5
MASK_VALUE = -1e30
DECAY_LORA = 64
AAA_LORA = 64
GATE_LORA = 160
RMS_EPS = 1e-6
GN_EPS = 64e-5
ATTN_COLS = ATTN_WIDTH + 2 * KV_WIDTH
PEER_HEADS = 8
N_KEYS = 128
PEER_TOPK = 16
D_KEY = 256

VMEM_LIMIT_BYTES = 48 * 1024 * 1024
MATMUL_ROWS = 512
MATMUL_COLS = 1024


def _norm_matmul_kernel(x_ref, g_ref, w_ref, o_ref):
    x = x_ref[...]
    ms = jnp.mean(x * x, axis=-1, keepdims=True)
    u = x * lax.rsqrt(ms + RMS_EPS) * g_ref[...]
    o_ref[...] = jnp.dot(
        u.astype(jnp.bfloat16), w_ref[...], preferred_element_type=jnp.float32).astype(o_ref.dtype)


def norm_matmul(x, g, w, tm, tn, out_dtype=jnp.float32):
    m, k = x.shape
    n = w.shape[1]
    return pl.pallas_call(
        _norm_matmul_kernel,
        grid=(n // tn, m // tm),
        in_specs=[
            pl.BlockSpec((tm, k), lambda j, i: (i, 0)),
            pl.BlockSpec((1, k), lambda j, i: (0, 0)),
            pl.BlockSpec((k, tn), lambda j, i: (0, j)),
        ],
        out_specs=pl.BlockSpec((tm, tn), lambda j, i: (i, j)),
        out_shape=jax.ShapeDtypeStruct((m, n), out_dtype),
        compiler_params=pltpu.CompilerParams(
            dimension_semantics=("arbitrary", "arbitrary"), vmem_limit_bytes=VMEM_LIMIT_BYTES),
        name="norm_matmul",
    )(x, g.reshape(1, k), w)


def _matmul_residual_kernel(xa_ref, xb_ref, w_ref, r_ref, o_ref):
    ka = xa_ref.shape[1]
    o_ref[...] = (r_ref[...]
                  + jnp.dot(xa_ref[...], w_ref[:ka, :], preferred_element_type=jnp.float32)
                  + jnp.dot(xb_ref[...], w_ref[ka:, :], preferred_element_type=jnp.float32))


def matmul_residual(xa, xb, w, r, tm, tn):
    m, ka = xa.shape
    kb = xb.shape[1]
    n = w.shape[1]
    return pl.pallas_call(
        _matmul_residual_kernel,
        grid=(n // tn, m // tm),
        in_specs=[
            pl.BlockSpec((tm, ka), lambda j, i: (i, 0)),
            pl.BlockSpec((tm, kb), lambda j, i: (i, 0)),
            pl.BlockSpec((ka + kb, tn), lambda j, i: (0, j)),
            pl.BlockSpec((tm, tn), lambda j, i: (i, j)),
        ],
        out_specs=pl.BlockSpec((tm, tn), lambda j, i: (i, j)),
        out_shape=jax.ShapeDtypeStruct((m, n), jnp.float32),
        compiler_params=pltpu.CompilerParams(
            dimension_semantics=("arbitrary", "arbitrary"), vmem_limit_bytes=VMEM_LIMIT_BYTES),
        name="matmul_residual",
    )(xa, xb, w, r)


RWKV_CHUNK = 64
RWKV_SEQS_PER_STEP = 2
LANES = 128
RWKV_TILES = RWKV_WIDTH // LANES
LORA_PAD = 512
GATE_PAD = 2 * LANES


def _dot(a, b, dims):
    return lax.dot_general(a, b, (dims, ((), ())), preferred_element_type=jnp.float32)


def _mm(a, b):
    return _dot(a, b, ((1,), (0,)))


def _mm_nt(a, b):
    return _dot(a, b, ((1,), (1,)))


def _mm_tn(a, b):
    return _dot(a, b, ((0,), (0,)))


def _iota2(shape, axis):
    return lax.broadcasted_iota(jnp.int32, shape, axis)


def _sigmoid(x):
    return 1.0 / (1.0 + jnp.exp(-x))


def _bf16_pieces(x, n):
    pieces = []
    for _ in range(n):
        p = x.astype(jnp.bfloat16)
        pieces.append(p)
        x = x - p.astype(jnp.float32)
    return pieces


def _token_shift(x, prev_ref, mu):
    rolled = pltpu.roll(x, 1, 0)
    prev = jnp.where(_iota2(x.shape, 0) == 0, prev_ref[...], rolled)
    prev_ref[...] = x[x.shape[0] - 1:, :]
    return x + mu * (prev - x)


def _rwkv_kernel(xr_ref, xk_ref, xv_ref, xl_ref, mu_ref, mul_ref, vec_ref, wup_ref, aup_ref, gup_ref,
                 o_ref, s_ref, pr_ref, pk_ref, pv_ref, pl_ref):
    C = RWKV_CHUNK
    NS = RWKV_SEQS_PER_STEP
    f32, bf16 = jnp.float32, jnp.bfloat16

    @pl.when(pl.program_id(1) == 0)
    def _():
        s_ref[...] = jnp.zeros_like(s_ref)
        pr_ref[...] = jnp.zeros_like(pr_ref)
        pk_ref[...] = jnp.zeros_like(pk_ref)
        pv_ref[...] = jnp.zeros_like(pv_ref)
        pl_ref[...] = jnp.zeros_like(pl_ref)

    def shifted(x_ref, prev_ref, mu):
        return jnp.concatenate([_token_shift(x_ref[i], prev_ref.at[i], mu) for i in range(NS)], axis=0)

    r = shifted(xr_ref, pr_ref, mu_ref[0:1, :])
    kraw = shifted(xk_ref, pk_ref, mu_ref[1:2, :])
    v = shifted(xv_ref, pv_ref, mu_ref[2:3, :])
    xl = shifted(xl_ref, pl_ref, mul_ref[...])

    w0, a0, k_k, k_a = vec_ref[0:1, :], vec_ref[1:2, :], vec_ref[2:3, :], vec_ref[3:4, :]
    r_k, gn_w, gn_b = vec_ref[4:5, :], vec_ref[5:6, :], vec_ref[6:7, :]

    x_wa = xl[:, :LANES]
    wl = w0 + _mm(jnp.tanh(x_wa).astype(jnp.bfloat16), wup_ref[...])
    z = -wl
    softplus = jnp.maximum(z, 0.0) + jnp.log(1.0 + jnp.exp(-jnp.abs(z)))
    lw = -jnp.exp(-softplus - 0.5)
    a = _sigmoid(a0 + _mm(x_wa.astype(jnp.bfloat16), aup_ref[...]))
    g = _mm(_sigmoid(xl[:, LANES:LANES + GATE_PAD]).astype(jnp.bfloat16), gup_ref[...])

    tr, tc = _iota2((NS * C, NS * C), 0), _iota2((NS * C, NS * C), 1)
    tri = ((tr >= tc) & ((tr & -C) == (tc & -C))).astype(bf16)
    cl = sum(_mm(tri, piece) for piece in _bf16_pieces(lw, 3))
    e_pos = jnp.exp(cl)
    e_excl = jnp.exp(cl - lw)
    e_neg = 1.0 / e_pos
    e_end = [e_pos[(i + 1) * C - 1:(i + 1) * C, :] for i in range(NS)]
    e_tail = jnp.concatenate([e_end[i] * e_neg[i * C:(i + 1) * C, :] for i in range(NS)], axis=0)

    lane = _iota2((LANES, LANES), 1)
    row = _iota2((LANES, LANES), 0)
    same_head = (lane >= HEAD_DIM) == (row >= HEAD_DIM)
    seg = same_head.astype(bf16)
    eye = (lane == row).astype(f32)
    t_row, t_col = row & (C - 1), lane & (C - 1)
    strict = t_col < t_row
    in_block4 = (t_row >> 2) == (t_col >> 2)
    in_block16 = (t_row >> 4) == (t_col >> 4)
    incl = t_col <= t_row
    incl2 = jnp.concatenate([incl, incl], axis=1)
    lane_c = _iota2((C, LANES), 1)
    head0 = lane_c < HEAD_DIM

    def seg_sum(x):
        return sum(_mm(piece, seg) for piece in _bf16_pieces(x, 2))

    def by_head(x):
        return jnp.concatenate([jnp.where(head0, x, 0.0), jnp.where(head0, 0.0, x)], axis=0).astype(bf16)

    def twice(x):
        return jnp.concatenate([x, x], axis=0)

    tiles = range(NS * RWKV_TILES)
    sls = [slice((j % RWKV_TILES) * LANES, (j % RWKV_TILES + 1) * LANES) for j in tiles]
    seq = [j // RWKV_TILES for j in tiles]

    def tile(x, j):
        return x[seq[j] * C:(seq[j] + 1) * C, sls[j]]

    kr = [tile(kraw, j) for j in tiles]
    kkr = [kr[j] * k_k[:, sls[j]] for j in tiles]
    kk_ss = [seg_sum(kkr[j] * kkr[j]) for j in tiles]
    kk = [kkr[j] / jnp.maximum(jnp.sqrt(kk_ss[j]), 1e-12) for j in tiles]
    a_t = [tile(a, j) for j in tiles]
    k2 = [kr[j] * (1.0 + (a_t[j] - 1.0) * k_a[:, sls[j]]) for j in tiles]
    r_t = [tile(r, j) for j in tiles]
    v_t = [tile(v, j) for j in tiles]
    kka = [kk[j] * a_t[j] for j in tiles]
    alpha = [-kk[j] * tile(e_excl, j) for j in tiles]
    r_dec = [r_t[j] * tile(e_pos, j) for j in tiles]
    beta = [kka[j] * tile(e_neg, j) for j in tiles]
    k_neg = [k2[j] * tile(e_neg, j) for j in tiles]
    s_old = [s_ref[j] for j in tiles]

    gram = [_mm_nt(jnp.concatenate([by_head(alpha[j]), by_head(r_dec[j])], axis=0),
                   jnp.concatenate([by_head(beta[j]), by_head(k_neg[j])], axis=0)) for j in tiles]
    m1 = [_mm_nt(jnp.concatenate([alpha[j], r_dec[j]], axis=0).astype(bf16), s_old[j].astype(bf16))
          for j in tiles]
    a_s = [jnp.where(strict, gram[j][:LANES, :LANES], 0.0) for j in tiles]
    b_s = [jnp.where(strict, gram[j][:LANES, LANES:], 0.0).astype(bf16) for j in tiles]
    r_i = [jnp.where(incl2, gram[j][LANES:, :], 0.0).astype(bf16) for j in tiles]
    vv = [twice(v_t[j]).astype(bf16) for j in tiles]
    rhs = [twice(m1[j][:C]) + _mm(b_s[j], vv[j]) for j in tiles]
    def mmb(x, y):
        return _mm(x.astype(bf16), y.astype(bf16))

    def inv_index4(nil):
        sq = [mmb(nil[j], nil[j]) for j in tiles]
        return [eye + nil[j] + sq[j] + mmb(sq[j], nil[j]) for j in tiles]

    t0 = inv_index4([jnp.where(in_block4, a_s[j], 0.0) for j in tiles])
    n1 = [mmb(t0[j], jnp.where(in_block16 & ~in_block4, a_s[j], 0.0)) for j in tiles]
    p1 = inv_index4(n1)
    t1 = [mmb(p1[j], t0[j]) for j in tiles]
    n2 = [mmb(t1[j], jnp.where(in_block16, 0.0, a_s[j])) for j in tiles]
    p2 = inv_index4(n2)
    t1_rhs = [mmb(t1[j], rhs[j]) for j in tiles]
    u_st = [mmb(p2[j], t1_rhs[j]) for j in tiles]
    y_st = [twice(m1[j][C:]) + _mm(r_i[j], jnp.concatenate([u_st[j].astype(bf16), vv[j]], axis=0))
            for j in tiles]
    u_t = [jnp.where(head0, u_st[j][:C], u_st[j][C:]) for j in tiles]
    y_t = [jnp.where(head0, y_st[j][:C], y_st[j][C:]) for j in tiles]

    upd = [_mm_tn(jnp.concatenate([u_t[j], v_t[j]], axis=0).astype(bf16),
                  jnp.concatenate([kka[j] * tile(e_tail, j), k2[j] * tile(e_tail, j)], axis=0).astype(bf16))
           for j in tiles]
    for j in tiles:
        s_ref[j] = s_old[j] * e_end[seq[j]][:, sls[j]] + jnp.where(same_head, upd[j], 0.0)

    mu_y = [seg_sum(y_t[j]) * (1.0 / HEAD_DIM) for j in tiles]
    dev = [y_t[j] - mu_y[j] for j in tiles]
    var = [seg_sum(dev[j] * dev[j]) * (1.0 / HEAD_DIM) for j in tiles]
    bonus = [seg_sum(r_t[j] * k2[j] * r_k[:, sls[j]]) * v_t[j] for j in tiles]
    outs = [(dev[j] * lax.rsqrt(var[j] + GN_EPS) * gn_w[:, sls[j]] + gn_b[:, sls[j]] + bonus[j]) * tile(g, j)
            for j in tiles]
    for i in range(NS):
        o_ref[i] = jnp.concatenate(outs[i * RWKV_TILES:(i + 1) * RWKV_TILES], axis=1).astype(o_ref.dtype)


def rwkv_params(shift_mu, w0, w_up, a0, a_up, g_up, k_k, k_a, r_k, gn_w, gn_b):
    W = RWKV_WIDTH
    n_lora = DECAY_LORA + AAA_LORA + GATE_LORA
    mu_rkv = shift_mu[:3 * W].reshape(3, W)
    mu_l = jnp.pad(shift_mu[3 * W:], (0, LORA_PAD - n_lora)).reshape(1, LORA_PAD)
    vecs = jnp.stack([w0, a0, k_k, k_a, r_k.reshape(W), gn_w, gn_b, jnp.zeros_like(w0)])
    w_up_p = jnp.pad(w_up, ((0, LANES - DECAY_LORA), (0, 0))).astype(jnp.bfloat16)
    a_up_p = jnp.pad(a_up, ((DECAY_LORA, LANES - DECAY_LORA - AAA_LORA), (0, 0))).astype(jnp.bfloat16)
    g_up_p = jnp.pad(g_up, ((0, GATE_PAD - GATE_LORA), (0, 0))).astype(jnp.bfloat16)
    return mu_rkv, mu_l, vecs, w_up_p, a_up_p, g_up_p


def rwkv7_mix(p, B, LP, col_r, col_k, col_v, col_l, mu_rkv, mu_l, vecs, w_up, a_up, g_up):
    C = RWKV_CHUNK
    NS = RWKV_SEQS_PER_STEP
    assert B % NS == 0 and LP % C == 0, (B, LP)
    nc = LP // C
    W = RWKV_WIDTH
    p3 = p.reshape(B, LP, p.shape[1])

    def col_spec(width, col):
        return pl.BlockSpec((NS, C, width), lambda b, c: (b, c, col // width))

    def full(shape):
        return pl.BlockSpec(shape, lambda b, c: (0,) * len(shape))

    def carried_row(width):
        return pltpu.VMEM((NS, 1, width), jnp.float32)

    out = pl.pallas_call(
        _rwkv_kernel,
        grid=(B // NS, nc),
        in_specs=[
            col_spec(W, col_r), col_spec(W, col_k), col_spec(W, col_v), col_spec(LORA_PAD, col_l),
            full(mu_rkv.shape), full(mu_l.shape), full(vecs.shape),
            full(w_up.shape), full(a_up.shape), full(g_up.shape),
        ],
        out_specs=pl.BlockSpec((NS, C, W), lambda b, c: (b, c, 0)),
        out_shape=jax.ShapeDtypeStruct((B, LP, W), jnp.bfloat16),
        scratch_shapes=[
            pltpu.VMEM((NS * RWKV_TILES, LANES, LANES), jnp.float32),
            carried_row(W), carried_row(W), carried_row(W), carried_row(LORA_PAD),
        ],
        compiler_params=pltpu.CompilerParams(
            dimension_semantics=("arbitrary", "arbitrary"), vmem_limit_bytes=VMEM_LIMIT_BYTES),
        name="rwkv7_mix",
    )(p3, p3, p3, p3, mu_rkv, mu_l, vecs, w_up, a_up, g_up)
    return out.reshape(B * LP, W)


PEER_PICKS = PEER_HEADS * PEER_TOPK
PEER_TOKENS_PER_STEP = 128
PEER_GROUP = 4
PEER_GROUPS_PER_STEP = PEER_TOKENS_PER_STEP // PEER_GROUP
PEER_BUFFERS = 4
EXPERT_TILE_ROWS = 16
EXPERT_TILE_PITCH = 20
assert PEER_GROUPS_PER_STEP % PEER_BUFFERS == 0


PACK_ROWS_PER_STEP = 256


def _pack_kernel(down_ref, up_ref, o_ref):
    def bf16_bits_high(t):
        return lax.bitcast_convert_type(t.astype(jnp.bfloat16).astype(jnp.float32), jnp.uint32)
    words = bf16_bits_high(down_ref[...]) | (bf16_bits_high(up_ref[...]) >> 16)
    o_ref[...] = words.reshape(o_ref.shape)


def pack_expert_table(down, up):
    E, D = down.shape
    r = PACK_ROWS_PER_STEP
    return pl.pallas_call(
        _pack_kernel,
        grid=(E // r,),
        in_specs=[pl.BlockSpec((r, D), lambda i: (i, 0)), pl.BlockSpec((r, D), lambda i: (i, 0))],
        out_specs=pl.BlockSpec((r, EXPERT_TILE_ROWS, D // EXPERT_TILE_ROWS), lambda i: (i, 0, 0)),
        out_shape=jax.ShapeDtypeStruct((E, EXPERT_TILE_ROWS, D // EXPERT_TILE_ROWS), jnp.uint32),
        compiler_params=pltpu.CompilerParams(
            dimension_semantics=("arbitrary",), vmem_limit_bytes=VMEM_LIMIT_BYTES),
        name="pack_expert_table",
    )(down, up)


def _peer_mix_kernel(idx_ref, idx_next_ref, gate_t_ref, res_ref, g_ref, pos_ref, tab_ref, o_ref, *scratch):
    bufs, sem, x_ref = scratch[:PEER_BUFFERS], scratch[PEER_BUFFERS], scratch[PEER_BUFFERS + 1]
    G = PEER_GROUP
    ahead = PEER_BUFFERS - 1
    block = pl.program_id(0)
    gate_t = gate_t_ref[0]
    tok_lane = _iota2(gate_t.shape, 1)

    def issue_token(picks_ref, group, s, b):
        for j in range(PEER_PICKS):
            e = picks_ref[0, j, group * G + s]
            tile_rows = pl.ds(j * EXPERT_TILE_PITCH, EXPERT_TILE_ROWS)
            pltpu.make_async_copy(tab_ref.at[e], bufs[b].at[s, tile_rows, :], sem.at[b]).start(priority=j % 2)

    def wait_group(b):
        view = bufs[b].at[:, pl.ds(0, PEER_PICKS * EXPERT_TILE_ROWS), :]
        pltpu.make_async_copy(view, view, sem.at[b]).wait()

    def words(b, s, r):
        return bufs[b][s, pl.ds(r, PEER_PICKS, stride=EXPERT_TILE_PITCH), :]

    def mix_token(group, s, b):
        tt = group * G + s
        seg_w = x_ref.shape[1] // EXPERT_TILE_ROWS
        x_row = x_ref[pl.ds(tt, 1), :]
        acc = jnp.zeros((PEER_PICKS, seg_w), jnp.float32)
        for r in range(EXPERT_TILE_ROWS):
            down = lax.bitcast_convert_type(words(b, s, r) & jnp.uint32(0xFFFF0000), jnp.float32)
            acc = acc + down * x_row[:, r * seg_w:(r + 1) * seg_w]
        h = jnp.sum(acc, axis=1, keepdims=True)
        gate = jnp.sum(jnp.where(tok_lane == tt, gate_t, 0.0), axis=1, keepdims=True)
        w = 0.5 * h * (1.0 + lax.erf(h * (2.0 ** -0.5))) * gate
        mix = [jnp.sum(w * lax.bitcast_convert_type(words(b, s, r) << 16, jnp.float32), axis=0, keepdims=True)
               for r in range(EXPERT_TILE_ROWS)]
        out = res_ref[pl.ds(tt, 1), :] + jnp.concatenate(mix, axis=1)
        o_ref[pl.ds(tt, 1), :] = jnp.where(pos_ref[block] + tt >= PAD_LEFT, out, 0.0)

    def step(group, b):
        wait_group(b)
        target = (b + ahead) % PEER_BUFFERS

        def token_pair(i, carry):
            for s in (2 * i, 2 * i + 1):
                if isinstance(group, int) and group + ahead >= PEER_GROUPS_PER_STEP:
                    @pl.when(block + 1 < pl.num_programs(0))
                    def _():
                        issue_token(idx_next_ref, group + ahead - PEER_GROUPS_PER_STEP, s, target)
                else:
                    issue_token(idx_ref, group + ahead, s, target)
            for s in (2 * i, 2 * i + 1):
                mix_token(group, s, b)
            return carry

        lax.fori_loop(0, G // 2, token_pair, 0)

    @pl.when(block == 0)
    def _():
        for g in range(ahead):
            for s in range(G):
                issue_token(idx_ref, g, s, g)

    res = res_ref[...]
    x_ref[...] = res * lax.rsqrt(jnp.mean(res * res, axis=1, keepdims=True) + RMS_EPS) * g_ref[...]
    n_main = (PEER_GROUPS_PER_STEP - ahead) // PEER_BUFFERS

    def rotation(it, carry):
        for b in range(PEER_BUFFERS):
            step(it * PEER_BUFFERS + b, b)
        return carry

    lax.fori_loop(0, n_main, rotation, 0)
    for g in range(n_main * PEER_BUFFERS, PEER_GROUPS_PER_STEP):
        step(g, g % PEER_BUFFERS)


def peer_mix(idx, gate, res, norm_g, table, src_block, seq_pos):
    D = res.shape[1]
    n = idx.shape[0]
    tb = PEER_TOKENS_PER_STEP
    tile = (EXPERT_TILE_ROWS, D // EXPERT_TILE_ROWS)
    return pl.pallas_call(
        _peer_mix_kernel,
        grid=(n,),
        in_specs=[
            pl.BlockSpec((1, PEER_PICKS, tb), lambda i: (i, 0, 0), memory_space=pltpu.SMEM),
            pl.BlockSpec((1, PEER_PICKS, tb), lambda i: (jnp.minimum(i + 1, n - 1), 0, 0),
                         memory_space=pltpu.SMEM),
            pl.BlockSpec((1, PEER_PICKS, tb), lambda i: (i, 0, 0)),
            pl.BlockSpec((tb, D), lambda i: (src_block(i), 0)),
            pl.BlockSpec((1, D), lambda i: (0, 0)),
            pl.BlockSpec(memory_space=pltpu.SMEM),
            pl.BlockSpec(memory_space=pl.ANY),
        ],
        out_specs=pl.BlockSpec((tb, D), lambda i: (i, 0)),
        out_shape=jax.ShapeDtypeStruct((n * tb, D), jnp.float32),
        scratch_shapes=[pltpu.VMEM((PEER_GROUP, PEER_PICKS * EXPERT_TILE_PITCH, tile[1]), jnp.uint32)
                        for _ in range(PEER_BUFFERS)]
        + [pltpu.SemaphoreType.DMA((PEER_BUFFERS,)), pltpu.VMEM((tb, D), jnp.float32)],
        compiler_params=pltpu.CompilerParams(
            dimension_semantics=("arbitrary",), vmem_limit_bytes=VMEM_LIMIT_BYTES),
        name="peer_mix",
    )(idx, idx, gate, res, norm_g.reshape(1, D), seq_pos, table)


def _top_rows(s, k, codes):
    vals, poss = [], []
    for _ in range(k):
        m = jnp.max(s, axis=0, keepdims=True)
        pos = jnp.min(jnp.where(s == m, codes, jnp.inf), axis=0, keepdims=True)
        vals.append(m)
        poss.append(pos)
        s = jnp.where(codes == pos, -jnp.inf, s)
    return jnp.concatenate(vals, axis=0), jnp.concatenate(poss, axis=0)


def _candidate_pairs(k):
    return [(i, j) for i in range(k) for j in range(k) if (i + 1) * (j + 1) <= k]


def candidate_selectors():
    pairs = _candidate_pairs(PEER_TOPK)
    n_rows = -(-len(pairs) // 8) * 8
    first = jnp.zeros((n_rows, PEER_TOPK), jnp.float32).at[
        jnp.arange(len(pairs)), jnp.array([i for i, _ in pairs])].set(1.0)
    second = jnp.zeros((n_rows, PEER_TOPK), jnp.float32).at[
        jnp.arange(len(pairs)), jnp.array([j for _, j in pairs])].set(1.0)
    return first.astype(jnp.bfloat16), second.astype(jnp.bfloat16)


def _pick_rows(table, sel):
    out = jnp.zeros_like(sel)
    for r in range(table.shape[0]):
        out = out + jnp.where(sel == float(r), table[r:r + 1, :], 0.0)
    return out


def _peer_select_kernel(q_ref, keys_ref, sel1_ref, sel2_ref, idx_ref, gate_ref):
    K = PEER_TOPK
    half_w = D_KEY // 2
    tokens = q_ref.shape[0]
    key_rows = _iota2((N_KEYS, tokens), 0).astype(jnp.float32)
    sel1, sel2 = sel1_ref[...], sel2_ref[...]
    n_cand = len(_candidate_pairs(K))
    rank = _iota2((K, tokens), 0).astype(jnp.bfloat16)
    cand_rows = _iota2((sel1.shape[0], tokens), 0)
    cand_code = jnp.where(cand_rows < n_cand, _mm(sel1, rank) * K + _mm(sel2, rank), float(K * K))

    def pick_exact(sel, x):
        return sum(_mm(sel, piece) for piece in _bf16_pieces(x, 3))

    idx_rows, gate_rows = [], []
    for h in range(PEER_HEADS):
        tops = []
        for c in range(2):
            hc = 2 * h + c
            scores = _mm_nt(keys_ref[hc], q_ref[:, hc * half_w:(hc + 1) * half_w])
            tops.append(_top_rows(scores, K, key_rows))
        (s1, i1), (s2, i2) = tops
        cand = jnp.where(cand_rows < n_cand, pick_exact(sel1, s1) + pick_exact(sel2, s2), -jnp.inf)
        top_s, pos = _top_rows(cand, K, cand_code)
        first = jnp.floor(pos * (1.0 / K))
        second = pos - first * K
        expert = _pick_rows(i1, first) * N_KEYS + _pick_rows(i2, second)
        e = jnp.exp(top_s - top_s[0:1, :])
        idx_rows.append(expert.astype(jnp.int32))
        gate_rows.append(e / jnp.sum(e, axis=0, keepdims=True))
    idx_ref[0] = jnp.concatenate(idx_rows, axis=0)
    gate_ref[0] = jnp.concatenate(gate_rows, axis=0)


def peer_select(q, keys, n, src_block):
    W = q.shape[1]
    tb = PEER_TOKENS_PER_STEP
    out_block = pl.BlockSpec((1, PEER_PICKS, tb), lambda i: (i, 0, 0))
    sel1, sel2 = candidate_selectors()
    return pl.pallas_call(
        _peer_select_kernel,
        grid=(n,),
        in_specs=[pl.BlockSpec((tb, W), lambda i: (src_block(i), 0)),
                  pl.BlockSpec(keys.shape, lambda i: (0, 0, 0)),
                  pl.BlockSpec(sel1.shape, lambda i: (0, 0)),
                  pl.BlockSpec(sel2.shape, lambda i: (0, 0))],
        out_specs=[out_block, out_block],
        out_shape=[jax.ShapeDtypeStruct((n, PEER_PICKS, tb), jnp.int32),
                   jax.ShapeDtypeStruct((n, PEER_PICKS, tb), jnp.float32)],
        compiler_params=pltpu.CompilerParams(
            dimension_semantics=("arbitrary",), vmem_limit_bytes=VMEM_LIMIT_BYTES),
        name="peer_select",
    )(q, keys, sel1, sel2)


ATTN_TILES = ATTN_WIDTH // LANES
Q_HEADS_PER_KV = ATTN_Q_HEADS // ATTN_KV_HEADS


def _attn_kernel(q_ref, kvp_ref, kvc_ref, qg_ref, kg_ref, sink_ref, o_ref):
    f32, bf16 = jnp.float32, jnp.bfloat16
    nb = pl.program_id(1)
    lane = _iota2((LANES, LANES), 1)
    row = _iota2((LANES, LANES), 0)
    seg = ((lane >= HEAD_DIM) == (row >= HEAD_DIM)).astype(bf16)

    def head_rms(x, gain):
        ms = sum(_mm(piece, seg) for piece in _bf16_pieces(x * x, 2)) * (1.0 / HEAD_DIM)
        return x * lax.rsqrt(ms + RMS_EPS) * gain

    kv = jnp.concatenate([kvp_ref[...], kvc_ref[...]], axis=0)
    k = head_rms(kv[:, :LANES], kg_ref[...])
    v = kv[:, LANES:]
    lane_k = _iota2(k.shape, 1)
    k_sw, v_sw = pltpu.roll(k, HEAD_DIM, 1), pltpu.roll(v, HEAD_DIM, 1)
    k_dup = [jnp.where(lane_k < HEAD_DIM, k, k_sw).astype(bf16), jnp.where(lane_k < HEAD_DIM, k_sw, k).astype(bf16)]
    v_dup = [jnp.where(lane_k < HEAD_DIM, v, v_sw).astype(bf16), jnp.where(lane_k < HEAD_DIM, v_sw, v).astype(bf16)]

    qi = _iota2((2 * BLOCK, 2 * BLOCK), 0) & (BLOCK - 1)
    ks = _iota2((2 * BLOCK, 2 * BLOCK), 1)
    dist = qi + BLOCK - ks
    key_pos = (nb - 1) * BLOCK + ks
    allowed = (dist >= 0) & (dist < WINDOW) & (key_pos >= PAD_LEFT)
    upper = _iota2((2 * BLOCK, 1), 0) < BLOCK
    head0 = _iota2((BLOCK, LANES), 1) < HEAD_DIM

    tiles = range(ATTN_TILES)
    q_t = [head_rms(q_ref[:, t * LANES:(t + 1) * LANES], qg_ref[...]) for t in tiles]
    q2 = [jnp.concatenate([jnp.where(head0, q_t[t], 0.0), jnp.where(head0, 0.0, q_t[t])], axis=0).astype(bf16)
          for t in tiles]
    s = [_mm_nt(q2[t], k_dup[(2 * t) // Q_HEADS_PER_KV]) * ATTN_SCALE for t in tiles]
    s = [jnp.where(allowed, s[t], MASK_VALUE) for t in tiles]
    sink = [jnp.where(upper, sink_ref[2 * t], sink_ref[2 * t + 1]) for t in tiles]
    m = [jnp.maximum(jnp.max(s[t], axis=1, keepdims=True), sink[t]) for t in tiles]
    e = [jnp.exp(s[t] - m[t]) for t in tiles]
    denom = [jnp.sum(e[t], axis=1, keepdims=True) + jnp.exp(sink[t] - m[t]) for t in tiles]
    p = [(e[t] / denom[t]).astype(bf16) for t in tiles]
    o2 = [_mm(p[t], v_dup[(2 * t) // Q_HEADS_PER_KV]) for t in tiles]
    o_ref[...] = jnp.concatenate(
        [jnp.where(head0, o2[t][:BLOCK], o2[t][BLOCK:]) for t in tiles], axis=1).astype(o_ref.dtype)


def swa_sink_attention(p, B, LP, col_q, col_kv, q_gain, k_gain, sinks):
    nblk = LP // BLOCK
    kvw = 2 * KV_WIDTH
    qg = jnp.tile(q_gain, LANES // HEAD_DIM).reshape(1, LANES)
    kg = jnp.tile(k_gain, LANES // HEAD_DIM).reshape(1, LANES)
    return pl.pallas_call(
        _attn_kernel,
        grid=(B, nblk),
        in_specs=[
            pl.BlockSpec((BLOCK, ATTN_WIDTH), lambda b, n: (b * nblk + n, col_q // ATTN_WIDTH)),
            pl.BlockSpec((BLOCK, kvw), lambda b, n: (b * nblk + jnp.maximum(n - 1, 0), col_kv // kvw)),
            pl.BlockSpec((BLOCK, kvw), lambda b, n: (b * nblk + n, col_kv // kvw)),
            pl.BlockSpec((1, LANES), lambda b, n: (0, 0)),
            pl.BlockSpec((1, LANES), lambda b, n: (0, 0)),
            pl.BlockSpec(memory_space=pltpu.SMEM),
        ],
        out_specs=pl.BlockSpec((BLOCK, ATTN_WIDTH), lambda b, n: (b * nblk + n, 0)),
        out_shape=jax.ShapeDtypeStruct((B * LP, ATTN_WIDTH), jnp.bfloat16),
        compiler_params=pltpu.CompilerParams(
            dimension_semantics=("arbitrary", "arbitrary"), vmem_limit_bytes=VMEM_LIMIT_BYTES),
        name="swa_sink_attention",
    )(p, p, p, qg, kg, sinks)


def kernel(x, meta_tokens, norm1_g, w_in, shift_mu, w0, w_up, a0, a_up, g_up, k_k, k_a, r_k, gn_w, gn_b,
           q_gain, k_gain, sinks, w_out, norm2_g, peer_query, peer_sub_keys, peer_down, peer_up):
    B, S, D = x.shape
    assert w_in.shape[0] == 1, "single-layer stack"
    LP = S + BLOCK
    T = B * LP
    meta =jnp.broadcast_to(meta_tokens[None], (B, N_META, D))
    h = jnp.concatenate([jnp.zeros((B, PAD_LEFT, D), x.dtype), meta, x], axis=1)
    ht = h.reshape(T, D)

    w = w_in[0]
    n_lora = DECAY_LORA + AAA_LORA + GATE_LORA
    rwkv0 = ATTN_COLS
    lora0 = rwkv0 + 3 * RWKV_WIDTH
    w_in_b = jnp.concatenate([
        w[:, rwkv0:lora0], w[:, :ATTN_WIDTH],
        jnp.pad(w[:, lora0:lora0 + n_lora], ((0, 0), (0, LORA_PAD - n_lora))),
        w[:, ATTN_WIDTH:ATTN_COLS]], axis=1).astype(jnp.bfloat16)
    col_q = 3 * RWKV_WIDTH
    col_l = col_q + ATTN_WIDTH
    col_kv = col_l + LORA_PAD
    n_in = col_kv + 2 * KV_WIDTH
    p = norm_matmul(ht, norm1_g[0], w_in_b, MATMUL_ROWS, n_in // 2)

    y_rwkv = rwkv7_mix(p, B, LP, 0, RWKV_WIDTH, 2 * RWKV_WIDTH, col_l,
                       *rwkv_params(shift_mu[0], w0[0], w_up[0], a0[0], a_up[0], g_up[0], k_k[0], k_a[0], r_k[0],
                                    gn_w[0], gn_b[0]))
    y_attn = swa_sink_attention(p, B, LP, col_q, col_kv, q_gain[0], k_gain[0], sinks[0])
    h2 = matmul_residual(y_rwkv, y_attn, w_out[0].astype(jnp.bfloat16), ht, MATMUL_ROWS, MATMUL_COLS)

    pq = norm_matmul(h2, norm2_g[0], peer_query[0].astype(jnp.bfloat16), MATMUL_ROWS, MATMUL_COLS, jnp.bfloat16)
    keys = peer_sub_keys[0].reshape(2 * PEER_HEADS, N_KEYS, D_KEY // 2).astype(jnp.bfloat16)
    table = pack_expert_table(peer_down[0], peer_up[0])

    blocks_per_seq = LP // BLOCK
    kept_per_seq = blocks_per_seq - 1

    def kept_block(i):
        return (i // kept_per_seq) * blocks_per_seq + 1 + i % kept_per_seq

    idx_t, gate = peer_select(pq, keys, B * kept_per_seq, kept_block)
    seq_pos = (1 + jnp.arange(B * kept_per_seq, dtype=jnp.int32) % kept_per_seq) * BLOCK
    return peer_mix(idx_t, gate, h2, norm2_g[0], table, kept_block, seq_pos).reshape(B, S, D)
```

```python
import jax
import jax.numpy as jnp
from jax import lax
from jax.experimental import pallas as pl
from jax.experimental.pallas import tpu as pltpu

N_META = 16
BLOCK = 128
PAD_LEFT = BLOCK - N_META
HEAD_DIM = 64
RWKV_WIDTH = 1024
ATTN_WIDTH = 1024
ATTN_Q_HEADS = ATTN_WIDTH // HEAD_DIM
ATTN_KV_HEADS = 2
KV_WIDTH = ATTN_KV_HEADS * HEAD_DIM
WINDOW = 128
ATTN_SCALE = HEAD_DIM ** -0.5
MASK_VALUE = -1e30
DECAY_LORA = 64
AAA_LORA = 64
GATE_LORA = 160
RMS_EPS = 1e-6
GN_EPS = 64e-5
ATTN_COLS = ATTN_WIDTH + 2 * KV_WIDTH
PEER_HEADS = 8
N_KEYS = 128
PEER_TOPK = 16
D_KEY = 256

VMEM_LIMIT_BYTES = 48 * 1024 * 1024
MATMUL_ROWS = 512
MATMUL_COLS = 1024


def _norm_matmul_kernel(x_ref, g_ref, w_ref, o_ref):
    x = x_ref[...]
    ms = jnp.mean(x * x, axis=-1, keepdims=True)
    u = x * lax.rsqrt(ms + RMS_EPS) * g_ref[...]
    o_ref[...] = jnp.dot(
        u.astype(jnp.bfloat16), w_ref[...], preferred_element_type=jnp.float32).astype(o_ref.dtype)


def norm_matmul(x, g, w, tm, tn, out_dtype=jnp.float32):
    m, k = x.shape
    n = w.shape[1]
    return pl.pallas_call(
        _norm_matmul_kernel,
        grid=(n // tn, m // tm),
        in_specs=[
            pl.BlockSpec((tm, k), lambda j, i: (i, 0)),
            pl.BlockSpec((1, k), lambda j, i: (0, 0)),
            pl.BlockSpec((k, tn), lambda j, i: (0, j)),
        ],
        out_specs=pl.BlockSpec((tm, tn), lambda j, i: (i, j)),
        out_shape=jax.ShapeDtypeStruct((m, n), out_dtype),
        compiler_params=pltpu.CompilerParams(
            dimension_semantics=("arbitrary", "arbitrary"), vmem_limit_bytes=VMEM_LIMIT_BYTES),
        name="norm_matmul",
    )(x, g.reshape(1, k), w)


def _matmul_residual_kernel(xa_ref, xb_ref, w_ref, r_ref, o_ref):
    ka = xa_ref.shape[1]
    o_ref[...] = (r_ref[...]
                  + jnp.dot(xa_ref[...], w_ref[:ka, :], preferred_element_type=jnp.float32)
                  + jnp.dot(xb_ref[...], w_ref[ka:, :], preferred_element_type=jnp.float32))


def matmul_residual(xa, xb, w, r, tm, tn):
    m, ka = xa.shape
    kb = xb.shape[1]
    n = w.shape[1]
    return pl.pallas_call(
        _matmul_residual_kernel,
        grid=(n // tn, m // tm),
        in_specs=[
            pl.BlockSpec((tm, ka), lambda j, i: (i, 0)),
            pl.BlockSpec((tm, kb), lambda j, i: (i, 0)),
            pl.BlockSpec((ka + kb, tn), lambda j, i: (0, j)),
            pl.BlockSpec((tm, tn), lambda j, i: (i, j)),
        ],
        out_specs=pl.BlockSpec((tm, tn), lambda j, i: (i, j)),
        out_shape=jax.ShapeDtypeStruct((m, n), jnp.float32),
        compiler_params=pltpu.CompilerParams(
            dimension_semantics=("arbitrary", "arbitrary"), vmem_limit_bytes=VMEM_LIMIT_BYTES),
        name="matmul_residual",
    )(xa, xb, w, r)


RWKV_CHUNK = 64
RWKV_SEQS_PER_STEP = 2
LANES = 128
RWKV_TILES = RWKV_WIDTH // LANES
LORA_PAD = 512
GATE_PAD = 2 * LANES


def _dot(a, b, dims):
    return lax.dot_general(a, b, (dims, ((), ())), preferred_element_type=jnp.float32)


def _mm(a, b):
    return _dot(a, b, ((1,), (0,)))


def _mm_nt(a, b):
    return _dot(a, b, ((1,), (1,)))


def _mm_tn(a, b):
    return _dot(a, b, ((0,), (0,)))


def _iota2(shape, axis):
    return lax.broadcasted_iota(jnp.int32, shape, axis)


def _sigmoid(x):
    return 1.0 / (1.0 + jnp.exp(-x))


def _bf16_pieces(x, n):
    pieces = []
    for _ in range(n):
        p = x.astype(jnp.bfloat16)
        pieces.append(p)
        x = x - p.astype(jnp.float32)
    return pieces


def _token_shift(x, prev_ref, mu):
    rolled = pltpu.roll(x, 1, 0)
    prev = jnp.where(_iota2(x.shape, 0) == 0, prev_ref[...], rolled)
    prev_ref[...] = x[x.shape[0] - 1:, :]
    return x + mu * (prev - x)


def _rwkv_kernel(xr_ref, xk_ref, xv_ref, xl_ref, mu_ref, mul_ref, vec_ref, wup_ref, aup_ref, gup_ref,
                 o_ref, s_ref, pr_ref, pk_ref, pv_ref, pl_ref):
    C = RWKV_CHUNK
    NS = RWKV_SEQS_PER_STEP
    f32, bf16 = jnp.float32, jnp.bfloat16

    @pl.when(pl.program_id(1) == 0)
    def _():
        s_ref[...] = jnp.zeros_like(s_ref)
        pr_ref[...] = jnp.zeros_like(pr_ref)
        pk_ref[...] = jnp.zeros_like(pk_ref)
        pv_ref[...] = jnp.zeros_like(pv_ref)
        pl_ref[...] = jnp.zeros_like(pl_ref)

    def shifted(x_ref, prev_ref, mu):
        return jnp.concatenate([_token_shift(x_ref[i], prev_ref.at[i], mu) for i in range(NS)], axis=0)

    r = shifted(xr_ref, pr_ref, mu_ref[0:1, :])
    kraw = shifted(xk_ref, pk_ref, mu_ref[1:2, :])
    v = shifted(xv_ref, pv_ref, mu_ref[2:3, :])
    xl = shifted(xl_ref, pl_ref, mul_ref[...])

    w0, a0, k_k, k_a = vec_ref[0:1, :], vec_ref[1:2, :], vec_ref[2:3, :], vec_ref[3:4, :]
    r_k, gn_w, gn_b = vec_ref[4:5, :], vec_ref[5:6, :], vec_ref[6:7, :]

    x_wa = xl[:, :LANES]
    wl = w0 + _mm(jnp.tanh(x_wa).astype(jnp.bfloat16), wup_ref[...])
    z = -wl
    softplus = jnp.maximum(z, 0.0) + jnp.log(1.0 + jnp.exp(-jnp.abs(z)))
    lw = -jnp.exp(-softplus - 0.5)
    a = _sigmoid(a0 + _mm(x_wa.astype(jnp.bfloat16), aup_ref[...]))
    g = _mm(_sigmoid(xl[:, LANES:LANES + GATE_PAD]).astype(jnp.bfloat16), gup_ref[...])

    tr, tc = _iota2((NS * C, NS * C), 0), _iota2((NS * C, NS * C), 1)
    tri = ((tr >= tc) & ((tr & -C) == (tc & -C))).astype(bf16)
    cl = sum(_mm(tri, piece) for piece in _bf16_pieces(lw, 3))
    e_pos = jnp.exp(cl)
    e_excl = jnp.exp(cl - lw)
    e_neg = 1.0 / e_pos
    e_end = [e_pos[(i + 1) * C - 1:(i + 1) * C, :] for i in range(NS)]
    e_tail = jnp.concatenate([e_end[i] * e_neg[i * C:(i + 1) * C, :] for i in range(NS)], axis=0)

    lane = _iota2((LANES, LANES), 1)
    row = _iota2((LANES, LANES), 0)
    same_head = (lane >= HEAD_DIM) == (row >= HEAD_DIM)
    seg = same_head.astype(bf16)
    eye = (lane == row).astype(f32)
    t_row, t_col = row & (C - 1), lane & (C - 1)
    strict = t_col < t_row
    in_block4 = (t_row >> 2) == (t_col >> 2)
    in_block16 = (t_row >> 4) == (t_col >> 4)
    incl = t_col <= t_row
    incl2 = jnp.concatenate([incl, incl], axis=1)
    lane_c = _iota2((C, LANES), 1)
    head0 = lane_c < HEAD_DIM

    def seg_sum(x):
        return sum(_mm(piece, seg) for piece in _bf16_pieces(x, 2))

    def by_head(x):
        return jnp.concatenate([jnp.where(head0, x, 0.0), jnp.where(head0, 0.0, x)], axis=0).astype(bf16)

    def twice(x):
        return jnp.concatenate([x, x], axis=0)

    tiles = range(NS * RWKV_TILES)
    sls = [slice((j % RWKV_TILES) * LANES, (j % RWKV_TILES + 1) * LANES) for j in tiles]
    seq = [j // RWKV_TILES for j in tiles]

    def tile(x, j):
        return x[seq[j] * C:(seq[j] + 1) * C, sls[j]]

    kr = [tile(kraw, j) for j in tiles]
    kkr = [kr[j] * k_k[:, sls[j]] for j in tiles]
    kk_ss = [seg_sum(kkr[j] * kkr[j]) for j in tiles]
    kk = [kkr[j] / jnp.maximum(jnp.sqrt(kk_ss[j]), 1e-12) for j in tiles]
    a_t = [tile(a, j) for j in tiles]
    k2 = [kr[j] * (1.0 + (a_t[j] - 1.0) * k_a[:, sls[j]]) for j in tiles]
    r_t = [tile(r, j) for j in tiles]
    v_t = [tile(v, j) for j in tiles]
    kka = [kk[j] * a_t[j] for j in tiles]
    alpha = [-kk[j] * tile(e_excl, j) for j in tiles]
    r_dec = [r_t[j] * tile(e_pos, j) for j in tiles]
    beta = [kka[j] * tile(e_neg, j) for j in tiles]
    k_neg = [k2[j] * tile(e_neg, j) for j in tiles]
    s_old = [s_ref[j] for j in tiles]

    gram = [_mm_nt(jnp.concatenate([by_head(alpha[j]), by_head(r_dec[j])], axis=0),
                   jnp.concatenate([by_head(beta[j]), by_head(k_neg[j])], axis=0)) for j in tiles]
    m1 = [_mm_nt(jnp.concatenate([alpha[j], r_dec[j]], axis=0).astype(bf16), s_old[j].astype(bf16))
          for j in tiles]
    a_s = [jnp.where(strict, gram[j][:LANES, :LANES], 0.0) for j in tiles]
    b_s = [jnp.where(strict, gram[j][:LANES, LANES:], 0.0).astype(bf16) for j in tiles]
    r_i = [jnp.where(incl2, gram[j][LANES:, :], 0.0).astype(bf16) for j in tiles]
    vv = [twice(v_t[j]).astype(bf16) for j in tiles]
    rhs = [twice(m1[j][:C]) + _mm(b_s[j], vv[j]) for j in tiles]
    def mmb(x, y):
        return _mm(x.astype(bf16), y.astype(bf16))

    def inv_index4(nil):
        sq = [mmb(nil[j], nil[j]) for j in tiles]
        return [eye + nil[j] + sq[j] + mmb(sq[j], nil[j]) for j in tiles]

    t0 = inv_index4([jnp.where(in_block4, a_s[j], 0.0) for j in tiles])
    n1 = [mmb(t0[j], jnp.where(in_block16 & ~in_block4, a_s[j], 0.0)) for j in tiles]
    p1 = inv_index4(n1)
    t1 = [mmb(p1[j], t0[j]) for j in tiles]
    n2 = [mmb(t1[j], jnp.where(in_block16, 0.0, a_s[j])) for j in tiles]
    p2 = inv_index4(n2)
    t1_rhs = [mmb(t1[j], rhs[j]) for j in tiles]
    u_st = [mmb(p2[j], t1_rhs[j]) for j in tiles]
    y_st = [twice(m1[j][C:]) + _mm(r_i[j], jnp.concatenate([u_st[j].astype(bf16), vv[j]], axis=0))
            for j in tiles]
    u_t = [jnp.where(head0, u_st[j][:C], u_st[j][C:]) for j in tiles]
    y_t = [jnp.where(head0, y_st[j][:C], y_st[j][C:]) for j in tiles]

    upd = [_mm_tn(jnp.concatenate([u_t[j], v_t[j]], axis=0).astype(bf16),
                  jnp.concatenate([kka[j] * tile(e_tail, j), k2[j] * tile(e_tail, j)], axis=0).astype(bf16))
           for j in tiles]
    for j in tiles:
        s_ref[j] = s_old[j] * e_end[seq[j]][:, sls[j]] + jnp.where(same_head, upd[j], 0.0)

    mu_y = [seg_sum(y_t[j]) * (1.0 / HEAD_DIM) for j in tiles]
    dev = [y_t[j] - mu_y[j] for j in tiles]
    var = [seg_sum(dev[j] * dev[j]) * (1.0 / HEAD_DIM) for j in tiles]
    bonus = [seg_sum(r_t[j] * k2[j] * r_k[:, sls[j]]) * v_t[j] for j in tiles]
    outs = [(dev[j] * lax.rsqrt(var[j] + GN_EPS) * gn_w[:, sls[j]] + gn_b[:, sls[j]] + bonus[j]) * tile(g, j)
            for j in tiles]
    for i in range(NS):
        o_ref[i] = jnp.concatenate(outs[i * RWKV_TILES:(i + 1) * RWKV_TILES], axis=1).astype(o_ref.dtype)


def rwkv_params(shift_mu, w0, w_up, a0, a_up, g_up, k_k, k_a, r_k, gn_w, gn_b):
    W = RWKV_WIDTH
    n_lora = DECAY_LORA + AAA_LORA + GATE_LORA
    mu_rkv = shift_mu[:3 * W].reshape(3, W)
    mu_l = jnp.pad(shift_mu[3 * W:], (0, LORA_PAD - n_lora)).reshape(1, LORA_PAD)
    vecs = jnp.stack([w0, a0, k_k, k_a, r_k.reshape(W), gn_w, gn_b, jnp.zeros_like(w0)])
    w_up_p = jnp.pad(w_up, ((0, LANES - DECAY_LORA), (0, 0))).astype(jnp.bfloat16)
    a_up_p = jnp.pad(a_up, ((DECAY_LORA, LANES - DECAY_LORA - AAA_LORA), (0, 0))).astype(jnp.bfloat16)
    g_up_p = jnp.pad(g_up, ((0, GATE_PAD - GATE_LORA), (0, 0))).astype(jnp.bfloat16)
    return mu_rkv, mu_l, vecs, w_up_p, a_up_p, g_up_p


def rwkv7_mix(p, B, LP, col_r, col_k, col_v, col_l, mu_rkv, mu_l, vecs, w_up, a_up, g_up):
    C = RWKV_CHUNK
    NS = RWKV_SEQS_PER_STEP
    assert B % NS == 0 and LP % C == 0, (B, LP)
    nc = LP // C
    W = RWKV_WIDTH
    p3 = p.reshape(B, LP, p.shape[1])

    def col_spec(width, col):
        return pl.BlockSpec((NS, C, width), lambda b, c: (b, c, col // width))

    def full(shape):
        return pl.BlockSpec(shape, lambda b, c: (0,) * len(shape))

    def carried_row(width):
        return pltpu.VMEM((NS, 1, width), jnp.float32)

    out = pl.pallas_call(
        _rwkv_kernel,
        grid=(B // NS, nc),
        in_specs=[
            col_spec(W, col_r), col_spec(W, col_k), col_spec(W, col_v), col_spec(LORA_PAD, col_l),
            full(mu_rkv.shape), full(mu_l.shape), full(vecs.shape),
            full(w_up.shape), full(a_up.shape), full(g_up.shape),
        ],
        out_specs=pl.BlockSpec((NS, C, W), lambda b, c: (b, c, 0)),
        out_shape=jax.ShapeDtypeStruct((B, LP, W), jnp.bfloat16),
        scratch_shapes=[
            pltpu.VMEM((NS * RWKV_TILES, LANES, LANES), jnp.float32),
            carried_row(W), carried_row(W), carried_row(W), carried_row(LORA_PAD),
        ],
        compiler_params=pltpu.CompilerParams(
            dimension_semantics=("arbitrary", "arbitrary"), vmem_limit_bytes=VMEM_LIMIT_BYTES),
        name="rwkv7_mix",
    )(p3, p3, p3, p3, mu_rkv, mu_l, vecs, w_up, a_up, g_up)
    return out.reshape(B * LP, W)


PEER_PICKS = PEER_HEADS * PEER_TOPK
PEER_TOKENS_PER_STEP = 128
PEER_GROUP = 4
PEER_GROUPS_PER_STEP = PEER_TOKENS_PER_STEP // PEER_GROUP
PEER_BUFFERS = 4
EXPERT_TILE_ROWS = 16
EXPERT_TILE_PITCH = 36
PEER_VMEM_LIMIT_BYTES = 56 * 1024 * 1024
assert PEER_GROUPS_PER_STEP % PEER_BUFFERS == 0


def expert_table(down, up):
    E, D = down.shape
    shape = (E, EXPERT_TILE_ROWS, D // EXPERT_TILE_ROWS)
    return jnp.concatenate([down.reshape(shape), up.reshape(shape)], axis=1)


def _peer_mix_kernel(idx_ref, idx_next_ref, gate_t_ref, res_ref, g_ref, pos_ref, tab_ref, o_ref, *scratch):
    bufs, sem, x_ref = scratch[:PEER_BUFFERS], scratch[PEER_BUFFERS], scratch[PEER_BUFFERS + 1]
    G = PEER_GROUP
    ahead = PEER_BUFFERS - 1
    block = pl.program_id(0)
    gate_t = gate_t_ref[0]
    tok_lane = _iota2(gate_t.shape, 1)

    def issue_token(picks_ref, group, s, b):
        token_picks = picks_ref.at[0, group * G + s]
        for j in range(PEER_PICKS):
            e = token_picks[j]
            tile_rows = pl.ds(j * EXPERT_TILE_PITCH, 2 * EXPERT_TILE_ROWS)
            pltpu.make_async_copy(tab_ref.at[e], bufs[b].at[s, tile_rows, :], sem.at[b]).start(priority=j % 2)

    def wait_group(b):
        view = bufs[b].at[:, pl.ds(0, PEER_PICKS * 2 * EXPERT_TILE_ROWS), :]
        pltpu.make_async_copy(view, view, sem.at[b]).wait()

    def tile_row(b, s, r):
        return bufs[b][s, pl.ds(r, PEER_PICKS, stride=EXPERT_TILE_PITCH), :]

    def mix_token(group, s, b):
        tt = group * G + s
        seg_w = x_ref.shape[1] // EXPERT_TILE_ROWS
        x_row = x_ref[pl.ds(tt, 1), :]
        acc = jnp.zeros((PEER_PICKS, seg_w), jnp.float32)
        for r in range(EXPERT_TILE_ROWS):
            acc = acc + tile_row(b, s, r) * x_row[:, r * seg_w:(r + 1) * seg_w]
        h = jnp.sum(acc, axis=1, keepdims=True)
        gate = jnp.sum(jnp.where(tok_lane == tt, gate_t, 0.0), axis=1, keepdims=True)
        w = 0.5 * h * (1.0 + lax.erf(h * (2.0 ** -0.5))) * gate
        mix = [jnp.sum(w * tile_row(b, s, EXPERT_TILE_ROWS + r), axis=0, keepdims=True)
               for r in range(EXPERT_TILE_ROWS)]
        out = res_ref[pl.ds(tt, 1), :] + jnp.concatenate(mix, axis=1)
        o_ref[pl.ds(tt, 1), :] = jnp.where(pos_ref[block] + tt >= PAD_LEFT, out, 0.0)

    def step(group, b):
        wait_group(b)
        target = (b + ahead) % PEER_BUFFERS

        def token_pair(i, carry):
            for s in (2 * i, 2 * i + 1):
                if isinstance(group, int) and group + ahead >= PEER_GROUPS_PER_STEP:
                    @pl.when(block + 1 < pl.num_programs(0))
                    def _():
                        issue_token(idx_next_ref, group + ahead - PEER_GROUPS_PER_STEP, s, target)
                else:
                    issue_token(idx_ref, group + ahead, s, target)
            for s in (2 * i, 2 * i + 1):
                mix_token(group, s, b)
            return carry

        lax.fori_loop(0, G // 2, token_pair, 0)

    @pl.when(block == 0)
    def _():
        for g in range(ahead):
            for s in range(G):
                issue_token(idx_ref, g, s, g)

    res = res_ref[...]
    x_ref[...] = res * lax.rsqrt(jnp.mean(res * res, axis=1, keepdims=True) + RMS_EPS) * g_ref[...]
    n_main = (PEER_GROUPS_PER_STEP - ahead) // PEER_BUFFERS

    def rotation(it, carry):
        for b in range(PEER_BUFFERS):
            step(it * PEER_BUFFERS + b, b)
        return carry

    lax.fori_loop(0, n_main, rotation, 0)
    for g in range(n_main * PEER_BUFFERS, PEER_GROUPS_PER_STEP):
        step(g, g % PEER_BUFFERS)


def peer_mix(idx, gate, res, norm_g, table, src_block, seq_pos):
    D = res.shape[1]
    n = idx.shape[0]
    tb = PEER_TOKENS_PER_STEP
    tile = (EXPERT_TILE_ROWS, D // EXPERT_TILE_ROWS)
    return pl.pallas_call(
        _peer_mix_kernel,
        grid=(n,),
        in_specs=[
            pl.BlockSpec((1, tb, PEER_PICKS), lambda i: (i, 0, 0), memory_space=pltpu.SMEM),
            pl.BlockSpec((1, tb, PEER_PICKS), lambda i: (jnp.minimum(i + 1, n - 1), 0, 0),
                         memory_space=pltpu.SMEM),
            pl.BlockSpec((1, PEER_PICKS, tb), lambda i: (i, 0, 0)),
            pl.BlockSpec((tb, D), lambda i: (src_block(i), 0)),
            pl.BlockSpec((1, D), lambda i: (0, 0)),
            pl.BlockSpec(memory_space=pltpu.SMEM),
            pl.BlockSpec(memory_space=pl.ANY),
        ],
        out_specs=pl.BlockSpec((tb, D), lambda i: (i, 0)),
        out_shape=jax.ShapeDtypeStruct((n * tb, D), jnp.float32),
        scratch_shapes=[pltpu.VMEM((PEER_GROUP, PEER_PICKS * EXPERT_TILE_PITCH, tile[1]), jnp.float32)
                        for _ in range(PEER_BUFFERS)]
        + [pltpu.SemaphoreType.DMA((PEER_BUFFERS,)), pltpu.VMEM((tb, D), jnp.float32)],
        compiler_params=pltpu.CompilerParams(
            dimension_semantics=("arbitrary",), vmem_limit_bytes=PEER_VMEM_LIMIT_BYTES),
        name="peer_mix",
    )(idx, idx, gate, res, norm_g.reshape(1, D), seq_pos, table)


def _top_rows(s, k, codes):
    vals, poss = [], []
    for _ in range(k):
        m = jnp.max(s, axis=0, keepdims=True)
        pos = jnp.min(jnp.where(s == m, codes, jnp.inf), axis=0, keepdims=True)
        vals.append(m)
        poss.append(pos)
        s = jnp.where(codes == pos, -jnp.inf, s)
    return jnp.concatenate(vals, axis=0), jnp.concatenate(poss, axis=0)


def _candidate_pairs(k):
    return [(i, j) for i in range(k) for j in range(k) if (i + 1) * (j + 1) <= k]


def candidate_selectors():
    pairs = _candidate_pairs(PEER_TOPK)
    n_rows = -(-len(pairs) // 8) * 8
    first = jnp.zeros((n_rows, PEER_TOPK), jnp.float32).at[
        jnp.arange(len(pairs)), jnp.array([i for i, _ in pairs])].set(1.0)
    second = jnp.zeros((n_rows, PEER_TOPK), jnp.float32).at[
        jnp.arange(len(pairs)), jnp.array([j for _, j in pairs])].set(1.0)
    return first.astype(jnp.bfloat16), second.astype(jnp.bfloat16)


def _pick_rows(table, sel):
    out = jnp.zeros_like(sel)
    for r in range(table.shape[0]):
        out = out + jnp.where(sel == float(r), table[r:r + 1, :], 0.0)
    return out


def _peer_select_kernel(q_ref, keys_ref, sel1_ref, sel2_ref, idx_ref, gate_ref):
    K = PEER_TOPK
    half_w = D_KEY // 2
    tokens = q_ref.shape[0]
    key_rows = _iota2((N_KEYS, tokens), 0).astype(jnp.float32)
    sel1, sel2 = sel1_ref[...], sel2_ref[...]
    n_cand = len(_candidate_pairs(K))
    rank = _iota2((K, tokens), 0).astype(jnp.bfloat16)
    cand_rows = _iota2((sel1.shape[0], tokens), 0)
    cand_code = jnp.where(cand_rows < n_cand, _mm(sel1, rank) * K + _mm(sel2, rank), float(K * K))

    def pick_exact(sel, x):
        return sum(_mm(sel, piece) for piece in _bf16_pieces(x, 3))

    idx_rows, gate_rows = [], []
    for h in range(PEER_HEADS):
        tops = []
        for c in range(2):
            hc = 2 * h + c
            scores = _mm_nt(keys_ref[hc], q_ref[:, hc * half_w:(hc + 1) * half_w])
            tops.append(_top_rows(scores, K, key_rows))
        (s1, i1), (s2, i2) = tops
        cand = jnp.where(cand_rows < n_cand, pick_exact(sel1, s1) + pick_exact(sel2, s2), -jnp.inf)
        top_s, pos = _top_rows(cand, K, cand_code)
        first = jnp.floor(pos * (1.0 / K))
        second = pos - first * K
        expert = _pick_rows(i1, first) * N_KEYS + _pick_rows(i2, second)
        e = jnp.exp(top_s - top_s[0:1, :])
        idx_rows.append(expert.astype(jnp.int32))
        gate_rows.append(e / jnp.sum(e, axis=0, keepdims=True))
    idx_ref[0] = jnp.concatenate(idx_rows, axis=0).T
    gate_ref[0] = jnp.concatenate(gate_rows, axis=0)


def peer_select(q, keys, n, src_block):
    W = q.shape[1]
    tb = PEER_TOKENS_PER_STEP
    out_block = pl.BlockSpec((1, PEER_PICKS, tb), lambda i: (i, 0, 0))
    sel1, sel2 = candidate_selectors()
    return pl.pallas_call(
        _peer_select_kernel,
        grid=(n,),
        in_specs=[pl.BlockSpec((tb, W), lambda i: (src_block(i), 0)),
                  pl.BlockSpec(keys.shape, lambda i: (0, 0, 0)),
                  pl.BlockSpec(sel1.shape, lambda i: (0, 0)),
                  pl.BlockSpec(sel2.shape, lambda i: (0, 0))],
        out_specs=[pl.BlockSpec((1, tb, PEER_PICKS), lambda i: (i, 0, 0)), out_block],
        out_shape=[jax.ShapeDtypeStruct((n, tb, PEER_PICKS), jnp.int32),
                   jax.ShapeDtypeStruct((n, PEER_PICKS, tb), jnp.float32)],
        compiler_params=pltpu.CompilerParams(
            dimension_semantics=("arbitrary",), vmem_limit_bytes=VMEM_LIMIT_BYTES),
        name="peer_select",
    )(q, keys, sel1, sel2)


ATTN_TILES = ATTN_WIDTH // LANES
Q_HEADS_PER_KV = ATTN_Q_HEADS // ATTN_KV_HEADS


def _attn_kernel(q_ref, kvp_ref, kvc_ref, qg_ref, kg_ref, sink_ref, o_ref):
    f32, bf16 = jnp.float32, jnp.bfloat16
    nb = pl.program_id(1)
    lane = _iota2((LANES, LANES), 1)
    row = _iota2((LANES, LANES), 0)
    seg = ((lane >= HEAD_DIM) == (row >= HEAD_DIM)).astype(bf16)

    def head_rms(x, gain):
        ms = sum(_mm(piece, seg) for piece in _bf16_pieces(x * x, 2)) * (1.0 / HEAD_DIM)
        return x * lax.rsqrt(ms + RMS_EPS) * gain

    kv = jnp.concatenate([kvp_ref[...], kvc_ref[...]], axis=0)
    k = head_rms(kv[:, :LANES], kg_ref[...])
    v = kv[:, LANES:]
    lane_k = _iota2(k.shape, 1)
    k_sw, v_sw = pltpu.roll(k, HEAD_DIM, 1), pltpu.roll(v, HEAD_DIM, 1)
    k_dup = [jnp.where(lane_k < HEAD_DIM, k, k_sw).astype(bf16), jnp.where(lane_k < HEAD_DIM, k_sw, k).astype(bf16)]
    v_dup = [jnp.where(lane_k < HEAD_DIM, v, v_sw).astype(bf16), jnp.where(lane_k < HEAD_DIM, v_sw, v).astype(bf16)]

    qi = _iota2((2 * BLOCK, 2 * BLOCK), 0) & (BLOCK - 1)
    ks = _iota2((2 * BLOCK, 2 * BLOCK), 1)
    dist = qi + BLOCK - ks
    key_pos = (nb - 1) * BLOCK + ks
    allowed = (dist >= 0) & (dist < WINDOW) & (key_pos >= PAD_LEFT)
    upper = _iota2((2 * BLOCK, 1), 0) < BLOCK
    head0 = _iota2((BLOCK, LANES), 1) < HEAD_DIM

    tiles = range(ATTN_TILES)
    q_t = [head_rms(q_ref[:, t * LANES:(t + 1) * LANES], qg_ref[...]) for t in tiles]
    q2 = [jnp.concatenate([jnp.where(head0, q_t[t], 0.0), jnp.where(head0, 0.0, q_t[t])], axis=0).astype(bf16)
          for t in tiles]
    s = [_mm_nt(q2[t], k_dup[(2 * t) // Q_HEADS_PER_KV]) * ATTN_SCALE for t in tiles]
    s = [jnp.where(allowed, s[t], MASK_VALUE) for t in tiles]
    sink = [jnp.where(upper, sink_ref[2 * t], sink_ref[2 * t + 1]) for t in tiles]
    m = [jnp.maximum(jnp.max(s[t], axis=1, keepdims=True), sink[t]) for t in tiles]
    e = [jnp.exp(s[t] - m[t]) for t in tiles]
    denom = [jnp.sum(e[t], axis=1, keepdims=True) + jnp.exp(sink[t] - m[t]) for t in tiles]
    p = [(e[t] / denom[t]).astype(bf16) for t in tiles]
    o2 = [_mm(p[t], v_dup[(2 * t) // Q_HEADS_PER_KV]) for t in tiles]
    o_ref[...] = jnp.concatenate(
        [jnp.where(head0, o2[t][:BLOCK], o2[t][BLOCK:]) for t in tiles], axis=1).astype(o_ref.dtype)


def swa_sink_attention(p, B, LP, col_q, col_kv, q_gain, k_gain, sinks):
    nblk = LP // BLOCK
    kvw = 2 * KV_WIDTH
    qg = jnp.tile(q_gain, LANES // HEAD_DIM).reshape(1, LANES)
    kg = jnp.tile(k_gain, LANES // HEAD_DIM).reshape(1, LANES)
    return pl.pallas_call(
        _attn_kernel,
        grid=(B, nblk),
        in_specs=[
            pl.BlockSpec((BLOCK, ATTN_WIDTH), lambda b, n: (b * nblk + n, col_q // ATTN_WIDTH)),
            pl.BlockSpec((BLOCK, kvw), lambda b, n: (b * nblk + jnp.maximum(n - 1, 0), col_kv // kvw)),
            pl.BlockSpec((BLOCK, kvw), lambda b, n: (b * nblk + n, col_kv // kvw)),
            pl.BlockSpec((1, LANES), lambda b, n: (0, 0)),
            pl.BlockSpec((1, LANES), lambda b, n: (0, 0)),
            pl.BlockSpec(memory_space=pltpu.SMEM),
        ],
        out_specs=pl.BlockSpec((BLOCK, ATTN_WIDTH), lambda b, n: (b * nblk + n, 0)),
        out_shape=jax.ShapeDtypeStruct((B * LP, ATTN_WIDTH), jnp.bfloat16),
        compiler_params=pltpu.CompilerParams(
            dimension_semantics=("arbitrary", "arbitrary"), vmem_limit_bytes=VMEM_LIMIT_BYTES),
        name="swa_sink_attention",
    )(p, p, p, qg, kg, sinks)


def kernel(x, meta_tokens, norm1_g, w_in, shift_mu, w0, w_up, a0, a_up, g_up, k_k, k_a, r_k, gn_w, gn_b,
           q_gain, k_gain, sinks, w_out, norm2_g, peer_query, peer_sub_keys, peer_down, peer_up):
    B, S, D = x.shape
    assert w_in.shape[0] == 1, "single-layer stack"
    LP = S + BLOCK
    T = B * LP
    meta =jnp.broadcast_to(meta_tokens[None], (B, N_META, D))
    h = jnp.concatenate([jnp.zeros((B, PAD_LEFT, D), x.dtype), meta, x], axis=1)
    ht = h.reshape(T, D)

    w = w_in[0]
    n_lora = DECAY_LORA + AAA_LORA + GATE_LORA
    rwkv0 = ATTN_COLS
    lora0 = rwkv0 + 3 * RWKV_WIDTH
    w_in_b = jnp.concatenate([
        w[:, rwkv0:lora0], w[:, :ATTN_WIDTH],
        jnp.pad(w[:, lora0:lora0 + n_lora], ((0, 0), (0, LORA_PAD - n_lora))),
        w[:, ATTN_WIDTH:ATTN_COLS]], axis=1).astype(jnp.bfloat16)
    col_q = 3 * RWKV_WIDTH
    col_l = col_q + ATTN_WIDTH
    col_kv = col_l + LORA_PAD
    n_in = col_kv + 2 * KV_WIDTH
    p = norm_matmul(ht, norm1_g[0], w_in_b, MATMUL_ROWS, n_in // 2)

    y_rwkv = rwkv7_mix(p, B, LP, 0, RWKV_WIDTH, 2 * RWKV_WIDTH, col_l,
                       *rwkv_params(shift_mu[0], w0[0], w_up[0], a0[0], a_up[0], g_up[0], k_k[0], k_a[0], r_k[0],
                                    gn_w[0], gn_b[0]))
    y_attn = swa_sink_attention(p, B, LP, col_q, col_kv, q_gain[0], k_gain[0], sinks[0])
    h2 = matmul_residual(y_rwkv, y_attn, w_out[0].astype(jnp.bfloat16), ht, MATMUL_ROWS, MATMUL_COLS)

    pq = norm_matmul(h2, norm2_g[0], peer_query[0].astype(jnp.bfloat16), MATMUL_ROWS, MATMUL_COLS, jnp.bfloat16)
    keys = peer_sub_keys[0].reshape(2 * PEER_HEADS, N_KEYS, D_KEY // 2).astype(jnp.bfloat16)
    table = expert_table(peer_down[0], peer_up[0])

    blocks_per_seq = LP // BLOCK
    kept_per_seq = blocks_per_seq - 1

    def kept_block(i):
        return (i // kept_per_seq) * blocks_per_seq + 1 + i % kept_per_seq

    idx_t, gate = peer_select(pq, keys, B * kept_per_seq, kept_block)
    seq_pos = (1 + jnp.arange(B * kept_per_seq, dtype=jnp.int32) % kept_per_seq) * BLOCK
    return peer_mix(idx_t, gate, h2, norm2_g[0], table, kept_block, seq_pos).reshape(B, S, D)
```
